```python
import jax, jax.numpy as jnp
from jax import lax
import numpy as np

D_MODEL = 1024
BATCH = 4
SEQ = 8192
DEPTH = 2
DEC_BATCH = 32
DEC_SEQ = 1
PAST_LEN = 16384
PAGE_SIZE = 128

P_DIM = 256
HEAD_DIM = 64
BRANCH_W = D_MODEL // 2
C_CONV = BRANCH_W
CONV_W = 31
H_ATT = BRANCH_W // HEAD_DIM
C_ATT = H_ATT * HEAD_DIM
ROT_DIM = HEAD_DIM // 4
ROPE_THETA = 500000.0
MOBA_BLOCK = 256
MOBA_TOPK = 3
Q_CHUNK = 64
H_RWKV = BRANCH_W // HEAD_DIM
C_RWKV = H_RWKV * HEAD_DIM
LORA_W = 64
LORA_A = 64
N_BRANCH = 3
SHIFT_W = 3 * C_RWKV + LORA_W + LORA_A
IN_SPLITS = (2 * C_CONV, C_CONV, 3 * C_ATT, C_ATT, SHIFT_W, C_RWKV, N_BRANCH * D_MODEL)
N_IN = sum(IN_SPLITS)
IN_SPLIT_IDX = tuple(sum(IN_SPLITS[:i + 1]) for i in range(len(IN_SPLITS) - 1))
RWKV_SPLITS = (C_RWKV, C_RWKV, C_RWKV, LORA_W, LORA_A)
RWKV_SPLIT_IDX = tuple(sum(RWKV_SPLITS[:i + 1]) for i in range(len(RWKV_SPLITS) - 1))
NORM_EPS = 1e-6
LN_EPS = 1e-5
GN_EPS = 64e-5

kernel_name = 'hybrid_conv_moba_rwkv7_gated_step'


def rmsnorm(x, g):
    xf = x.astype(jnp.float32)
    y = xf * lax.rsqrt(jnp.mean(xf * xf, axis=-1, keepdims=True) + NORM_EPS)
    return (y * g.astype(jnp.float32)).astype(x.dtype)


def layernorm(x, g, b):
    xf = x.astype(jnp.float32)
    mu = jnp.mean(xf, axis=-1, keepdims=True)
    var = jnp.mean(jnp.square(xf - mu), axis=-1, keepdims=True)
    return (xf - mu) * lax.rsqrt(var + LN_EPS) * g + b


def rope_partial(x, pos):
    half = ROT_DIM // 2
    inv = jnp.power(ROPE_THETA, -jnp.arange(half, dtype=jnp.float32) * (2.0 / ROT_DIM))
    ang = pos.astype(jnp.float32)[:, None] * inv[None, :]
    cos = jnp.cos(ang)[None, :, None, :]
    sin = jnp.sin(ang)[None, :, None, :]
    xf = x.astype(jnp.float32)
    x1 = xf[..., :half]
    x2 = xf[..., half:ROT_DIM]
    out = jnp.concatenate([x1 * cos - x2 * sin, x2 * cos + x1 * sin, xf[..., ROT_DIM:]], axis=-1)
    return out.astype(x.dtype)


def conv_module(u_glu, buf, conv_w, conv_b, ln_g, ln_b):
    a, gt = jnp.split(u_glu, 2, axis=-1)
    u = a * jax.nn.sigmoid(gt)
    ext = jnp.concatenate([buf.astype(u.dtype), u], axis=1)
    y = lax.conv_general_dilated(ext, conv_w[:, None, :].astype(u.dtype), (1,), 'VALID',
                                 dimension_numbers=('NWC', 'WIO', 'NWC'),
                                 feature_group_count=C_CONV)
    y = layernorm(y + conv_b, ln_g, ln_b)
    return jax.nn.silu(y), ext[:, -(CONV_W - 1):].astype(buf.dtype)


def moba_query(q, q_pos, kb, vb, kmean):
    b, h, nb, bs, _ = kb.shape
    n_q = q.shape[1]
    scale = HEAD_DIM ** -0.5
    qf = q.astype(jnp.float32)
    own = q_pos // MOBA_BLOCK
    gate = jnp.einsum('bqhd,bhnd->bhqn', qf, kmean)
    fully_past = jnp.arange(nb)[None, :] < own[:, None]
    gate = jnp.where(fully_past[None, None], gate, -jnp.inf)
    _, idx = lax.top_k(gate, MOBA_TOPK)
    valid = jnp.arange(MOBA_TOPK)[None, :] < own[:, None]
    bi = jnp.arange(b)[:, None, None, None]
    hi = jnp.arange(h)[None, :, None, None]
    k_sel = kb[bi, hi, idx]
    v_sel = vb[bi, hi, idx]
    s_sel = jnp.einsum('bqhd,bhqtkd->bhqtk', qf, k_sel) * scale
    s_sel = jnp.where(valid[None, None, :, :, None], s_sel, -jnp.inf)
    s_sel = s_sel.reshape(b, h, n_q, MOBA_TOPK * bs)
    own3 = own[None, None, :]
    k_own = kb[bi[..., 0], hi[..., 0], own3]
    v_own = vb[bi[..., 0], hi[..., 0], own3]
    s_own = jnp.einsum('bqhd,bhqkd->bhqk', qf, k_own) * scale
    key_pos = own[:, None] * MOBA_BLOCK + jnp.arange(bs)[None, :]
    s_own = jnp.where((key_pos <= q_pos[:, None])[None, None], s_own, -jnp.inf)
    p = jax.nn.softmax(jnp.concatenate([s_sel, s_own], axis=-1), axis=-1)
    p_sel = p[..., :MOBA_TOPK * bs].reshape(b, h, n_q, MOBA_TOPK, bs)
    p_own = p[..., MOBA_TOPK * bs:]
    out = (jnp.einsum('bhqtk,bhqtkd->bqhd', p_sel, v_sel)
           + jnp.einsum('bhqk,bhqkd->bqhd', p_own, v_own))
    return out.astype(q.dtype)


def moba_attend(q, q_pos, k_all, v_all):
    b, t_k = k_all.shape[0], k_all.shape[1]
    n_q = q.shape[1]
    nb = max(-(-t_k // MOBA_BLOCK), MOBA_TOPK)
    pad = nb * MOBA_BLOCK - t_k

    def to_blocks(a):
        a = jnp.pad(a, ((0, 0), (0, pad), (0, 0), (0, 0)))
        return a.reshape(b, nb, MOBA_BLOCK, H_ATT, HEAD_DIM).transpose(0, 3, 1, 2, 4)

    kb = to_blocks(k_all)
    vb = to_blocks(v_all)
    kmean = jnp.mean(kb, axis=3, dtype=jnp.float32)
    if n_q > Q_CHUNK and n_q % Q_CHUNK == 0:
        nc = n_q // Q_CHUNK
        qc = q.reshape(b, nc, Q_CHUNK, H_ATT, HEAD_DIM).transpose(1, 0, 2, 3, 4)
        pc = q_pos.reshape(nc, Q_CHUNK)
        out = lax.map(lambda a: moba_query(a[0], a[1], kb, vb, kmean), (qc, pc))
        return out.transpose(1, 0, 2, 3, 4).reshape(b, n_q, H_ATT, HEAD_DIM)
    return moba_query(q, q_pos, kb, vb, kmean)


def rwkv7_mix(sh, shift0, wkv0, mu, w0, w_lb, a0, a_lb, k_k, k_a, r_k, gn_g, gn_b):
    b, t = sh.shape[0], sh.shape[1]
    cur = sh.astype(jnp.float32)
    prev = jnp.concatenate([shift0.astype(jnp.float32)[:, None], cur[:, :-1]], axis=1)
    mixed = cur + (prev - cur) * mu
    r, k, v, xw, xa = jnp.split(mixed, RWKV_SPLIT_IDX, axis=-1)
    w_log = -jax.nn.softplus(-(w0 + jnp.tanh(xw) @ w_lb)) - 0.5
    decay = jnp.exp(-jnp.exp(w_log))
    a = jax.nn.sigmoid(a0 + xa @ a_lb)

    def heads(u):
        return u.reshape(b, t, H_RWKV, HEAD_DIM)

    kk = heads(k * k_k)
    kk = kk * lax.rsqrt(jnp.maximum(jnp.sum(kk * kk, axis=-1, keepdims=True), 1e-24))
    k = k * (1.0 + (a - 1.0) * k_a)
    r, k, v, decay, a = heads(r), heads(k), heads(v), heads(decay), heads(a)

    def step(state, inp):
        r_t, w_t, k_t, v_t, kk_t, b_t = inp
        sa = jnp.einsum('bhvk,bhk->bhv', state, -kk_t)
        state = (state * w_t[:, :, None, :] + sa[..., None] * b_t[:, :, None, :]
                 + v_t[..., None] * k_t[:, :, None, :])
        return state, jnp.einsum('bhvk,bhk->bhv', state, r_t)

    xs = tuple(jnp.moveaxis(u, 1, 0) for u in (r, decay, k, v, kk, kk * a))
    state, ys = lax.scan(step, wkv0.astype(jnp.float32), xs)
    y = jnp.moveaxis(ys, 0, 1)
    m = jnp.mean(y, axis=-1, keepdims=True)
    var = jnp.mean(jnp.square(y - m), axis=-1, keepdims=True)
    y = ((y - m) * lax.rsqrt(var + GN_EPS)).reshape(b, t, C_RWKV) * gn_g + gn_b
    bonus = jnp.sum(r * k * r_k, axis=-1, keepdims=True) * v
    y = y + bonus.reshape(b, t, C_RWKV)
    return y, state.astype(wkv0.dtype), sh[:, -1].astype(shift0.dtype)


def trunk_layer(x, pe, pos, conv_buf, wkv0, shift0, past_k, past_v, lp):
    b, t = x.shape[0], x.shape[1]
    h = rmsnorm(x, lp['norm_g'])
    z = h @ lp['w_in']
    u_glu, g_conv, qkv, g_att, sh, g_rwkv, merge = jnp.split(z, IN_SPLIT_IDX, axis=-1)
    y_conv, conv_new = conv_module(u_glu, conv_buf, lp['conv_w'], lp['conv_b'],
                                   lp['conv_ln_g'], lp['conv_ln_b'])
    y_conv = (y_conv * jax.nn.silu(g_conv)).astype(x.dtype)
    q, k, v = jnp.split(qkv, 3, axis=-1)
    q = rope_partial(q.reshape(b, t, H_ATT, HEAD_DIM), pos)
    k = rope_partial(k.reshape(b, t, H_ATT, HEAD_DIM), pos)
    v = v.reshape(b, t, H_ATT, HEAD_DIM)
    if past_k is None:
        k_all, v_all = k, v
    else:
        k_all = jnp.concatenate([past_k.astype(k.dtype), k], axis=1)
        v_all = jnp.concatenate([past_v.astype(v.dtype), v], axis=1)
    att = moba_attend(q, pos, k_all, v_all).reshape(b, t, C_ATT)
    y_att = (att * jax.nn.silu(g_att)).astype(x.dtype)
    y_rw, wkv_new, shift_new = rwkv7_mix(sh, shift0, wkv0, lp['shift_mu'], lp['w0'], lp['w_lora_b'],
                                         lp['a0'], lp['a_lora_b'], lp['k_k'], lp['k_a'],
                                         lp['r_k'], lp['gn_g'], lp['gn_b'])
    y_rw = (y_rw * jax.nn.silu(g_rwkv)).astype(x.dtype)
    ys = jnp.stack([y_conv, y_att, y_rw], axis=2)
    proj = jnp.einsum('btnw,nwd->btnd', ys, lp['w_branch'])
    gates = jax.nn.sigmoid(merge.reshape(b, t, N_BRANCH, D_MODEL))
    x = x + (jnp.sum(gates * proj, axis=2) @ lp['w_out']).astype(x.dtype)
    ple = jax.nn.sigmoid(x @ lp['ple_gate']) * (pe @ lp['ple_proj'])
    x = x + ple.astype(x.dtype)
    return x, k, v, conv_new, wkv_new, shift_new


def setup_inputs(seed: int = 0) -> dict:
    key = jax.random.key(seed)
    ks = jax.random.split(key, 32)
    n_pages = PAST_LEN // PAGE_SIZE
    n_pool = (DEC_BATCH * n_pages * 5) // 4

    def nrm(k, shape, scale):
        return jax.random.normal(k, shape, jnp.float32) * scale

    page_table = jax.random.permutation(ks[4], n_pool)[:DEC_BATCH * n_pages]
    page_table = page_table.reshape(DEC_BATCH, n_pages).astype(jnp.int32)
    return {
        'x_prompt': nrm(ks[0], (BATCH, SEQ, D_MODEL), 1.0),
        'x_sample': nrm(ks[1], (DEC_BATCH, DEC_SEQ, D_MODEL), 1.0),
        'cache_k': nrm(ks[2], (DEPTH, n_pool, PAGE_SIZE, H_ATT, HEAD_DIM), 1.0),
        'cache_v': nrm(ks[3], (DEPTH, n_pool, PAGE_SIZE, H_ATT, HEAD_DIM), 1.0),
        'page_table': page_table,
        'state_conv': nrm(ks[5], (DEPTH, DEC_BATCH, CONV_W - 1, C_CONV), 0.5),
        'state_wkv': nrm(ks[6], (DEPTH, DEC_BATCH, H_RWKV, HEAD_DIM, HEAD_DIM), 0.3),
        'state_shift': nrm(ks[7], (DEPTH, DEC_BATCH, SHIFT_W), 1.0),
        'p_prompt': nrm(ks[8], (DEPTH, BATCH, SEQ, P_DIM), 1.0),
        'p_sample': nrm(ks[9], (DEPTH, DEC_BATCH, DEC_SEQ, P_DIM), 1.0),
        'norm_g': 1.0 + nrm(ks[10], (DEPTH, D_MODEL), 0.02),
        'w_in': nrm(ks[11], (DEPTH, D_MODEL, N_IN), D_MODEL ** -0.5),
        'conv_w': nrm(ks[12], (DEPTH, CONV_W, C_CONV), CONV_W ** -0.5),
        'conv_b': nrm(ks[13], (DEPTH, C_CONV), 0.01),
        'conv_ln_g': 1.0 + nrm(ks[14], (DEPTH, C_CONV), 0.02),
        'conv_ln_b': nrm(ks[15], (DEPTH, C_CONV), 0.01),
        'shift_mu': jax.random.uniform(ks[16], (DEPTH, SHIFT_W), jnp.float32, 0.0, 1.0),
        'w0': nrm(ks[17], (DEPTH, C_RWKV), 0.5),
        'w_lora_b': nrm(ks[18], (DEPTH, LORA_W, C_RWKV), 0.1),
        'a0': nrm(ks[19], (DEPTH, C_RWKV), 0.1),
        'a_lora_b': nrm(ks[20], (DEPTH, LORA_A, C_RWKV), 0.5 * LORA_A ** -0.5),
        'k_k': 0.85 + nrm(ks[21], (DEPTH, C_RWKV), 0.02),
        'k_a': 1.0 + nrm(ks[22], (DEPTH, C_RWKV), 0.02),
        'r_k': nrm(ks[23], (DEPTH, H_RWKV, HEAD_DIM), 0.1),
        'gn_g': 1.0 + nrm(ks[24], (DEPTH, C_RWKV), 0.02),
        'gn_b': nrm(ks[25], (DEPTH, C_RWKV), 0.01),
        'w_branch': nrm(ks[26], (DEPTH, N_BRANCH, BRANCH_W, D_MODEL), BRANCH_W ** -0.5),
        'w_out': nrm(ks[27], (DEPTH, D_MODEL, D_MODEL), D_MODEL ** -0.5),
        'ple_proj': nrm(ks[28], (DEPTH, P_DIM, D_MODEL), P_DIM ** -0.5),
        'ple_gate': nrm(ks[29], (DEPTH, D_MODEL, D_MODEL), D_MODEL ** -0.5),
        'final_norm_g': 1.0 + nrm(ks[30], (D_MODEL,), 0.02),
    }


def reference(x_prompt, x_sample, cache_k, cache_v, page_table, state_conv, state_wkv, state_shift,
              p_prompt, p_sample, norm_g, w_in, conv_w, conv_b, conv_ln_g, conv_ln_b, shift_mu,
              w0, w_lora_b, a0, a_lora_b, k_k, k_a, r_k, gn_g, gn_b, w_branch, w_out,
              ple_proj, ple_gate, final_norm_g):
    b_p, t_p = x_prompt.shape[0], x_prompt.shape[1]
    b_s, t_s = x_sample.shape[0], x_sample.shape[1]
    past_len = page_table.shape[1] * cache_k.shape[2]
    pos_p = jnp.arange(t_p, dtype=jnp.int32)
    pos_s = past_len + jnp.arange(t_s, dtype=jnp.int32)
    xp, xs = x_prompt, x_sample
    kp_l, vp_l, ks_l, vs_l, cp_l, cs_l, wp_l, ws_l, sp_l, ss_l = ([] for _ in range(10))
    for i in range(DEPTH):
        lp = dict(norm_g=norm_g[i], w_in=w_in[i], conv_w=conv_w[i], conv_b=conv_b[i],
                  conv_ln_g=conv_ln_g[i], conv_ln_b=conv_ln_b[i], shift_mu=shift_mu[i],
                  w0=w0[i], w_lora_b=w_lora_b[i], a0=a0[i], a_lora_b=a_lora_b[i],
                  k_k=k_k[i], k_a=k_a[i], r_k=r_k[i], gn_g=gn_g[i], gn_b=gn_b[i],
                  w_branch=w_branch[i], w_out=w_out[i], ple_proj=ple_proj[i], ple_gate=ple_gate[i])
        xp, kp, vp, cp, wp, spn = trunk_layer(
            xp, p_prompt[i], pos_p,
            jnp.zeros((b_p, CONV_W - 1, C_CONV), state_conv.dtype),
            jnp.zeros((b_p, H_RWKV, HEAD_DIM, HEAD_DIM), state_wkv.dtype),
            jnp.zeros((b_p, SHIFT_W), state_shift.dtype),
            None, None, lp)
        past_k = cache_k[i][page_table].reshape(b_s, past_len, H_ATT, HEAD_DIM)
        past_v = cache_v[i][page_table].reshape(b_s, past_len, H_ATT, HEAD_DIM)
        xs, ksn, vsn, csn, wsn, ssn = trunk_layer(
            xs, p_sample[i], pos_s, state_conv[i], state_wkv[i], state_shift[i],
            past_k, past_v, lp)
        kp_l.append(kp); vp_l.append(vp); ks_l.append(ksn); vs_l.append(vsn)
        cp_l.append(cp); cs_l.append(csn); wp_l.append(wp); ws_l.append(wsn)
        sp_l.append(spn); ss_l.append(ssn)
    y_prompt = rmsnorm(xp, final_norm_g)
    y_sample = rmsnorm(xs, final_norm_g)
    k_new_prompt = jnp.stack(kp_l)
    v_new_prompt = jnp.stack(vp_l)
    k_new_sample = jnp.stack(ks_l)
    v_new_sample = jnp.stack(vs_l)
    conv_prompt = jnp.stack(cp_l)
    conv_sample = jnp.stack(cs_l)
    wkv_prompt = jnp.stack(wp_l)
    wkv_sample = jnp.stack(ws_l)
    shift_prompt = jnp.stack(sp_l)
    shift_sample = jnp.stack(ss_l)
    return (y_prompt, y_sample, k_new_prompt, v_new_prompt, k_new_sample, v_new_sample,
            conv_prompt, conv_sample, wkv_prompt, wkv_sample, shift_prompt, shift_sample)
```

```python
import functools

import jax
import jax.numpy as jnp
from jax import lax
from jax.experimental import pallas as pl
from jax.experimental.pallas import tpu as pltpu

D_MODEL = 1024
P_DIM = 256
HEAD_DIM = 64
BRANCH_W = 512
N_HEAD = 8
N_PAIR = 4
CONV_W = 31
ROT_DIM = 16
ROPE_THETA = 500000.0
MOBA_BLOCK = 256
MOBA_TOPK = 3
LORA = 64
SHIFT_W = 3 * BRANCH_W + 2 * LORA
N_CONV_IN = 3 * BRANCH_W
N_ATT_IN = 4 * BRANCH_W
N_RWKV_IN = SHIFT_W + BRANCH_W
N_MERGE_IN = 3 * D_MODEL
N_IN = N_CONV_IN + N_ATT_IN + N_RWKV_IN + N_MERGE_IN
NORM_EPS = 1e-6
LN_EPS = 1e-5
GN_EPS = 64e-5
NEG = -1e30

LANE = 128
TOK_TILE = 256
CHUNK = 64
HIST = 32
CONV_ROWS = 32
PAGES_PER_STEP = 8
VMEM_LIMIT = 56 * 1024 * 1024

F32 = jnp.float32
BF16 = jnp.bfloat16


def _bdot(a, b):
    return jnp.dot(a.astype(BF16), b.astype(BF16), preferred_element_type=F32)


def _bdot_nt(a, b):
    return lax.dot_general(a.astype(BF16), b.astype(BF16), (((1,), (1,)), ((), ())),
                           preferred_element_type=F32)


def _split_dot(x, m_bf16):
    hi = x.astype(BF16)
    lo = (x - hi.astype(F32)).astype(BF16)
    return (jnp.dot(hi, m_bf16, preferred_element_type=F32)
            + jnp.dot(lo, m_bf16, preferred_element_type=F32))


def _split3_dot_left(m_bf16, x):
    hi = x.astype(BF16)
    r1 = x - hi.astype(F32)
    mid = r1.astype(BF16)
    lo = (r1 - mid.astype(F32)).astype(BF16)
    return (jnp.dot(m_bf16, hi, preferred_element_type=F32)
            + jnp.dot(m_bf16, mid, preferred_element_type=F32)
            + jnp.dot(m_bf16, lo, preferred_element_type=F32))


def _sigmoid(x):
    return 1.0 / (1.0 + jnp.exp(-x))


def _silu(x):
    return x * _sigmoid(x)


def _softplus(x):
    return jnp.maximum(x, 0.0) + jnp.log(1.0 + jnp.exp(-jnp.abs(x)))


def _rms(x, g):
    return x * lax.rsqrt(jnp.mean(x * x, axis=-1, keepdims=True) + NORM_EPS) * g


def _params(*sem):
    return pltpu.CompilerParams(dimension_semantics=sem, vmem_limit_bytes=VMEM_LIMIT)


def _full(shape):
    nd = len(shape)
    return pl.BlockSpec(shape, lambda *_: (0,) * nd)


def _conv_branch(ext_ref, base, rows, cw_ref, cb, lg, lb, g_conv):
    acc = jnp.zeros((rows, BRANCH_W), F32) + cb
    for j in range(CONV_W):
        start = base - (CONV_W - 1) + j
        acc = acc + ext_ref[start:start + rows, :] * cw_ref[j:j + 1, :]
    mu = jnp.mean(acc, axis=-1, keepdims=True)
    d = acc - mu
    var = jnp.mean(d * d, axis=-1, keepdims=True)
    y = d * lax.rsqrt(var + LN_EPS) * lg + lb
    return (_silu(y) * _silu(g_conv)).astype(BF16)


def _conv_kernel(x_ref, g_ref, w_ref, cw_ref, cb_ref, lg_ref, lb_ref, y_ref, cs_ref, z_scr, ext_scr):
    tt = x_ref.shape[0]

    @pl.when(pl.program_id(1) == 0)
    def _():
        ext_scr[0:HIST, :] = jnp.zeros((HIST, BRANCH_W), F32)

    z_scr[...] = _bdot(_rms(x_ref[...], g_ref[...]), w_ref[...])
    ext_scr[HIST:HIST + tt, :] = z_scr[:, 0:BRANCH_W] * _sigmoid(z_scr[:, BRANCH_W:2 * BRANCH_W])
    cb, lg, lb = cb_ref[...], lg_ref[...], lb_ref[...]
    for r in range(0, tt, CONV_ROWS):
        y_ref[r:r + CONV_ROWS, :] = _conv_branch(
            ext_scr, HIST + r, CONV_ROWS, cw_ref, cb, lg, lb,
            z_scr[r:r + CONV_ROWS, 2 * BRANCH_W:3 * BRANCH_W])
    cs_ref[...] = ext_scr[HIST + tt - (CONV_W - 1):HIST + tt, :]
    ext_scr[0:HIST, :] = ext_scr[tt:tt + HIST, :]


def _conv_call(x, norm_g, w_c, conv_w, conv_b, ln_g, ln_b):
    b, t, _ = x.shape
    tt = TOK_TILE
    return pl.pallas_call(
        _conv_kernel,
        grid=(b, t // tt),
        in_specs=[
            pl.BlockSpec((None, tt, D_MODEL), lambda i, j: (i, j, 0)),
            _full((1, D_MODEL)), _full((D_MODEL, N_CONV_IN)), _full((CONV_W, BRANCH_W)),
            _full((1, BRANCH_W)), _full((1, BRANCH_W)), _full((1, BRANCH_W)),
        ],
        out_specs=[
            pl.BlockSpec((None, tt, BRANCH_W), lambda i, j: (i, j, 0)),
            pl.BlockSpec((None, CONV_W - 1, BRANCH_W), lambda i, j: (i, 0, 0)),
        ],
        out_shape=[
            jax.ShapeDtypeStruct((b, t, BRANCH_W), BF16),
            jax.ShapeDtypeStruct((b, CONV_W - 1, BRANCH_W), F32),
        ],
        scratch_shapes=[pltpu.VMEM((tt, N_CONV_IN), F32), pltpu.VMEM((HIST + tt, BRANCH_W), F32)],
        compiler_params=_params("parallel", "arbitrary"),
        name="conv_branch",
    )(x, norm_g, w_c, conv_w, conv_b, ln_g, ln_b)


def _rope_tables(pos):
    half = ROT_DIM // 2
    inv = jnp.power(ROPE_THETA, -jnp.arange(half, dtype=F32) * (2.0 / ROT_DIM))
    ang = pos.astype(F32)[:, None] * inv[None, :]
    cos, sin = jnp.cos(ang), jnp.sin(ang)
    n = pos.shape[0]
    pad = jnp.zeros((n, HEAD_DIM - ROT_DIM), F32)
    cos64 = jnp.concatenate([cos, cos, pad + 1.0], axis=1)
    sa64 = jnp.concatenate([-sin, jnp.zeros_like(sin), pad], axis=1)
    sb64 = jnp.concatenate([jnp.zeros_like(sin), sin, pad], axis=1)
    return tuple(jnp.concatenate([a, a], axis=1) for a in (cos64, sa64, sb64))


def _rope(a, cos, sa, sb):
    w = a.shape[1]
    return a * cos + pltpu.roll(a, w - ROT_DIM // 2, 1) * sa + pltpu.roll(a, ROT_DIM // 2, 1) * sb


def _tile_lanes(a, n):
    return jnp.concatenate([a] * n, axis=1)


def _att1_kernel(x_ref, g_ref, w_ref, cos_ref, sa_ref, sb_ref,
                 k_ref, v_ref, qt_ref, kb_ref, vt_ref, sgt_ref, km_ref, z_scr):
    z_scr[...] = _bdot(_rms(x_ref[...], g_ref[...]), w_ref[...])
    n = BRANCH_W // LANE
    cos, sa, sb = (_tile_lanes(r[...], n) for r in (cos_ref, sa_ref, sb_ref))
    q = _rope(z_scr[:, 0:BRANCH_W], cos, sa, sb) * (HEAD_DIM ** -0.5)
    k = _rope(z_scr[:, BRANCH_W:2 * BRANCH_W], cos, sa, sb)
    v = z_scr[:, 2 * BRANCH_W:3 * BRANCH_W]
    k_ref[...] = k
    v_ref[...] = v
    qt_ref[...] = q.T.astype(BF16)
    for p in range(N_PAIR):
        kb_ref[p] = k[:, p * LANE:(p + 1) * LANE].astype(BF16)
    vt_ref[...] = v.T.astype(BF16)
    sgt_ref[...] = _silu(z_scr[:, 3 * BRANCH_W:4 * BRANCH_W]).T
    km_ref[...] = jnp.mean(k, axis=0, keepdims=True)


def _att1_call(x, norm_g, w_a, tables):
    b, t, _ = x.shape
    tt = TOK_TILE
    nb = t // tt
    tok = lambda i, j: (i, j, 0)
    tr = lambda i, j: (i, 0, j)
    return pl.pallas_call(
        _att1_kernel,
        grid=(b, nb),
        in_specs=[
            pl.BlockSpec((None, tt, D_MODEL), tok),
            _full((1, D_MODEL)), _full((D_MODEL, N_ATT_IN)),
            pl.BlockSpec((tt, LANE), lambda i, j: (j, 0)),
            pl.BlockSpec((tt, LANE), lambda i, j: (j, 0)),
            pl.BlockSpec((tt, LANE), lambda i, j: (j, 0)),
        ],
        out_specs=[
            pl.BlockSpec((None, tt, BRANCH_W), tok),
            pl.BlockSpec((None, tt, BRANCH_W), tok),
            pl.BlockSpec((None, BRANCH_W, tt), tr),
            pl.BlockSpec((None, N_PAIR, tt, LANE), lambda i, j: (i, 0, j, 0)),
            pl.BlockSpec((None, None, BRANCH_W, tt), lambda i, j: (i, j, 0, 0)),
            pl.BlockSpec((None, BRANCH_W, tt), tr),
            pl.BlockSpec((None, None, 1, BRANCH_W), lambda i, j: (i, j, 0, 0)),
        ],
        out_shape=[
            jax.ShapeDtypeStruct((b, t, BRANCH_W), F32),
            jax.ShapeDtypeStruct((b, t, BRANCH_W), F32),
            jax.ShapeDtypeStruct((b, BRANCH_W, t), BF16),
            jax.ShapeDtypeStruct((b, N_PAIR, t, LANE), BF16),
            jax.ShapeDtypeStruct((b, nb, BRANCH_W, tt), BF16),
            jax.ShapeDtypeStruct((b, BRANCH_W, t), F32),
            jax.ShapeDtypeStruct((b, nb, 1, BRANCH_W), F32),
        ],
        scratch_shapes=[pltpu.VMEM((tt, N_ATT_IN), F32)],
        compiler_params=_params("parallel", "parallel"),
        name="att_project",
    )(x, norm_g, w_a, *tables)


def _top_blocks(gate, n_slots):
    nb = gate.shape[0]
    row = lax.broadcasted_iota(jnp.int32, gate.shape, 0).astype(F32)
    sel = jnp.zeros(gate.shape, F32)
    for j in range(MOBA_TOPK):
        m = jnp.max(gate, axis=0, keepdims=True)
        idx = jnp.min(jnp.where(gate == m, row, float(nb)), axis=0, keepdims=True)
        hit = row == jnp.where(j < n_slots, idx, -1.0)
        sel = jnp.where(hit, 1.0, sel)
        gate = jnp.where(row == idx, -jnp.inf, gate)
    return sel


def _att2_kernel(qt_ref, kb_ref, vt_ref, km_ref, sgt_ref, yt_ref, sel_scr):
    i = pl.program_id(1)
    h = pl.program_id(2)
    p = h // 2
    tq = qt_ref.shape[1]
    nb = km_ref.shape[1]
    row = lax.broadcasted_iota(jnp.int32, (LANE, tq), 0)
    in_head = (row >= HEAD_DIM) == (h % 2 == 1)
    qpad = jnp.where(in_head, qt_ref[...].astype(F32), 0.0).astype(BF16)
    v_row = pl.multiple_of(h * HEAD_DIM, HEAD_DIM)

    gate = jnp.dot(km_ref[p].astype(BF16), qpad, preferred_element_type=F32)
    blk = lax.broadcasted_iota(jnp.int32, (nb, tq), 0)
    gate = jnp.where(blk < i, gate, -jnp.inf)
    sel_scr[...] = _top_blocks(gate, i)

    def scores(n):
        start = pl.multiple_of(n * MOBA_BLOCK, MOBA_BLOCK)
        return jnp.dot(kb_ref[p, pl.ds(start, MOBA_BLOCK), :], qpad, preferred_element_type=F32)

    def values(n):
        return vt_ref[n, pl.ds(v_row, HEAD_DIM), :]

    kpos = lax.broadcasted_iota(jnp.int32, (MOBA_BLOCK, tq), 0)
    qpos = lax.broadcasted_iota(jnp.int32, (MOBA_BLOCK, tq), 1)
    s = jnp.where(kpos <= qpos, scores(i), NEG)
    m0 = jnp.max(s, axis=0, keepdims=True)
    e = jnp.exp(s - m0)
    l0 = jnp.sum(e, axis=0, keepdims=True)
    acc0 = jnp.dot(values(i), e.astype(BF16), preferred_element_type=F32)

    def body(n, carry):
        m, l, acc = carry
        s = jnp.where(sel_scr[pl.ds(n, 1), :] > 0.0, scores(n), NEG)
        m_new = jnp.maximum(m, jnp.max(s, axis=0, keepdims=True))
        alpha = jnp.exp(m - m_new)
        e = jnp.exp(s - m_new)
        l = alpha * l + jnp.sum(e, axis=0, keepdims=True)
        acc = alpha * acc + jnp.dot(values(n), e.astype(BF16), preferred_element_type=F32)
        return m_new, l, acc

    _, l, acc = lax.fori_loop(0, i, body, (m0, l0, acc0))
    yt_ref[...] = acc / l * sgt_ref[...]


def _att2_call(qt, kb, vt, km, sgt):
    b, _, t = qt.shape
    nb = vt.shape[1]
    tq = MOBA_BLOCK
    return pl.pallas_call(
        _att2_kernel,
        grid=(b, t // tq, N_HEAD),
        in_specs=[
            pl.BlockSpec((None, LANE, tq), lambda bi, i, h: (bi, h // 2, i)),
            pl.BlockSpec((None, N_PAIR, t, LANE), lambda bi, i, h: (bi, 0, 0, 0)),
            pl.BlockSpec((None, nb, BRANCH_W, MOBA_BLOCK), lambda bi, i, h: (bi, 0, 0, 0)),
            pl.BlockSpec((None, N_PAIR, nb, LANE), lambda bi, i, h: (bi, 0, 0, 0)),
            pl.BlockSpec((None, HEAD_DIM, tq), lambda bi, i, h: (bi, h, i)),
        ],
        out_specs=pl.BlockSpec((None, HEAD_DIM, tq), lambda bi, i, h: (bi, h, i)),
        out_shape=jax.ShapeDtypeStruct((b, BRANCH_W, t), F32),
        scratch_shapes=[pltpu.VMEM((nb, tq), F32)],
        compiler_params=_params("parallel", "arbitrary", "arbitrary"),
        name="moba_attention",
    )(qt, kb, vt, km, sgt)


def _rwkv_columns(mixed, w0, wlb_ref, a0, alb_ref, kk_scale, k_a, bd_ref):
    r = mixed[:, 0:BRANCH_W]
    k = mixed[:, BRANCH_W:2 * BRANCH_W]
    v = mixed[:, 2 * BRANCH_W:3 * BRANCH_W]
    lora = mixed[:, 3 * BRANCH_W:3 * BRANCH_W + 2 * LORA]
    lane = lax.broadcasted_iota(jnp.int32, lora.shape, 1)
    lora = jnp.where(lane < LORA, jnp.tanh(lora), lora)
    w_log = -_softplus(-(w0 + _bdot(lora, wlb_ref[...]))) - 0.5
    log_decay = -jnp.exp(w_log)
    a_lr = _sigmoid(a0 + _bdot(lora, alb_ref[...]))
    kk = k * kk_scale
    kk = kk * lax.rsqrt(jnp.maximum(_split_dot(kk * kk, bd_ref[...]), 1e-24))
    k = k * (1.0 + (a_lr - 1.0) * k_a)
    return r, k, v, log_decay, kk, a_lr


def _rwkv_finish(y, r, k, v, r_k, gn_g, gn_b, g_rwkv, bd_ref):
    m = _split_dot(y, bd_ref[...]) * (1.0 / HEAD_DIM)
    d = y - m
    var = _split_dot(d * d, bd_ref[...]) * (1.0 / HEAD_DIM)
    yn = d * lax.rsqrt(var + GN_EPS) * gn_g + gn_b
    bonus = _split_dot(r * k * r_k, bd_ref[...]) * v
    return ((yn + bonus) * _silu(g_rwkv)).astype(BF16)


def _stack_heads(a, mask_a):
    return jnp.concatenate([jnp.where(mask_a, a, 0.0), jnp.where(mask_a, 0.0, a)], axis=0)


def _unit_lower_inverse(low):
    n = low.shape[0]
    eye = (lax.broadcasted_iota(jnp.int32, (n, n), 0)
           == lax.broadcasted_iota(jnp.int32, (n, n), 1)).astype(F32)
    x = eye + low
    power = low
    steps = CHUNK.bit_length() - 2
    for _ in range(steps):
        power = _bdot(power, power)
        x = x + _bdot(x, power)
    return x


def _rwkv_kernel(x_ref, g_ref, w_ref, mu_ref, w0_ref, wlb_ref, a0_ref, alb_ref, kk_ref, ka_ref,
                 rk_ref, gg_ref, gb_ref, bd_ref,
                 y_ref, st_ref, sh_ref,
                 z_scr, shs_scr, r_scr, k_scr, v_scr, ld_scr, a_scr, b_scr, y_scr, s_scr):
    tt = x_ref.shape[0]

    @pl.when(pl.program_id(1) == 0)
    def _():
        shs_scr[0:8, :] = jnp.zeros((8, SHIFT_W), F32)
        s_scr[...] = jnp.zeros(s_scr.shape, F32)

    z_scr[...] = _bdot(_rms(x_ref[...], g_ref[...]), w_ref[...])
    shs_scr[8:8 + tt, :] = z_scr[:, 0:SHIFT_W]
    cur = z_scr[:, 0:SHIFT_W]
    mixed = cur + (shs_scr[7:7 + tt, :] - cur) * mu_ref[...]
    sh_ref[...] = shs_scr[7 + tt:8 + tt, :]
    shs_scr[7:8, :] = shs_scr[7 + tt:8 + tt, :]

    r, k, v, log_decay, kk, a_lr = _rwkv_columns(
        mixed, w0_ref[...], wlb_ref, a0_ref[...], alb_ref, kk_ref[...], ka_ref[...], bd_ref)
    r_scr[...] = r
    k_scr[...] = k
    v_scr[...] = v
    ld_scr[...] = log_decay
    a_scr[...] = -kk
    b_scr[...] = kk * a_lr

    n2 = 2 * CHUNK
    row = lax.broadcasted_iota(jnp.int32, (n2, n2), 0)
    col = lax.broadcasted_iota(jnp.int32, (n2, n2), 1)
    same = (row >= CHUNK) == (col >= CHUNK)
    rt = jnp.where(row >= CHUNK, row - CHUNK, row)
    ct = jnp.where(col >= CHUNK, col - CHUNK, col)
    strict = same & (rt > ct)
    incl = same & (rt >= ct)
    crow = lax.broadcasted_iota(jnp.int32, (CHUNK, CHUNK), 0)
    ccol = lax.broadcasted_iota(jnp.int32, (CHUNK, CHUNK), 1)
    tri = jnp.where(crow >= ccol, 1.0, 0.0).astype(BF16)
    mask_a = lax.broadcasted_iota(jnp.int32, (CHUNK, LANE), 1) < HEAD_DIM

    def chunk(c, _):
        rows = pl.ds(pl.multiple_of(c * CHUNK, CHUNK), CHUNK)
        for p in range(N_PAIR):
            lanes = slice(p * LANE, (p + 1) * LANE)
            ld = ld_scr[rows, lanes]
            cum = _split3_dot_left(tri, ld)
            last = cum[CHUNK - 1:CHUNK, :]
            p_inc = jnp.exp(cum)
            p_inv = jnp.exp(-cum)
            p_end = jnp.exp(last - cum)
            vv = v_scr[rows, lanes]
            bb = b_scr[rows, lanes]
            kc = k_scr[rows, lanes]
            r_s = _stack_heads(r_scr[rows, lanes] * p_inc, mask_a)
            a_s = _stack_heads(a_scr[rows, lanes] * jnp.exp(cum - ld), mask_a)
            b_s = _stack_heads(bb * p_inv, mask_a)
            k_s = _stack_heads(kc * p_inv, mask_a)
            v_s = _stack_heads(vv, mask_a)
            bk_end = jnp.concatenate([_stack_heads(bb * p_end, mask_a),
                                      _stack_heads(kc * p_end, mask_a)], axis=0)
            s0 = s_scr[p]
            a_ab = jnp.where(strict, _bdot_nt(a_s, b_s), 0.0)
            a_ak = jnp.where(strict, _bdot_nt(a_s, k_s), 0.0)
            a_rb = jnp.where(incl, _bdot_nt(r_s, b_s), 0.0)
            a_rk = jnp.where(incl, _bdot_nt(r_s, k_s), 0.0)
            c_s = _bdot(_unit_lower_inverse(a_ab), _bdot_nt(a_s, s0) + _bdot(a_ak, v_s))
            y_s = _bdot_nt(r_s, s0) + _bdot(a_rb, c_s) + _bdot(a_rk, v_s)
            y_scr[rows, lanes] = y_s[0:CHUNK, :] + y_s[CHUNK:n2, :]
            cv_t = jnp.concatenate([c_s, v_s], axis=0).T
            s_scr[p] = s0 * jnp.exp(last) + _bdot(cv_t, bk_end)
        return 0

    lax.fori_loop(0, tt // CHUNK, chunk, 0)

    y_ref[...] = _rwkv_finish(y_scr[...], r_scr[...], k_scr[...], v_scr[...], rk_ref[...],
                              gg_ref[...], gb_ref[...], z_scr[:, SHIFT_W:SHIFT_W + BRANCH_W], bd_ref)
    for p in range(N_PAIR):
        st_ref[2 * p] = s_scr[p, 0:HEAD_DIM, 0:HEAD_DIM]
        st_ref[2 * p + 1] = s_scr[p, HEAD_DIM:LANE, HEAD_DIM:LANE]


def _head_block_diag():
    i = jnp.arange(BRANCH_W) // HEAD_DIM
    return (i[:, None] == i[None, :]).astype(BF16)


def _rwkv_call(x, norm_g, w_r, lp):
    b, t, _ = x.shape
    tt = TOK_TILE
    vec = _full((1, BRANCH_W))
    scr = lambda w: pltpu.VMEM((tt, w), F32)
    return pl.pallas_call(
        _rwkv_kernel,
        grid=(b, t // tt),
        in_specs=[
            pl.BlockSpec((None, tt, D_MODEL), lambda i, j: (i, j, 0)),
            _full((1, D_MODEL)), _full((D_MODEL, N_RWKV_IN)), _full((1, SHIFT_W)),
            vec, _full((2 * LORA, BRANCH_W)), vec, _full((2 * LORA, BRANCH_W)), vec, vec,
            vec, vec, vec, _full((BRANCH_W, BRANCH_W)),
        ],
        out_specs=[
            pl.BlockSpec((None, tt, BRANCH_W), lambda i, j: (i, j, 0)),
            pl.BlockSpec((None, N_HEAD, HEAD_DIM, HEAD_DIM), lambda i, j: (i, 0, 0, 0)),
            pl.BlockSpec((None, 1, SHIFT_W), lambda i, j: (i, 0, 0)),
        ],
        out_shape=[
            jax.ShapeDtypeStruct((b, t, BRANCH_W), BF16),
            jax.ShapeDtypeStruct((b, N_HEAD, HEAD_DIM, HEAD_DIM), F32),
            jax.ShapeDtypeStruct((b, 1, SHIFT_W), F32),
        ],
        scratch_shapes=[
            scr(N_RWKV_IN), pltpu.VMEM((8 + tt, SHIFT_W), F32),
            scr(BRANCH_W), scr(BRANCH_W), scr(BRANCH_W), scr(BRANCH_W), scr(BRANCH_W), scr(BRANCH_W),
            scr(BRANCH_W), pltpu.VMEM((N_PAIR, LANE, LANE), F32),
        ],
        compiler_params=_params("parallel", "arbitrary"),
        name="rwkv_branch",
    )(x, norm_g, w_r, lp["shift_mu"], lp["w0"], lp["w_lora_b"], lp["a0"], lp["a_lora_b"],
      lp["k_k"], lp["k_a"], lp["r_k"], lp["gn_g"], lp["gn_b"], lp["head_bd"])


def _merge_kernel(x_ref, yc_ref, ya_ref, yr_ref, pe_ref, g_ref, wm_ref, wb_ref, wo_ref, pg_ref,
                  pp_ref, fg_ref, o_ref, mg_scr, *, att_transposed, final_norm):
    x = x_ref[...]
    mg_scr[...] = _bdot(_rms(x, g_ref[...]), wm_ref[...])
    ya = ya_ref[...].T if att_transposed else ya_ref[...]
    s = (_sigmoid(mg_scr[:, 0:D_MODEL]) * _bdot(yc_ref[...], wb_ref[0])
         + _sigmoid(mg_scr[:, D_MODEL:2 * D_MODEL]) * _bdot(ya, wb_ref[1])
         + _sigmoid(mg_scr[:, 2 * D_MODEL:3 * D_MODEL]) * _bdot(yr_ref[...], wb_ref[2]))
    x = x + _bdot(s, wo_ref[...])
    x = x + _sigmoid(_bdot(x, pg_ref[...])) * _bdot(pe_ref[...], pp_ref[...])
    o_ref[...] = _rms(x, fg_ref[...]) if final_norm else x


def _merge_call(x, y_conv, y_att, y_rw, pe, lp, final_g, *, att_transposed, final_norm):
    b, t, _ = x.shape
    tm = min(TOK_TILE, t)
    tok = lambda i, j: (i, j, 0)
    att_spec = (pl.BlockSpec((None, BRANCH_W, tm), lambda i, j: (i, 0, j)) if att_transposed
                else pl.BlockSpec((None, tm, BRANCH_W), tok))
    return pl.pallas_call(
        functools.partial(_merge_kernel, att_transposed=att_transposed, final_norm=final_norm),
        grid=(b, t // tm),
        in_specs=[
            pl.BlockSpec((None, tm, D_MODEL), tok),
            pl.BlockSpec((None, tm, BRANCH_W), tok),
            att_spec,
            pl.BlockSpec((None, tm, BRANCH_W), tok),
            pl.BlockSpec((None, tm, P_DIM), tok),
            _full((1, D_MODEL)), _full((D_MODEL, N_MERGE_IN)), _full((3, BRANCH_W, D_MODEL)),
            _full((D_MODEL, D_MODEL)), _full((D_MODEL, D_MODEL)), _full((P_DIM, D_MODEL)),
            _full((1, D_MODEL)),
        ],
        out_specs=pl.BlockSpec((None, tm, D_MODEL), tok),
        out_shape=jax.ShapeDtypeStruct((b, t, D_MODEL), F32),
        scratch_shapes=[pltpu.VMEM((tm, N_MERGE_IN), F32)],
        compiler_params=_params("parallel", "parallel"),
        name="merge",
    )(x, y_conv, y_att, y_rw, pe, lp["norm_g"], lp["w_merge"], lp["w_branch"], lp["w_out"],
      lp["ple_gate"], lp["ple_proj"], final_g)


def _proj_kernel(x_ref, g_ref, w_ref, z_ref):
    z_ref[...] = _bdot(_rms(x_ref[...], g_ref[...]), w_ref[...])


def _proj_call(x, norm_g, w_in):
    m = x.shape[0]
    n_tiles = 3
    tn = N_IN // n_tiles
    return pl.pallas_call(
        _proj_kernel,
        grid=(n_tiles,),
        in_specs=[_full((m, D_MODEL)), _full((1, D_MODEL)),
                  pl.BlockSpec((D_MODEL, tn), lambda j: (0, j))],
        out_specs=pl.BlockSpec((m, tn), lambda j: (0, j)),
        out_shape=jax.ShapeDtypeStruct((m, N_IN), F32),
        compiler_params=_params("parallel"),
        name="decode_project",
    )(x, norm_g, w_in)


def _row_to_col(row_vec):
    n = row_vec.shape[1]
    eye = lax.broadcasted_iota(jnp.int32, (n, n), 0) == lax.broadcasted_iota(jnp.int32, (n, n), 1)
    return jnp.sum(jnp.where(eye, row_vec, 0.0), axis=1, keepdims=True)


def _col_to_row(col_vec):
    n = col_vec.shape[0]
    eye = lax.broadcasted_iota(jnp.int32, (n, n), 0) == lax.broadcasted_iota(jnp.int32, (n, n), 1)
    return jnp.sum(jnp.where(eye, col_vec, 0.0), axis=0, keepdims=True)


def _mix_kernel(z_ref, buf_ref, st_ref, sh0_ref, cw_ref, cb_ref, lg_ref, lb_ref, cos_ref, sa_ref,
                sb_ref, mu_ref, w0_ref, wlb_ref, a0_ref, alb_ref, kk_ref, ka_ref, rk_ref, gg_ref,
                gb_ref, bd_ref,
                yc_ref, cs_ref, q_ref, k_ref, v_ref, sg_ref, yr_ref, so_ref, sho_ref, ext_scr):
    o_att = N_CONV_IN
    o_rw = N_CONV_IN + N_ATT_IN
    u = z_ref[:, 0:BRANCH_W] * _sigmoid(z_ref[:, BRANCH_W:2 * BRANCH_W])
    ext_scr[0:8, :] = jnp.zeros((8, BRANCH_W), F32)
    ext_scr[2:HIST, :] = buf_ref[...]
    ext_scr[HIST:HIST + 8, :] = jnp.broadcast_to(u, (8, BRANCH_W))
    yc_ref[...] = _conv_branch(ext_scr, HIST, 8, cw_ref, cb_ref[...], lg_ref[...], lb_ref[...],
                               z_ref[:, 2 * BRANCH_W:3 * BRANCH_W])[0:1, :]
    cs_ref[...] = ext_scr[3:HIST + 1, :]
    n = BRANCH_W // LANE
    cos, sa, sb = (_tile_lanes(r[...], n) for r in (cos_ref, sa_ref, sb_ref))
    q_ref[...] = _rope(z_ref[:, o_att:o_att + BRANCH_W], cos, sa, sb) * (HEAD_DIM ** -0.5)
    k_ref[...] = _rope(z_ref[:, o_att + BRANCH_W:o_att + 2 * BRANCH_W], cos, sa, sb)
    v_ref[...] = z_ref[:, o_att + 2 * BRANCH_W:o_att + 3 * BRANCH_W]
    sg_ref[...] = _silu(z_ref[:, o_att + 3 * BRANCH_W:o_att + 4 * BRANCH_W])
    cur = z_ref[:, o_rw:o_rw + SHIFT_W]
    sho_ref[...] = cur
    mixed = cur + (sh0_ref[...] - cur) * mu_ref[...]
    r, k, v, log_decay, kk, a_lr = _rwkv_columns(
        mixed, w0_ref[...], wlb_ref, a0_ref[...], alb_ref, kk_ref[...], ka_ref[...], bd_ref)
    decay = jnp.exp(log_decay)
    b_vec = kk * a_lr
    ys = []
    for h in range(N_HEAD):
        lanes = slice(h * HEAD_DIM, (h + 1) * HEAD_DIM)
        s = st_ref[h]
        sa_col = jnp.sum(s * (-kk[:, lanes]), axis=1, keepdims=True)
        s = s * decay[:, lanes] + sa_col * b_vec[:, lanes] + _row_to_col(v[:, lanes]) * k[:, lanes]
        so_ref[h] = s
        ys.append(_col_to_row(jnp.sum(s * r[:, lanes], axis=1, keepdims=True)))
    y = jnp.concatenate(ys, axis=1)
    yr_ref[...] = _rwkv_finish(y, r, k, v, rk_ref[...], gg_ref[...], gb_ref[...],
                               z_ref[:, o_rw + SHIFT_W:o_rw + SHIFT_W + BRANCH_W], bd_ref)


def _mix_call(z, buf, state, shift0, lp, tables):
    b = z.shape[0]
    row = lambda w: pl.BlockSpec((None, 1, w), lambda i: (i, 0, 0))
    vec = _full((1, BRANCH_W))
    tab = _full((1, LANE))
    o = lambda w, dt=F32: jax.ShapeDtypeStruct((b, 1, w), dt)
    return pl.pallas_call(
        _mix_kernel,
        grid=(b,),
        in_specs=[
            row(N_IN),
            pl.BlockSpec((None, CONV_W - 1, BRANCH_W), lambda i: (i, 0, 0)),
            pl.BlockSpec((None, N_HEAD, HEAD_DIM, HEAD_DIM), lambda i: (i, 0, 0, 0)),
            row(SHIFT_W),
            _full((CONV_W, BRANCH_W)), vec, vec, vec, tab, tab, tab,
            _full((1, SHIFT_W)), vec, _full((2 * LORA, BRANCH_W)), vec, _full((2 * LORA, BRANCH_W)),
            vec, vec, vec, vec, vec, _full((BRANCH_W, BRANCH_W)),
        ],
        out_specs=[
            row(BRANCH_W),
            pl.BlockSpec((None, CONV_W - 1, BRANCH_W), lambda i: (i, 0, 0)),
            row(BRANCH_W), row(BRANCH_W), row(BRANCH_W), row(BRANCH_W), row(BRANCH_W),
            pl.BlockSpec((None, N_HEAD, HEAD_DIM, HEAD_DIM), lambda i: (i, 0, 0, 0)),
            row(SHIFT_W),
        ],
        out_shape=[
            o(BRANCH_W, BF16), jax.ShapeDtypeStruct((b, CONV_W - 1, BRANCH_W), F32),
            o(BRANCH_W), o(BRANCH_W), o(BRANCH_W), o(BRANCH_W), o(BRANCH_W, BF16),
            jax.ShapeDtypeStruct((b, N_HEAD, HEAD_DIM, HEAD_DIM), F32), o(SHIFT_W),
        ],
        scratch_shapes=[pltpu.VMEM((HIST + 8, BRANCH_W), F32)],
        compiler_params=_params("parallel"),
        name="decode_mix",
    )(z.reshape(b, 1, N_IN), buf, state, shift0.reshape(b, 1, SHIFT_W),
      lp["conv_w"], lp["conv_b"], lp["conv_ln_g"], lp["conv_ln_b"], *tables,
      lp["shift_mu"], lp["w0"], lp["w_lora_b"], lp["a0"], lp["a_lora_b"], lp["k_k"], lp["k_a"],
      lp["r_k"], lp["gn_g"], lp["gn_b"], lp["head_bd"])


def _head_rows(q_row):
    lane = lax.broadcasted_iota(jnp.int32, (N_HEAD, BRANCH_W), 1)
    row = lax.broadcasted_iota(jnp.int32, (N_HEAD, BRANCH_W), 0)
    return jnp.where((lane >= row * HEAD_DIM) & (lane < (row + 1) * HEAD_DIM), q_row, 0.0)


def _score_kernel(pt_ref, q_ref, *refs):
    del pt_ref
    pages = refs[:PAGES_PER_STEP]
    s_ref, gate_ref = refs[PAGES_PER_STEP:]
    j = pl.program_id(1)
    page_rows = pages[0].shape[0]

    @pl.when(j == 0)
    def _():
        gate_ref[...] = jnp.zeros(gate_ref.shape, F32)

    qh = _head_rows(q_ref[...])
    lane = lax.broadcasted_iota(jnp.int32, gate_ref.shape, 1)
    pages_per_block = MOBA_BLOCK // page_rows
    gate = gate_ref[...]
    for r in range(PAGES_PER_STEP):
        s = _bdot_nt(qh, pages[r][...])
        s_ref[:, r * page_rows:(r + 1) * page_rows] = s
        blk = (j * PAGES_PER_STEP + r) // pages_per_block
        gate = gate + jnp.where(lane == blk, jnp.sum(s, axis=1, keepdims=True), 0.0)
    gate_ref[...] = gate


def _score_call(page_table, q, cache, layer, n_pool):
    b, n_pages = page_table.shape
    page_rows = cache.shape[1]
    past = n_pages * page_rows
    assert n_pages % PAGES_PER_STEP == 0 and MOBA_BLOCK % page_rows == 0
    assert past // MOBA_BLOCK <= LANE

    def page_spec(r):
        return pl.BlockSpec(
            (None, page_rows, BRANCH_W),
            lambda i, j, pt: (layer * n_pool + pt[i, j * PAGES_PER_STEP + r], 0, 0))

    grid_spec = pltpu.PrefetchScalarGridSpec(
        num_scalar_prefetch=1,
        grid=(b, n_pages // PAGES_PER_STEP),
        in_specs=[pl.BlockSpec((None, 1, BRANCH_W), lambda i, j, pt: (i, 0, 0))]
        + [page_spec(r) for r in range(PAGES_PER_STEP)],
        out_specs=[
            pl.BlockSpec((None, N_HEAD, PAGES_PER_STEP * page_rows), lambda i, j, pt: (i, 0, j)),
            pl.BlockSpec((None, N_HEAD, LANE), lambda i, j, pt: (i, 0, 0)),
        ],
    )
    return pl.pallas_call(
        _score_kernel,
        grid_spec=grid_spec,
        out_shape=[jax.ShapeDtypeStruct((b, N_HEAD, past), F32),
                   jax.ShapeDtypeStruct((b, N_HEAD, LANE), F32)],
        compiler_params=_params("parallel", "arbitrary"),
        name="decode_scores",
    )(page_table, q, *([cache] * PAGES_PER_STEP))


def _select_kernel(s_ref, gate_ref, q_ref, k_ref, p_ref, idx_ref, pown_ref, *, n_blocks):
    lane = lax.broadcasted_iota(jnp.int32, gate_ref.shape, 1).astype(F32)
    gate = jnp.where(lane < n_blocks, gate_ref[...], -jnp.inf)
    key_blk = jnp.right_shift(lax.broadcasted_iota(jnp.int32, s_ref.shape, 1),
                              MOBA_BLOCK.bit_length() - 1).astype(F32)
    sel = jnp.zeros(s_ref.shape, F32)
    idx_out = jnp.zeros(gate_ref.shape, F32)
    for j in range(MOBA_TOPK):
        m = jnp.max(gate, axis=1, keepdims=True)
        idx = jnp.min(jnp.where(gate == m, lane, float(LANE)), axis=1, keepdims=True)
        sel = jnp.where(key_blk == idx, 1.0, sel)
        idx_out = jnp.where(lane == j, idx, idx_out)
        gate = jnp.where(lane == idx, -jnp.inf, gate)
    s_own = jnp.sum(_head_rows(q_ref[...]) * k_ref[...], axis=1, keepdims=True)
    s = jnp.where(sel > 0.0, s_ref[...], NEG)
    m = jnp.maximum(jnp.max(s, axis=1, keepdims=True), s_own)
    e = jnp.exp(s - m)
    e_own = jnp.exp(s_own - m)
    l = jnp.sum(e, axis=1, keepdims=True) + e_own
    p_ref[...] = e / l
    pown_ref[...] = jnp.broadcast_to(e_own / l, pown_ref.shape)
    idx_ref[...] = idx_out.astype(jnp.int32)


def _select_call(scores, gate, q, k_new):
    b, _, past = scores.shape
    n_blocks = past // MOBA_BLOCK
    assert n_blocks >= MOBA_TOPK
    head = lambda w: pl.BlockSpec((None, N_HEAD, w), lambda i: (i, 0, 0))
    row = pl.BlockSpec((None, 1, BRANCH_W), lambda i: (i, 0, 0))
    return pl.pallas_call(
        functools.partial(_select_kernel, n_blocks=n_blocks),
        grid=(b,),
        in_specs=[head(past), head(LANE), row, row],
        out_specs=[head(past), head(LANE), head(LANE)],
        out_shape=[jax.ShapeDtypeStruct((b, N_HEAD, past), F32),
                   jax.ShapeDtypeStruct((b, N_HEAD, LANE), jnp.int32),
                   jax.ShapeDtypeStruct((b, N_HEAD, LANE), F32)],
        compiler_params=_params("parallel"),
        name="decode_select",
    )(scores, gate, q, k_new)


def _gather_kernel(pt_ref, ix_ref, pown_ref, vnew_ref, sg_ref, *refs, n_sel):
    del pt_ref, ix_ref
    p_rows = refs[:n_sel]
    v_pages = refs[n_sel:2 * n_sel]
    y_ref = refs[2 * n_sel]
    h = pl.program_id(1)

    @pl.when(h == 0)
    def _():
        y_ref[...] = jnp.zeros(y_ref.shape, F32)

    acc = pown_ref[pl.ds(h, 1), 0:1] * vnew_ref[...]
    for r in range(n_sel):
        p8 = jnp.broadcast_to(p_rows[r][...], (8, p_rows[r].shape[1]))
        acc = acc + _bdot(p8, v_pages[r][...])[0:1, :]
    lane = lax.broadcasted_iota(jnp.int32, acc.shape, 1)
    mine = (lane >= h * HEAD_DIM) & (lane < (h + 1) * HEAD_DIM)
    y_ref[...] = y_ref[...] + jnp.where(mine, acc * sg_ref[...], 0.0)


def _gather_call(page_table, idx, probs, p_own, v_new, sg, cache, layer, n_pool):
    b, n_pages = page_table.shape
    page_rows = cache.shape[1]
    ppb = MOBA_BLOCK // page_rows
    n_sel = MOBA_TOPK * ppb
    probs = probs.reshape(b, N_HEAD, n_pages, 1, page_rows)

    def seq_page(i, h, ix, r):
        return ix[i, h * MOBA_TOPK + r // ppb] * ppb + r % ppb

    def p_spec(r):
        return pl.BlockSpec((None, None, None, 1, page_rows),
                            lambda i, h, pt, ix: (i, h, seq_page(i, h, ix, r), 0, 0))

    def v_spec(r):
        return pl.BlockSpec((None, page_rows, BRANCH_W),
                            lambda i, h, pt, ix: (layer * n_pool + pt[i, seq_page(i, h, ix, r)], 0, 0))

    row = pl.BlockSpec((None, 1, BRANCH_W), lambda i, h, pt, ix: (i, 0, 0))
    grid_spec = pltpu.PrefetchScalarGridSpec(
        num_scalar_prefetch=2,
        grid=(b, N_HEAD),
        in_specs=[pl.BlockSpec((None, N_HEAD, LANE), lambda i, h, pt, ix: (i, 0, 0)), row, row]
        + [p_spec(r) for r in range(n_sel)] + [v_spec(r) for r in range(n_sel)],
        out_specs=row,
    )
    return pl.pallas_call(
        functools.partial(_gather_kernel, n_sel=n_sel),
        grid_spec=grid_spec,
        out_shape=jax.ShapeDtypeStruct((b, 1, BRANCH_W), F32),
        compiler_params=_params("parallel", "arbitrary"),
        name="decode_gather",
    )(page_table, idx, p_own, v_new, sg, *([probs] * n_sel), *([cache] * n_sel))


def _layer_params(i, norm_g, w_in, conv_w, conv_b, conv_ln_g, conv_ln_b, shift_mu, w0, w_lora_b, a0,
                  a_lora_b, k_k, k_a, r_k, gn_g, gn_b, w_branch, w_out, ple_proj, ple_gate):
    w = w_in[i].astype(BF16)
    o1, o2, o3 = N_CONV_IN, N_CONV_IN + N_ATT_IN, N_CONV_IN + N_ATT_IN + N_RWKV_IN
    row = lambda a: a[i].reshape(1, -1)
    zeros = jnp.zeros((LORA, BRANCH_W), F32)
    return dict(
        norm_g=row(norm_g), w_in=w, w_conv=w[:, :o1], w_att=w[:, o1:o2], w_rwkv=w[:, o2:o3],
        w_merge=w[:, o3:],
        conv_w=conv_w[i], conv_b=row(conv_b), conv_ln_g=row(conv_ln_g), conv_ln_b=row(conv_ln_b),
        shift_mu=row(shift_mu), w0=row(w0), a0=row(a0),
        w_lora_b=jnp.concatenate([w_lora_b[i], zeros], axis=0).astype(BF16),
        a_lora_b=jnp.concatenate([zeros, a_lora_b[i]], axis=0).astype(BF16),
        k_k=row(k_k), k_a=row(k_a), r_k=row(r_k), gn_g=row(gn_g), gn_b=row(gn_b),
        w_branch=w_branch[i].astype(BF16), w_out=w_out[i].astype(BF16),
        ple_proj=ple_proj[i].astype(BF16), ple_gate=ple_gate[i].astype(BF16),
        head_bd=_head_block_diag(),
    )


def _prompt_layer(x, pe, lp, tables, final_g, final_norm):
    b, t, _ = x.shape
    nb = t // MOBA_BLOCK
    y_conv, conv_new = _conv_call(x, lp["norm_g"], lp["w_conv"], lp["conv_w"], lp["conv_b"],
                                  lp["conv_ln_g"], lp["conv_ln_b"])
    k, v, qt, kb, vt, sgt, km = _att1_call(x, lp["norm_g"], lp["w_att"], tables)
    km = km.reshape(b, nb, N_PAIR, LANE).transpose(0, 2, 1, 3)
    y_att_t = _att2_call(qt, kb, vt, km, sgt)
    y_rw, wkv, shift = _rwkv_call(x, lp["norm_g"], lp["w_rwkv"], lp)
    x = _merge_call(x, y_conv, y_att_t, y_rw, pe, lp, final_g,
                    att_transposed=True, final_norm=final_norm)
    return x, k, v, conv_new, wkv, shift.reshape(b, SHIFT_W)


def _sample_layer(x, pe, lp, tables, final_g, final_norm, layer, n_pool, cache_k, cache_v,
                  page_table, buf, state, shift0):
    b = x.shape[0]
    z = _proj_call(x.reshape(b, D_MODEL), lp["norm_g"], lp["w_in"])
    y_conv, conv_new, q, k, v, sg, y_rw, wkv, shift = _mix_call(z, buf, state, shift0, lp, tables)
    scores, gate = _score_call(page_table, q, cache_k, layer, n_pool)
    probs, idx, p_own = _select_call(scores, gate, q, k)
    idx = idx[:, :, :MOBA_TOPK].reshape(b, N_HEAD * MOBA_TOPK)
    y_att = _gather_call(page_table, idx, probs, p_own, v, sg, cache_v, layer, n_pool)
    tok = lambda a: a.reshape(1, b, -1)
    x = _merge_call(tok(x), tok(y_conv), tok(y_att), tok(y_rw), tok(pe), lp, final_g,
                    att_transposed=False, final_norm=final_norm)
    return (x.reshape(b, 1, D_MODEL), k, v, conv_new, wkv, shift.reshape(b, SHIFT_W))


def kernel(x_prompt, x_sample, cache_k, cache_v, page_table, state_conv, state_wkv, state_shift,
           p_prompt, p_sample, norm_g, w_in, conv_w, conv_b, conv_ln_g, conv_ln_b, shift_mu, w0,
           w_lora_b, a0, a_lora_b, k_k, k_a, r_k, gn_g, gn_b, w_branch, w_out, ple_proj, ple_gate,
           final_norm_g):
    depth = w_in.shape[0]
    b_p, t_p, _ = x_prompt.shape
    b_s, t_s, _ = x_sample.shape
    assert t_s == 1 and t_p % TOK_TILE == 0 and TOK_TILE == MOBA_BLOCK
    n_pool, page_rows = cache_k.shape[1], cache_k.shape[2]
    past_len = page_table.shape[1] * page_rows
    assert past_len % MOBA_BLOCK == 0
    tables_p = _rope_tables(jnp.arange(t_p, dtype=jnp.int32))
    tables_s = _rope_tables(past_len + jnp.arange(1, dtype=jnp.int32))
    cache_k = cache_k.reshape(depth * n_pool, page_rows, BRANCH_W)
    cache_v = cache_v.reshape(depth * n_pool, page_rows, BRANCH_W)
    final_g = final_norm_g.reshape(1, D_MODEL)
    xp, xs = x_prompt, x_sample
    outs = [[] for _ in range(10)]
    for i in range(depth):
        lp = _layer_params(i, norm_g, w_in, conv_w, conv_b, conv_ln_g, conv_ln_b, shift_mu, w0,
                           w_lora_b, a0, a_lora_b, k_k, k_a, r_k, gn_g, gn_b, w_branch, w_out,
                           ple_proj, ple_gate)
        last = i == depth - 1
        xp, kp, vp, cp, wp, sp = _prompt_layer(xp, p_prompt[i], lp, tables_p, final_g, last)
        xs, ks, vs, cs, ws, ss = _sample_layer(
            xs, p_sample[i], lp, tables_s, final_g, last, i, n_pool, cache_k, cache_v, page_table,
            state_conv[i], state_wkv[i], state_shift[i])
        heads_p = lambda a: a.reshape(b_p, t_p, N_HEAD, HEAD_DIM)
        heads_s = lambda a: a.reshape(b_s, 1, N_HEAD, HEAD_DIM)
        for lst, a in zip(outs, (heads_p(kp), heads_p(vp), heads_s(ks), heads_s(vs),
                                 cp, cs, wp, ws, sp, ss)):
            lst.append(a)
    return (xp, xs) + tuple(jnp.stack(lst) for lst in outs)
```

```python
import functools

import jax
import jax.numpy as jnp
from jax import lax
from jax.experimental import pallas as pl
from jax.experimental.pallas import tpu as pltpu

D_MODEL = 1024
P_DIM = 256
HEAD_DIM = 64
BRANCH_W = 512
N_HEAD = 8
N_PAIR = 4
CONV_W = 31
ROT_DIM = 16
ROPE_THETA = 500000.0
MOBA_BLOCK = 256
MOBA_TOPK = 3
LORA = 64
SHIFT_W = 3 * BRANCH_W + 2 * LORA
N_CONV_IN = 3 * BRANCH_W
N_ATT_IN = 4 * BRANCH_W
N_RWKV_IN = SHIFT_W + BRANCH_W
N_MERGE_IN = 3 * D_MODEL
N_IN = N_CONV_IN + N_ATT_IN + N_RWKV_IN + N_MERGE_IN
NORM_EPS = 1e-6
LN_EPS = 1e-5
GN_EPS = 64e-5
NEG = -1e30
LOG2_E = 1.4426950408889634

LANE = 128
TOK_TILE = 256
CHUNK = 64
HIST = 32
CONV_ROWS = 32
PAGES_PER_STEP = 16
VMEM_LIMIT = 56 * 1024 * 1024

F32 = jnp.float32
BF16 = jnp.bfloat16


def _bdot(a, b):
    return jnp.dot(a.astype(BF16), b.astype(BF16), preferred_element_type=F32)


def _bdot_nt(a, b):
    return lax.dot_general(a.astype(BF16), b.astype(BF16), (((1,), (1,)), ((), ())),
                           preferred_element_type=F32)


def _split_dot(x, m_bf16):
    hi = x.astype(BF16)
    lo = (x - hi.astype(F32)).astype(BF16)
    return (jnp.dot(hi, m_bf16, preferred_element_type=F32)
            + jnp.dot(lo, m_bf16, preferred_element_type=F32))


def _split3_dot_left(m_bf16, x):
    hi = x.astype(BF16)
    r1 = x - hi.astype(F32)
    mid = r1.astype(BF16)
    lo = (r1 - mid.astype(F32)).astype(BF16)
    return (jnp.dot(m_bf16, hi, preferred_element_type=F32)
            + jnp.dot(m_bf16, mid, preferred_element_type=F32)
            + jnp.dot(m_bf16, lo, preferred_element_type=F32))


def _sigmoid(x):
    return 1.0 / (1.0 + jnp.exp(-x))


def _silu(x):
    return x * _sigmoid(x)


def _softplus(x):
    return jnp.maximum(x, 0.0) + jnp.log(1.0 + jnp.exp(-jnp.abs(x)))


def _rms(x, g):
    return x * lax.rsqrt(jnp.mean(x * x, axis=-1, keepdims=True) + NORM_EPS) * g


def _params(*sem):
    return pltpu.CompilerParams(dimension_semantics=sem, vmem_limit_bytes=VMEM_LIMIT)


def _full(shape):
    nd = len(shape)
    return pl.BlockSpec(shape, lambda *_: (0,) * nd)


def _conv_branch(ext_ref, base, rows, cw_ref, cb, lg, lb, g_conv):
    acc = jnp.zeros((rows, BRANCH_W), F32) + cb
    first = base - (CONV_W - 1)
    for s in range(8):
        z = None
        for j in range(CONV_W):
            if (first + j) % 8 == s:
                start = first + j - s
                term = ext_ref[start:start + rows + 8, :] * cw_ref[j:j + 1, :]
                z = term if z is None else z + term
        acc = acc + z[s:s + rows, :]
    mu = jnp.mean(acc, axis=-1, keepdims=True)
    d = acc - mu
    var = jnp.mean(d * d, axis=-1, keepdims=True)
    y = d * lax.rsqrt(var + LN_EPS) * lg + lb
    return (_silu(y) * _silu(g_conv)).astype(BF16)


def _conv_kernel(x_ref, g_ref, w_ref, cw_ref, cb_ref, lg_ref, lb_ref, y_ref, cs_ref, z_scr, ext_scr):
    tt = x_ref.shape[0]

    @pl.when(pl.program_id(1) == 0)
    def _():
        ext_scr[0:HIST, :] = jnp.zeros((HIST, BRANCH_W), F32)
        ext_scr[HIST + tt:HIST + tt + 8, :] = jnp.zeros((8, BRANCH_W), F32)

    z_scr[...] = _bdot(_rms(x_ref[...], g_ref[...]), w_ref[...])
    ext_scr[HIST:HIST + tt, :] = z_scr[:, 0:BRANCH_W] * _sigmoid(z_scr[:, BRANCH_W:2 * BRANCH_W])
    cb, lg, lb = cb_ref[...], lg_ref[...], lb_ref[...]
    for r in range(0, tt, CONV_ROWS):
        y_ref[r:r + CONV_ROWS, :] = _conv_branch(
            ext_scr, HIST + r, CONV_ROWS, cw_ref, cb, lg, lb,
            z_scr[r:r + CONV_ROWS, 2 * BRANCH_W:3 * BRANCH_W])
    cs_ref[...] = ext_scr[HIST + tt - (CONV_W - 1):HIST + tt, :]
    ext_scr[0:HIST, :] = ext_scr[tt:tt + HIST, :]


def _conv_call(x, norm_g, w_c, conv_w, conv_b, ln_g, ln_b):
    b, t, _ = x.shape
    tt = TOK_TILE
    return pl.pallas_call(
        _conv_kernel,
        grid=(b, t // tt),
        in_specs=[
            pl.BlockSpec((None, tt, D_MODEL), lambda i, j: (i, j, 0)),
            _full((1, D_MODEL)), _full((D_MODEL, N_CONV_IN)), _full((CONV_W, BRANCH_W)),
            _full((1, BRANCH_W)), _full((1, BRANCH_W)), _full((1, BRANCH_W)),
        ],
        out_specs=[
            pl.BlockSpec((None, tt, BRANCH_W), lambda i, j: (i, j, 0)),
            pl.BlockSpec((None, CONV_W - 1, BRANCH_W), lambda i, j: (i, 0, 0)),
        ],
        out_shape=[
            jax.ShapeDtypeStruct((b, t, BRANCH_W), BF16),
            jax.ShapeDtypeStruct((b, CONV_W - 1, BRANCH_W), F32),
        ],
        scratch_shapes=[pltpu.VMEM((tt, N_CONV_IN), F32),
                        pltpu.VMEM((HIST + tt + 8, BRANCH_W), F32)],
        compiler_params=_params("parallel", "arbitrary"),
        name="conv_branch",
    )(x, norm_g, w_c, conv_w, conv_b, ln_g, ln_b)


def _rope_tables(pos):
    half = ROT_DIM // 2
    inv = jnp.power(ROPE_THETA, -jnp.arange(half, dtype=F32) * (2.0 / ROT_DIM))
    ang = pos.astype(F32)[:, None] * inv[None, :]
    cos, sin = jnp.cos(ang), jnp.sin(ang)
    n = pos.shape[0]
    pad = jnp.zeros((n, HEAD_DIM - ROT_DIM), F32)
    cos64 = jnp.concatenate([cos, cos, pad + 1.0], axis=1)
    sa64 = jnp.concatenate([-sin, jnp.zeros_like(sin), pad], axis=1)
    sb64 = jnp.concatenate([jnp.zeros_like(sin), sin, pad], axis=1)
    return tuple(jnp.concatenate([a, a], axis=1) for a in (cos64, sa64, sb64))


def _rope(a, cos, sa, sb):
    w = a.shape[1]
    return a * cos + pltpu.roll(a, w - ROT_DIM // 2, 1) * sa + pltpu.roll(a, ROT_DIM // 2, 1) * sb


def _tile_lanes(a, n):
    return jnp.concatenate([a] * n, axis=1)


def _att1_kernel(x_ref, g_ref, w_ref, cos_ref, sa_ref, sb_ref,
                 k_ref, v_ref, qt_ref, kb_ref, vt_ref, sgt_ref, km_ref, z_scr):
    z_scr[...] = _bdot(_rms(x_ref[...], g_ref[...]), w_ref[...])
    n = BRANCH_W // LANE
    cos, sa, sb = (_tile_lanes(r[...], n) for r in (cos_ref, sa_ref, sb_ref))
    q = _rope(z_scr[:, 0:BRANCH_W], cos, sa, sb) * (LOG2_E * HEAD_DIM ** -0.5)
    k = _rope(z_scr[:, BRANCH_W:2 * BRANCH_W], cos, sa, sb)
    v = z_scr[:, 2 * BRANCH_W:3 * BRANCH_W]
    k_ref[...] = k
    v_ref[...] = v
    qt_ref[...] = q.T.astype(BF16)
    for p in range(N_PAIR):
        kb_ref[p] = k[:, p * LANE:(p + 1) * LANE].astype(BF16)
    vt_ref[...] = v.T.astype(BF16)
    sgt_ref[...] = _silu(z_scr[:, 3 * BRANCH_W:4 * BRANCH_W]).T
    km_ref[...] = jnp.mean(k, axis=0, keepdims=True)


def _att1_call(x, norm_g, w_a, tables):
    b, t, _ = x.shape
    tt = TOK_TILE
    nb = t // tt
    tok = lambda i, j: (i, j, 0)
    tr = lambda i, j: (i, 0, j)
    return pl.pallas_call(
        _att1_kernel,
        grid=(b, nb),
        in_specs=[
            pl.BlockSpec((None, tt, D_MODEL), tok),
            _full((1, D_MODEL)), _full((D_MODEL, N_ATT_IN)),
            pl.BlockSpec((tt, LANE), lambda i, j: (j, 0)),
            pl.BlockSpec((tt, LANE), lambda i, j: (j, 0)),
            pl.BlockSpec((tt, LANE), lambda i, j: (j, 0)),
        ],
        out_specs=[
            pl.BlockSpec((None, tt, BRANCH_W), tok),
            pl.BlockSpec((None, tt, BRANCH_W), tok),
            pl.BlockSpec((None, BRANCH_W, tt), tr),
            pl.BlockSpec((None, N_PAIR, tt, LANE), lambda i, j: (i, 0, j, 0)),
            pl.BlockSpec((None, None, BRANCH_W, tt), lambda i, j: (i, j, 0, 0)),
            pl.BlockSpec((None, BRANCH_W, tt), tr),
            pl.BlockSpec((None, None, 1, BRANCH_W), lambda i, j: (i, j, 0, 0)),
        ],
        out_shape=[
            jax.ShapeDtypeStruct((b, t, BRANCH_W), F32),
            jax.ShapeDtypeStruct((b, t, BRANCH_W), F32),
            jax.ShapeDtypeStruct((b, BRANCH_W, t), BF16),
            jax.ShapeDtypeStruct((b, N_PAIR, t, LANE), BF16),
            jax.ShapeDtypeStruct((b, nb, BRANCH_W, tt), BF16),
            jax.ShapeDtypeStruct((b, BRANCH_W, t), F32),
            jax.ShapeDtypeStruct((b, nb, 1, BRANCH_W), F32),
        ],
        scratch_shapes=[pltpu.VMEM((tt, N_ATT_IN), F32)],
        compiler_params=_params("parallel", "parallel"),
        name="att_project",
    )(x, norm_g, w_a, *tables)


def _top_blocks(gate, n_slots):
    nb = gate.shape[0]
    row = lax.broadcasted_iota(jnp.int32, gate.shape, 0).astype(F32)
    sel = jnp.zeros(gate.shape, F32)
    for j in range(MOBA_TOPK):
        m = jnp.max(gate, axis=0, keepdims=True)
        idx = jnp.min(jnp.where(gate == m, row, float(nb)), axis=0, keepdims=True)
        hit = row == jnp.where(j < n_slots, idx, -1.0)
        sel = jnp.where(hit, 1.0, sel)
        gate = jnp.where(row == idx, -jnp.inf, gate)
    return sel


def _att2_kernel(qt_ref, kb_ref, vt_ref, km_ref, sgt_ref, yt_ref,
                 qp_scr, bias_scr, s_scr, e_scr, alpha_scr, m_scr, l_scr, acc_scr):
    i = pl.program_id(1)
    tq = qt_ref.shape[1]
    nb = km_ref.shape[1]
    row = lax.broadcasted_iota(jnp.int32, (LANE, tq), 0)
    blk = lax.broadcasted_iota(jnp.int32, (nb, tq), 0)
    for h in range(N_HEAD):
        p = h // 2
        qpair = qt_ref[p * LANE:(p + 1) * LANE, :].astype(F32)
        qpad = jnp.where((row >= HEAD_DIM) == (h % 2 == 1), qpair, 0.0).astype(BF16)
        qp_scr[h] = qpad
        gate = jnp.dot(km_ref[p].astype(BF16), qpad, preferred_element_type=F32)
        sel = _top_blocks(jnp.where(blk < i, gate, -jnp.inf), i)
        bias_scr[h] = jnp.where(sel > 0.0, 0.0, NEG)
    m_scr[...] = jnp.full(m_scr.shape, NEG, F32)
    l_scr[...] = jnp.zeros(l_scr.shape, F32)
    acc_scr[...] = jnp.zeros(acc_scr.shape, F32)

    def key_block(n, bias_of):
        start = pl.multiple_of(n * MOBA_BLOCK, MOBA_BLOCK)
        for h in range(N_HEAD):
            s_scr[h] = jnp.dot(kb_ref[h // 2, pl.ds(start, MOBA_BLOCK), :], qp_scr[h],
                               preferred_element_type=F32)
        for h in range(N_HEAD):
            s = s_scr[h] + bias_of(h)
            m = m_scr[h:h + 1, :]
            m_new = jnp.maximum(m, jnp.max(s, axis=0, keepdims=True))
            alpha = jnp.exp2(m - m_new)
            e = jnp.exp2(s - m_new)
            m_scr[h:h + 1, :] = m_new
            alpha_scr[h:h + 1, :] = alpha
            l_scr[h:h + 1, :] = alpha * l_scr[h:h + 1, :] + jnp.sum(e, axis=0, keepdims=True)
            e_scr[h] = e.astype(BF16)
        for h in range(N_HEAD):
            pv = jnp.dot(vt_ref[n, h * HEAD_DIM:(h + 1) * HEAD_DIM, :], e_scr[h],
                         preferred_element_type=F32)
            acc_scr[h] = alpha_scr[h:h + 1, :] * acc_scr[h] + pv

    causal = jnp.where(lax.broadcasted_iota(jnp.int32, (MOBA_BLOCK, tq), 0)
                       <= lax.broadcasted_iota(jnp.int32, (MOBA_BLOCK, tq), 1), 0.0, NEG)
    key_block(i, lambda h: causal)

    def body(n, carry):
        key_block(n, lambda h: bias_scr[h, pl.ds(n, 1), :])
        return carry

    lax.fori_loop(0, i, body, 0)
    for h in range(N_HEAD):
        rows = slice(h * HEAD_DIM, (h + 1) * HEAD_DIM)
        yt_ref[rows, :] = acc_scr[h] / l_scr[h:h + 1, :] * sgt_ref[rows, :]


def _att2_call(qt, kb, vt, km, sgt):
    b, _, t = qt.shape
    nb = vt.shape[1]
    tq = MOBA_BLOCK
    return pl.pallas_call(
        _att2_kernel,
        grid=(b, t // tq),
        in_specs=[
            pl.BlockSpec((None, BRANCH_W, tq), lambda bi, i: (bi, 0, i)),
            pl.BlockSpec((None, N_PAIR, t, LANE), lambda bi, i: (bi, 0, 0, 0)),
            pl.BlockSpec((None, nb, BRANCH_W, MOBA_BLOCK), lambda bi, i: (bi, 0, 0, 0)),
            pl.BlockSpec((None, N_PAIR, nb, LANE), lambda bi, i: (bi, 0, 0, 0)),
            pl.BlockSpec((None, BRANCH_W, tq), lambda bi, i: (bi, 0, i)),
        ],
        out_specs=pl.BlockSpec((None, BRANCH_W, tq), lambda bi, i: (bi, 0, i)),
        out_shape=jax.ShapeDtypeStruct((b, BRANCH_W, t), F32),
        scratch_shapes=[
            pltpu.VMEM((N_HEAD, LANE, tq), BF16), pltpu.VMEM((N_HEAD, nb, tq), F32),
            pltpu.VMEM((N_HEAD, MOBA_BLOCK, tq), F32), pltpu.VMEM((N_HEAD, MOBA_BLOCK, tq), BF16),
            pltpu.VMEM((N_HEAD, tq), F32), pltpu.VMEM((N_HEAD, tq), F32), pltpu.VMEM((N_HEAD, tq), F32),
            pltpu.VMEM((N_HEAD, HEAD_DIM, tq), F32),
        ],
        compiler_params=_params("parallel", "arbitrary"),
        name="moba_attention",
    )(qt, kb, vt, km, sgt)


def _rwkv_columns(mixed, w0, wlb_ref, a0, alb_ref, kk_scale, k_a, bd_ref):
    r = mixed[:, 0:BRANCH_W]
    k = mixed[:, BRANCH_W:2 * BRANCH_W]
    v = mixed[:, 2 * BRANCH_W:3 * BRANCH_W]
    lora = mixed[:, 3 * BRANCH_W:3 * BRANCH_W + 2 * LORA]
    lane = lax.broadcasted_iota(jnp.int32, lora.shape, 1)
    lora = jnp.where(lane < LORA, jnp.tanh(lora), lora)
    w_log = -_softplus(-(w0 + _bdot(lora, wlb_ref[...]))) - 0.5
    log_decay = -jnp.exp(w_log)
    a_lr = _sigmoid(a0 + _bdot(lora, alb_ref[...]))
    kk = k * kk_scale
    kk = kk * lax.rsqrt(jnp.maximum(_bdot(kk * kk, bd_ref[...]), 1e-24))
    k = k * (1.0 + (a_lr - 1.0) * k_a)
    return r, k, v, log_decay, kk, a_lr


def _rwkv_finish(y, r, k, v, r_k, gn_g, gn_b, g_rwkv, bd_ref):
    m = _split_dot(y, bd_ref[...]) * (1.0 / HEAD_DIM)
    d = y - m
    var = _bdot(d * d, bd_ref[...]) * (1.0 / HEAD_DIM)
    yn = d * lax.rsqrt(var + GN_EPS) * gn_g + gn_b
    bonus = _bdot(r * k * r_k, bd_ref[...]) * v
    return ((yn + bonus) * _silu(g_rwkv)).astype(BF16)


def _stack_heads(a, mask_a):
    return jnp.concatenate([jnp.where(mask_a, a, 0.0), jnp.where(mask_a, 0.0, a)], axis=0)


def _rwkv_kernel(x_ref, g_ref, w_ref, mu_ref, w0_ref, wlb_ref, a0_ref, alb_ref, kk_ref, ka_ref,
                 rk_ref, gg_ref, gb_ref, bd_ref,
                 y_ref, st_ref, sh_ref,
                 z_scr, shs_scr, r_scr, k_scr, v_scr, ld_scr, a_scr, b_scr, y_scr, s_scr,
                 cum_scr, rs_scr, as_scr, bs_scr, ks_scr, vs_scr, vst_scr, bend_scr, kend_scr, pw_scr,
                 aak_scr, arb_scr, ark_scr, t1_scr, wt_scr, x_scr, yv_scr, kv_scr, ut_scr, pt_scr):
    tt = x_ref.shape[0]

    @pl.when(pl.program_id(1) == 0)
    def _():
        shs_scr[0:8, :] = jnp.zeros((8, SHIFT_W), F32)
        s_scr[...] = jnp.zeros(s_scr.shape, F32)

    z_scr[...] = _bdot(_rms(x_ref[...], g_ref[...]), w_ref[...])
    shs_scr[8:8 + tt, :] = z_scr[:, 0:SHIFT_W]
    cur = z_scr[:, 0:SHIFT_W]
    mixed = cur + (shs_scr[7:7 + tt, :] - cur) * mu_ref[...]
    sh_ref[...] = shs_scr[7 + tt:8 + tt, :]
    shs_scr[7:8, :] = shs_scr[7 + tt:8 + tt, :]

    r, k, v, log_decay, kk, a_lr = _rwkv_columns(
        mixed, w0_ref[...], wlb_ref, a0_ref[...], alb_ref, kk_ref[...], ka_ref[...], bd_ref)
    r_scr[...] = r
    k_scr[...] = k
    v_scr[...] = v
    ld_scr[...] = log_decay
    a_scr[...] = -kk
    b_scr[...] = kk * a_lr

    n2 = 2 * CHUNK
    row = lax.broadcasted_iota(jnp.int32, (n2, n2), 0)
    col = lax.broadcasted_iota(jnp.int32, (n2, n2), 1)
    same = (row >= CHUNK) == (col >= CHUNK)
    rt = jnp.where(row >= CHUNK, row - CHUNK, row)
    ct = jnp.where(col >= CHUNK, col - CHUNK, col)
    strict = same & (rt > ct)
    incl = same & (rt >= ct)
    crow = lax.broadcasted_iota(jnp.int32, (CHUNK, CHUNK), 0)
    ccol = lax.broadcasted_iota(jnp.int32, (CHUNK, CHUNK), 1)
    tri = jnp.where(crow >= ccol, 1.0, 0.0).astype(BF16)
    mask_a = lax.broadcasted_iota(jnp.int32, (CHUNK, LANE), 1) < HEAD_DIM

    n_chunk = tt // CHUNK
    units = [(c, p) for c in range(n_chunk) for p in range(N_PAIR)]
    n_unit = range(len(units))
    dot = functools.partial(jnp.dot, preferred_element_type=F32)

    def nt(a, b):
        return lax.dot_general(a, b, (((1,), (1,)), ((), ())), preferred_element_type=F32)

    def blk(ref, c, p):
        return ref[c * CHUNK:(c + 1) * CHUNK, p * LANE:(p + 1) * LANE]

    for c in range(n_chunk):
        rows = slice(c * CHUNK, (c + 1) * CHUNK)
        cum_scr[rows, :] = _split3_dot_left(tri, ld_scr[rows, :])
    for u, (c, p) in enumerate(units):
        cum = blk(cum_scr, c, p)
        last = cum[CHUNK - 1:CHUNK, :]
        p_inv = jnp.exp(-cum)
        p_end = jnp.exp(last - cum)
        bb = blk(b_scr, c, p)
        kc = blk(k_scr, c, p)
        v_s = _stack_heads(blk(v_scr, c, p), mask_a)
        rs_scr[u] = _stack_heads(blk(r_scr, c, p) * jnp.exp(cum), mask_a).astype(BF16)
        as_scr[u] = _stack_heads(blk(a_scr, c, p) * jnp.exp(cum - blk(ld_scr, c, p)),
                                 mask_a).astype(BF16)
        bs_scr[u] = _stack_heads(bb * p_inv, mask_a).astype(BF16)
        ks_scr[u] = _stack_heads(kc * p_inv, mask_a).astype(BF16)
        vs_scr[u] = v_s.astype(BF16)
        vst_scr[u] = v_s.T.astype(BF16)
        bend_scr[u] = _stack_heads(bb * p_end, mask_a).astype(BF16)
        kend_scr[u] = _stack_heads(kc * p_end, mask_a).astype(BF16)
        pt_scr[u] = jnp.broadcast_to(jnp.exp(last), (8, LANE))
    eye = jnp.where(row == col, 1.0, 0.0)
    for u in n_unit:
        a_ab = jnp.where(strict, nt(as_scr[u], bs_scr[u]), 0.0)
        x_scr[u] = eye + a_ab
        pw_scr[u] = a_ab.astype(BF16)
    for u in n_unit:
        aak_scr[u] = jnp.where(strict, nt(as_scr[u], ks_scr[u]), 0.0).astype(BF16)
    for u in n_unit:
        arb_scr[u] = jnp.where(incl, nt(rs_scr[u], bs_scr[u]), 0.0).astype(BF16)
    for u in n_unit:
        ark_scr[u] = jnp.where(incl, nt(rs_scr[u], ks_scr[u]), 0.0).astype(BF16)
    for _ in range(CHUNK.bit_length() - 2):
        for u in n_unit:
            pw_scr[u] = dot(pw_scr[u], pw_scr[u]).astype(BF16)
        for u in n_unit:
            x_scr[u] = x_scr[u] + dot(x_scr[u].astype(BF16), pw_scr[u])
    for u in n_unit:
        t1_scr[u] = dot(aak_scr[u], vs_scr[u]).astype(BF16)
    for u in n_unit:
        yv_scr[u] = dot(ark_scr[u], vs_scr[u])
    for u in n_unit:
        kv_scr[u] = dot(vst_scr[u], kend_scr[u])
    for u in n_unit:
        wt_scr[u] = dot(x_scr[u].astype(BF16), as_scr[u]).T.astype(BF16)
    for u in n_unit:
        ut_scr[u] = dot(x_scr[u].astype(BF16), t1_scr[u]).T
    for c in range(n_chunk):
        rows = slice(c * CHUNK, (c + 1) * CHUNK)
        us = [c * N_PAIR + p for p in range(N_PAIR)]
        s0 = [s_scr[p] for p in range(N_PAIR)]
        s0b = [a.astype(BF16) for a in s0]
        c_t = [dot(s0b[p], wt_scr[u]) + ut_scr[u] for p, u in enumerate(us)]
        for p, u in enumerate(us):
            s_scr[p] = s0[p] * pt_scr[u, 0:1, :] + dot(c_t[p].astype(BF16), bend_scr[u]) + kv_scr[u]
        for p, u in enumerate(us):
            y_s = nt(rs_scr[u], s0b[p]) + dot(arb_scr[u], c_t[p].T.astype(BF16)) + yv_scr[u]
            y_scr[rows, p * LANE:(p + 1) * LANE] = y_s[0:CHUNK, :] + y_s[CHUNK:n2, :]

    y_ref[...] = _rwkv_finish(y_scr[...], r_scr[...], k_scr[...], v_scr[...], rk_ref[...],
                              gg_ref[...], gb_ref[...], z_scr[:, SHIFT_W:SHIFT_W + BRANCH_W], bd_ref)
    for p in range(N_PAIR):
        st_ref[2 * p] = s_scr[p, 0:HEAD_DIM, 0:HEAD_DIM]
        st_ref[2 * p + 1] = s_scr[p, HEAD_DIM:LANE, HEAD_DIM:LANE]


def _head_block_diag():
    i = jnp.arange(BRANCH_W) // HEAD_DIM
    return (i[:, None] == i[None, :]).astype(BF16)


def _rwkv_call(x, norm_g, w_r, lp):
    b, t, _ = x.shape
    tt = TOK_TILE
    vec = _full((1, BRANCH_W))
    scr = lambda w: pltpu.VMEM((tt, w), F32)
    units = (tt // CHUNK) * N_PAIR
    unit = lambda dt: pltpu.VMEM((units, LANE, LANE), dt)
    return pl.pallas_call(
        _rwkv_kernel,
        grid=(b, t // tt),
        in_specs=[
            pl.BlockSpec((None, tt, D_MODEL), lambda i, j: (i, j, 0)),
            _full((1, D_MODEL)), _full((D_MODEL, N_RWKV_IN)), _full((1, SHIFT_W)),
            vec, _full((2 * LORA, BRANCH_W)), vec, _full((2 * LORA, BRANCH_W)), vec, vec,
            vec, vec, vec, _full((BRANCH_W, BRANCH_W)),
        ],
        out_specs=[
            pl.BlockSpec((None, tt, BRANCH_W), lambda i, j: (i, j, 0)),
            pl.BlockSpec((None, N_HEAD, HEAD_DIM, HEAD_DIM), lambda i, j: (i, 0, 0, 0)),
            pl.BlockSpec((None, 1, SHIFT_W), lambda i, j: (i, 0, 0)),
        ],
        out_shape=[
            jax.ShapeDtypeStruct((b, t, BRANCH_W), BF16),
            jax.ShapeDtypeStruct((b, N_HEAD, HEAD_DIM, HEAD_DIM), F32),
            jax.ShapeDtypeStruct((b, 1, SHIFT_W), F32),
        ],
        scratch_shapes=[
            scr(N_RWKV_IN), pltpu.VMEM((8 + tt, SHIFT_W), F32),
            scr(BRANCH_W), scr(BRANCH_W), scr(BRANCH_W), scr(BRANCH_W), scr(BRANCH_W), scr(BRANCH_W),
            scr(BRANCH_W), pltpu.VMEM((N_PAIR, LANE, LANE), F32),
            scr(BRANCH_W), *([unit(BF16)] * 14), *([unit(F32)] * 4),
            pltpu.VMEM((units, 8, LANE), F32),
        ],
        compiler_params=_params("parallel", "arbitrary"),
        name="rwkv_branch",
    )(x, norm_g, w_r, lp["shift_mu"], lp["w0"], lp["w_lora_b"], lp["a0"], lp["a_lora_b"],
      lp["k_k"], lp["k_a"], lp["r_k"], lp["gn_g"], lp["gn_b"], lp["head_bd"])


def _merge_kernel(x_ref, yc_ref, ya_ref, yr_ref, pe_ref, g_ref, wm_ref, wb_ref, wo_ref, pg_ref,
                  pp_ref, fg_ref, o_ref, mg_scr, *, att_transposed, final_norm):
    x = x_ref[...]
    mg_scr[...] = _bdot(_rms(x, g_ref[...]), wm_ref[...])
    ya = ya_ref[...].T if att_transposed else ya_ref[...]
    s = (_sigmoid(mg_scr[:, 0:D_MODEL]) * _bdot(yc_ref[...], wb_ref[0])
         + _sigmoid(mg_scr[:, D_MODEL:2 * D_MODEL]) * _bdot(ya, wb_ref[1])
         + _sigmoid(mg_scr[:, 2 * D_MODEL:3 * D_MODEL]) * _bdot(yr_ref[...], wb_ref[2]))
    x = x + _bdot(s, wo_ref[...])
    x = x + _sigmoid(_bdot(x, pg_ref[...])) * _bdot(pe_ref[...], pp_ref[...])
    o_ref[...] = _rms(x, fg_ref[...]) if final_norm else x


def _merge_call(x, y_conv, y_att, y_rw, pe, lp, final_g, *, att_transposed, final_norm):
    b, t, _ = x.shape
    tm = min(TOK_TILE, t)
    tok = lambda i, j: (i, j, 0)
    att_spec = (pl.BlockSpec((None, BRANCH_W, tm), lambda i, j: (i, 0, j)) if att_transposed
                else pl.BlockSpec((None, tm, BRANCH_W), tok))
    return pl.pallas_call(
        functools.partial(_merge_kernel, att_transposed=att_transposed, final_norm=final_norm),
        grid=(b, t // tm),
        in_specs=[
            pl.BlockSpec((None, tm, D_MODEL), tok),
            pl.BlockSpec((None, tm, BRANCH_W), tok),
            att_spec,
            pl.BlockSpec((None, tm, BRANCH_W), tok),
            pl.BlockSpec((None, tm, P_DIM), tok),
            _full((1, D_MODEL)), _full((D_MODEL, N_MERGE_IN)), _full((3, BRANCH_W, D_MODEL)),
            _full((D_MODEL, D_MODEL)), _full((D_MODEL, D_MODEL)), _full((P_DIM, D_MODEL)),
            _full((1, D_MODEL)),
        ],
        out_specs=pl.BlockSpec((None, tm, D_MODEL), tok),
        out_shape=jax.ShapeDtypeStruct((b, t, D_MODEL), F32),
        scratch_shapes=[pltpu.VMEM((tm, N_MERGE_IN), F32)],
        compiler_params=_params("parallel", "parallel"),
        name="merge",
    )(x, y_conv, y_att, y_rw, pe, lp["norm_g"], lp["w_merge"], lp["w_branch"], lp["w_out"],
      lp["ple_gate"], lp["ple_proj"], final_g)


def _proj_kernel(x_ref, g_ref, w_ref, z_ref):
    z_ref[...] = _bdot(_rms(x_ref[...], g_ref[...]), w_ref[...])


def _proj_call(x, norm_g, w_in):
    m = x.shape[0]
    n_tiles = 3
    tn = N_IN // n_tiles
    return pl.pallas_call(
        _proj_kernel,
        grid=(n_tiles,),
        in_specs=[_full((m, D_MODEL)), _full((1, D_MODEL)),
                  pl.BlockSpec((D_MODEL, tn), lambda j: (0, j))],
        out_specs=pl.BlockSpec((m, tn), lambda j: (0, j)),
        out_shape=jax.ShapeDtypeStruct((m, N_IN), F32),
        compiler_params=_params("parallel"),
        name="decode_project",
    )(x, norm_g, w_in)


def _row_to_col(row_vec):
    n = row_vec.shape[1]
    eye = lax.broadcasted_iota(jnp.int32, (n, n), 0) == lax.broadcasted_iota(jnp.int32, (n, n), 1)
    return jnp.sum(jnp.where(eye, row_vec, 0.0), axis=1, keepdims=True)


def _col_to_row(col_vec):
    n = col_vec.shape[0]
    eye = lax.broadcasted_iota(jnp.int32, (n, n), 0) == lax.broadcasted_iota(jnp.int32, (n, n), 1)
    return jnp.sum(jnp.where(eye, col_vec, 0.0), axis=0, keepdims=True)


def _mix_kernel(z_ref, buf_ref, st_ref, sh0_ref, cw_ref, cb_ref, lg_ref, lb_ref, cos_ref, sa_ref,
                sb_ref, mu_ref, w0_ref, wlb_ref, a0_ref, alb_ref, kk_ref, ka_ref, rk_ref, gg_ref,
                gb_ref, bd_ref,
                yc_ref, cs_ref, q_ref, k_ref, v_ref, sg_ref, yr_ref, so_ref, sho_ref, ext_scr):
    o_att = N_CONV_IN
    o_rw = N_CONV_IN + N_ATT_IN
    u = z_ref[:, 0:BRANCH_W] * _sigmoid(z_ref[:, BRANCH_W:2 * BRANCH_W])
    ext_scr[0:8, :] = jnp.zeros((8, BRANCH_W), F32)
    ext_scr[2:HIST, :] = buf_ref[...]
    ext_scr[HIST:HIST + 16, :] = jnp.broadcast_to(u, (16, BRANCH_W))
    yc_ref[...] = _conv_branch(ext_scr, HIST, 8, cw_ref, cb_ref[...], lg_ref[...], lb_ref[...],
                               z_ref[:, 2 * BRANCH_W:3 * BRANCH_W])[0:1, :]
    cs_ref[...] = ext_scr[3:HIST + 1, :]
    n = BRANCH_W // LANE
    cos, sa, sb = (_tile_lanes(r[...], n) for r in (cos_ref, sa_ref, sb_ref))
    q_ref[...] = _rope(z_ref[:, o_att:o_att + BRANCH_W], cos, sa, sb) * (HEAD_DIM ** -0.5)
    k_ref[...] = _rope(z_ref[:, o_att + BRANCH_W:o_att + 2 * BRANCH_W], cos, sa, sb)
    v_ref[...] = z_ref[:, o_att + 2 * BRANCH_W:o_att + 3 * BRANCH_W]
    sg_ref[...] = _silu(z_ref[:, o_att + 3 * BRANCH_W:o_att + 4 * BRANCH_W])
    cur = z_ref[:, o_rw:o_rw + SHIFT_W]
    sho_ref[...] = cur
    mixed = cur + (sh0_ref[...] - cur) * mu_ref[...]
    r, k, v, log_decay, kk, a_lr = _rwkv_columns(
        mixed, w0_ref[...], wlb_ref, a0_ref[...], alb_ref, kk_ref[...], ka_ref[...], bd_ref)
    decay = jnp.exp(log_decay)
    b_vec = kk * a_lr
    ys = []
    for h in range(N_HEAD):
        lanes = slice(h * HEAD_DIM, (h + 1) * HEAD_DIM)
        s = st_ref[h]
        sa_col = jnp.sum(s * (-kk[:, lanes]), axis=1, keepdims=True)
        s = s * decay[:, lanes] + sa_col * b_vec[:, lanes] + _row_to_col(v[:, lanes]) * k[:, lanes]
        so_ref[h] = s
        ys.append(_col_to_row(jnp.sum(s * r[:, lanes], axis=1, keepdims=True)))
    y = jnp.concatenate(ys, axis=1)
    yr_ref[...] = _rwkv_finish(y, r, k, v, rk_ref[...], gg_ref[...], gb_ref[...],
                               z_ref[:, o_rw + SHIFT_W:o_rw + SHIFT_W + BRANCH_W], bd_ref)


def _mix_call(z, buf, state, shift0, lp, tables):
    b = z.shape[0]
    row = lambda w: pl.BlockSpec((None, 1, w), lambda i: (i, 0, 0))
    vec = _full((1, BRANCH_W))
    tab = _full((1, LANE))
    o = lambda w, dt=F32: jax.ShapeDtypeStruct((b, 1, w), dt)
    return pl.pallas_call(
        _mix_kernel,
        grid=(b,),
        in_specs=[
            row(N_IN),
            pl.BlockSpec((None, CONV_W - 1, BRANCH_W), lambda i: (i, 0, 0)),
            pl.BlockSpec((None, N_HEAD, HEAD_DIM, HEAD_DIM), lambda i: (i, 0, 0, 0)),
            row(SHIFT_W),
            _full((CONV_W, BRANCH_W)), vec, vec, vec, tab, tab, tab,
            _full((1, SHIFT_W)), vec, _full((2 * LORA, BRANCH_W)), vec, _full((2 * LORA, BRANCH_W)),
            vec, vec, vec, vec, vec, _full((BRANCH_W, BRANCH_W)),
        ],
        out_specs=[
            row(BRANCH_W),
            pl.BlockSpec((None, CONV_W - 1, BRANCH_W), lambda i: (i, 0, 0)),
            row(BRANCH_W), row(BRANCH_W), row(BRANCH_W), row(BRANCH_W), row(BRANCH_W),
            pl.BlockSpec((None, N_HEAD, HEAD_DIM, HEAD_DIM), lambda i: (i, 0, 0, 0)),
            row(SHIFT_W),
        ],
        out_shape=[
            o(BRANCH_W, BF16), jax.ShapeDtypeStruct((b, CONV_W - 1, BRANCH_W), F32),
            o(BRANCH_W), o(BRANCH_W), o(BRANCH_W), o(BRANCH_W), o(BRANCH_W, BF16),
            jax.ShapeDtypeStruct((b, N_HEAD, HEAD_DIM, HEAD_DIM), F32), o(SHIFT_W),
        ],
        scratch_shapes=[pltpu.VMEM((HIST + 16, BRANCH_W), F32)],
        compiler_params=_params("parallel"),
        name="decode_mix",
    )(z.reshape(b, 1, N_IN), buf, state, shift0.reshape(b, 1, SHIFT_W),
      lp["conv_w"], lp["conv_b"], lp["conv_ln_g"], lp["conv_ln_b"], *tables,
      lp["shift_mu"], lp["w0"], lp["w_lora_b"], lp["a0"], lp["a_lora_b"], lp["k_k"], lp["k_a"],
      lp["r_k"], lp["gn_g"], lp["gn_b"], lp["head_bd"])


def _head_rows(q_row):
    lane = lax.broadcasted_iota(jnp.int32, (N_HEAD, BRANCH_W), 1)
    row = lax.broadcasted_iota(jnp.int32, (N_HEAD, BRANCH_W), 0)
    return jnp.where((lane >= row * HEAD_DIM) & (lane < (row + 1) * HEAD_DIM), q_row, 0.0)


def _score_kernel(pt_ref, q_ref, *refs):
    del pt_ref
    pages = refs[:PAGES_PER_STEP]
    s_ref, gate_ref = refs[PAGES_PER_STEP:]
    j = pl.program_id(1)
    page_rows = pages[0].shape[1]

    @pl.when(j == 0)
    def _():
        gate_ref[...] = jnp.zeros(gate_ref.shape, F32)

    qh = _head_rows(q_ref[...])
    lane = lax.broadcasted_iota(jnp.int32, gate_ref.shape, 1)
    pages_per_block = MOBA_BLOCK // page_rows
    gate = gate_ref[...]
    for r in range(PAGES_PER_STEP):
        s = _bdot(qh, pages[r][...])
        s_ref[:, r * page_rows:(r + 1) * page_rows] = s
        blk = (j * PAGES_PER_STEP + r) // pages_per_block
        gate = gate + jnp.where(lane == blk, jnp.sum(s, axis=1, keepdims=True), 0.0)
    gate_ref[...] = gate


def _score_call(page_table, q, cache, layer, n_pool):
    b, n_pages = page_table.shape
    page_rows = cache.shape[2]
    past = n_pages * page_rows
    assert n_pages % PAGES_PER_STEP == 0 and MOBA_BLOCK % page_rows == 0
    assert past // MOBA_BLOCK <= LANE

    def page_spec(r):
        return pl.BlockSpec(
            (None, BRANCH_W, page_rows),
            lambda i, j, pt: (layer * n_pool + pt[i, j * PAGES_PER_STEP + r], 0, 0))

    grid_spec = pltpu.PrefetchScalarGridSpec(
        num_scalar_prefetch=1,
        grid=(b, n_pages // PAGES_PER_STEP),
        in_specs=[pl.BlockSpec((None, 1, BRANCH_W), lambda i, j, pt: (i, 0, 0))]
        + [page_spec(r) for r in range(PAGES_PER_STEP)],
        out_specs=[
            pl.BlockSpec((None, N_HEAD, PAGES_PER_STEP * page_rows), lambda i, j, pt: (i, 0, j)),
            pl.BlockSpec((None, N_HEAD, LANE), lambda i, j, pt: (i, 0, 0)),
        ],
    )
    return pl.pallas_call(
        _score_kernel,
        grid_spec=grid_spec,
        out_shape=[jax.ShapeDtypeStruct((b, N_HEAD, past), F32),
                   jax.ShapeDtypeStruct((b, N_HEAD, LANE), F32)],
        compiler_params=_params("parallel", "arbitrary"),
        name="decode_scores",
    )(page_table, q, *([cache] * PAGES_PER_STEP))


def _select_kernel(s_ref, gate_ref, q_ref, k_ref, p_ref, idx_ref, pown_ref, *, n_blocks):
    lane = lax.broadcasted_iota(jnp.int32, gate_ref.shape, 1).astype(F32)
    gate = jnp.where(lane < n_blocks, gate_ref[...], -jnp.inf)
    key_blk = jnp.right_shift(lax.broadcasted_iota(jnp.int32, s_ref.shape, 1),
                              MOBA_BLOCK.bit_length() - 1).astype(F32)
    sel = jnp.zeros(s_ref.shape, F32)
    idx_out = jnp.zeros(gate_ref.shape, F32)
    for j in range(MOBA_TOPK):
        m = jnp.max(gate, axis=1, keepdims=True)
        idx = jnp.min(jnp.where(gate == m, lane, float(LANE)), axis=1, keepdims=True)
        sel = jnp.where(key_blk == idx, 1.0, sel)
        idx_out = jnp.where(lane == j, idx, idx_out)
        gate = jnp.where(lane == idx, -jnp.inf, gate)
    s_own = jnp.sum(_head_rows(q_ref[...]) * k_ref[...], axis=1, keepdims=True)
    s = jnp.where(sel > 0.0, s_ref[...], NEG)
    m = jnp.maximum(jnp.max(s, axis=1, keepdims=True), s_own)
    e = jnp.exp(s - m)
    e_own = jnp.exp(s_own - m)
    l = jnp.sum(e, axis=1, keepdims=True) + e_own
    p_ref[...] = e / l
    pown_ref[...] = jnp.broadcast_to(e_own / l, pown_ref.shape)
    idx_ref[...] = idx_out.astype(jnp.int32)


def _select_call(scores, gate, q, k_new):
    b, _, past = scores.shape
    n_blocks = past // MOBA_BLOCK
    assert n_blocks >= MOBA_TOPK
    head = lambda w: pl.BlockSpec((None, N_HEAD, w), lambda i: (i, 0, 0))
    row = pl.BlockSpec((None, 1, BRANCH_W), lambda i: (i, 0, 0))
    return pl.pallas_call(
        functools.partial(_select_kernel, n_blocks=n_blocks),
        grid=(b,),
        in_specs=[head(past), head(LANE), row, row],
        out_specs=[head(past), head(LANE), head(LANE)],
        out_shape=[jax.ShapeDtypeStruct((b, N_HEAD, past), F32),
                   jax.ShapeDtypeStruct((b, N_HEAD, LANE), jnp.int32),
                   jax.ShapeDtypeStruct((b, N_HEAD, LANE), F32)],
        compiler_params=_params("parallel"),
        name="decode_select",
    )(scores, gate, q, k_new)


def _gather_kernel(pt_ref, ix_ref, pown_ref, vnew_ref, sg_ref, *refs, n_sel):
    del pt_ref, ix_ref
    p_rows = refs[:n_sel]
    v_pages = refs[n_sel:2 * n_sel]
    y_ref = refs[2 * n_sel]
    h = pl.program_id(1)

    @pl.when(h == 0)
    def _():
        y_ref[...] = jnp.zeros(y_ref.shape, F32)

    acc = pown_ref[pl.ds(h, 1), 0:1] * vnew_ref[...]
    for r in range(n_sel):
        p8 = jnp.broadcast_to(p_rows[r][...], (8, p_rows[r].shape[1]))
        acc = acc + _bdot_nt(p8, v_pages[r][...])[0:1, :]
    lane = lax.broadcasted_iota(jnp.int32, acc.shape, 1)
    mine = (lane >= h * HEAD_DIM) & (lane < (h + 1) * HEAD_DIM)
    y_ref[...] = y_ref[...] + jnp.where(mine, acc * sg_ref[...], 0.0)


def _gather_call(page_table, idx, probs, p_own, v_new, sg, cache, layer, n_pool):
    b, n_pages = page_table.shape
    page_rows = cache.shape[2]
    ppb = MOBA_BLOCK // page_rows
    n_sel = MOBA_TOPK * ppb
    probs = probs.reshape(b, N_HEAD, n_pages, 1, page_rows)

    def seq_page(i, h, ix, r):
        return ix[i, h * MOBA_TOPK + r // ppb] * ppb + r % ppb

    def p_spec(r):
        return pl.BlockSpec((None, None, None, 1, page_rows),
                            lambda i, h, pt, ix: (i, h, seq_page(i, h, ix, r), 0, 0))

    def v_spec(r):
        return pl.BlockSpec((None, BRANCH_W, page_rows),
                            lambda i, h, pt, ix: (layer * n_pool + pt[i, seq_page(i, h, ix, r)], 0, 0))

    row = pl.BlockSpec((None, 1, BRANCH_W), lambda i, h, pt, ix: (i, 0, 0))
    grid_spec = pltpu.PrefetchScalarGridSpec(
        num_scalar_prefetch=2,
        grid=(b, N_HEAD),
        in_specs=[pl.BlockSpec((None, N_HEAD, LANE), lambda i, h, pt, ix: (i, 0, 0)), row, row]
        + [p_spec(r) for r in range(n_sel)] + [v_spec(r) for r in range(n_sel)],
        out_specs=row,
    )
    return pl.pallas_call(
        functools.partial(_gather_kernel, n_sel=n_sel),
        grid_spec=grid_spec,
        out_shape=jax.ShapeDtypeStruct((b, 1, BRANCH_W), F32),
        compiler_params=_params("parallel", "arbitrary"),
        name="decode_gather",
    )(page_table, idx, p_own, v_new, sg, *([probs] * n_sel), *([cache] * n_sel))


def _layer_params(i, norm_g, w_in, conv_w, conv_b, conv_ln_g, conv_ln_b, shift_mu, w0, w_lora_b, a0,
                  a_lora_b, k_k, k_a, r_k, gn_g, gn_b, w_branch, w_out, ple_proj, ple_gate):
    w = w_in[i].astype(BF16)
    o1, o2, o3 = N_CONV_IN, N_CONV_IN + N_ATT_IN, N_CONV_IN + N_ATT_IN + N_RWKV_IN
    row = lambda a: a[i].reshape(1, -1)
    zeros = jnp.zeros((LORA, BRANCH_W), F32)
    return dict(
        norm_g=row(norm_g), w_in=w, w_conv=w[:, :o1], w_att=w[:, o1:o2], w_rwkv=w[:, o2:o3],
        w_merge=w[:, o3:],
        conv_w=conv_w[i], conv_b=row(conv_b), conv_ln_g=row(conv_ln_g), conv_ln_b=row(conv_ln_b),
        shift_mu=row(shift_mu), w0=row(w0), a0=row(a0),
        w_lora_b=jnp.concatenate([w_lora_b[i], zeros], axis=0).astype(BF16),
        a_lora_b=jnp.concatenate([zeros, a_lora_b[i]], axis=0).astype(BF16),
        k_k=row(k_k), k_a=row(k_a), r_k=row(r_k), gn_g=row(gn_g), gn_b=row(gn_b),
        w_branch=w_branch[i].astype(BF16), w_out=w_out[i].astype(BF16),
        ple_proj=ple_proj[i].astype(BF16), ple_gate=ple_gate[i].astype(BF16),
        head_bd=_head_block_diag(),
    )


def _prompt_layer(x, pe, lp, tables, final_g, final_norm):
    b, t, _ = x.shape
    nb = t // MOBA_BLOCK
    y_conv, conv_new = _conv_call(x, lp["norm_g"], lp["w_conv"], lp["conv_w"], lp["conv_b"],
                                  lp["conv_ln_g"], lp["conv_ln_b"])
    k, v, qt, kb, vt, sgt, km = _att1_call(x, lp["norm_g"], lp["w_att"], tables)
    km = km.reshape(b, nb, N_PAIR, LANE).transpose(0, 2, 1, 3)
    y_att_t = _att2_call(qt, kb, vt, km, sgt)
    y_rw, wkv, shift = _rwkv_call(x, lp["norm_g"], lp["w_rwkv"], lp)
    x = _merge_call(x, y_conv, y_att_t, y_rw, pe, lp, final_g,
                    att_transposed=True, final_norm=final_norm)
    return x, k, v, conv_new, wkv, shift.reshape(b, SHIFT_W)


def _sample_layer(x, pe, lp, tables, final_g, final_norm, layer, n_pool, cache_k, cache_v,
                  page_table, buf, state, shift0):
    b = x.shape[0]
    z = _proj_call(x.reshape(b, D_MODEL), lp["norm_g"], lp["w_in"])
    y_conv, conv_new, q, k, v, sg, y_rw, wkv, shift = _mix_call(z, buf, state, shift0, lp, tables)
    scores, gate = _score_call(page_table, q, cache_k, layer, n_pool)
    probs, idx, p_own = _select_call(scores, gate, q, k)
    idx = idx[:, :, :MOBA_TOPK].reshape(b, N_HEAD * MOBA_TOPK)
    y_att = _gather_call(page_table, idx, probs, p_own, v, sg, cache_v, layer, n_pool)
    tok = lambda a: a.reshape(1, b, -1)
    x = _merge_call(tok(x), tok(y_conv), tok(y_att), tok(y_rw), tok(pe), lp, final_g,
                    att_transposed=False, final_norm=final_norm)
    return (x.reshape(b, 1, D_MODEL), k, v, conv_new, wkv, shift.reshape(b, SHIFT_W))


def kernel(x_prompt, x_sample, cache_k, cache_v, page_table, state_conv, state_wkv, state_shift,
           p_prompt, p_sample, norm_g, w_in, conv_w, conv_b, conv_ln_g, conv_ln_b, shift_mu, w0,
           w_lora_b, a0, a_lora_b, k_k, k_a, r_k, gn_g, gn_b, w_branch, w_out, ple_proj, ple_gate,
           final_norm_g):
    depth = w_in.shape[0]
    b_p, t_p, _ = x_prompt.shape
    b_s, t_s, _ = x_sample.shape
    assert t_s == 1 and t_p % TOK_TILE == 0 and TOK_TILE == MOBA_BLOCK
    n_pool, page_rows = cache_k.shape[1], cache_k.shape[2]
    past_len = page_table.shape[1] * page_rows
    assert past_len % MOBA_BLOCK == 0
    tables_p = _rope_tables(jnp.arange(t_p, dtype=jnp.int32))
    tables_s = _rope_tables(past_len + jnp.arange(1, dtype=jnp.int32))
    by_page = lambda c: c.transpose(0, 1, 3, 4, 2).reshape(depth * n_pool, BRANCH_W, page_rows)
    cache_k, cache_v = by_page(cache_k), by_page(cache_v)
    final_g = final_norm_g.reshape(1, D_MODEL)
    xp, xs = x_prompt, x_sample
    outs = [[] for _ in range(10)]
    for i in range(depth):
        lp = _layer_params(i, norm_g, w_in, conv_w, conv_b, conv_ln_g, conv_ln_b, shift_mu, w0,
                           w_lora_b, a0, a_lora_b, k_k, k_a, r_k, gn_g, gn_b, w_branch, w_out,
                           ple_proj, ple_gate)
        last = i == depth - 1
        xp, kp, vp, cp, wp, sp = _prompt_layer(xp, p_prompt[i], lp, tables_p, final_g, last)
        xs, ks, vs, cs, ws, ss = _sample_layer(
            xs, p_sample[i], lp, tables_s, final_g, last, i, n_pool, cache_k, cache_v, page_table,
            state_conv[i], state_wkv[i], state_shift[i])
        heads_p = lambda a: a.reshape(b_p, t_p, N_HEAD, HEAD_DIM)
        heads_s = lambda a: a.reshape(b_s, 1, N_HEAD, HEAD_DIM)
        for lst, a in zip(outs, (heads_p(kp), heads_p(vp), heads_s(ks), heads_s(vs),
                                 cp, cs, wp, ws, sp, ss)):
            lst.append(a)
    return (xp, xs) + tuple(jnp.stack(lst) for lst in outs)
```

```python
import functools

import jax
import jax.numpy as jnp
from jax import lax
from jax.experimental import pallas as pl
from jax.experimental.pallas import tpu as pltpu

D_MODEL = 1024
P_DIM = 256
HEAD_DIM = 64
BRANCH_W = 512
N_HEAD = 8
N_PAIR = 4
CONV_W = 31
ROT_DIM = 16
ROPE_THETA = 500000.0
MOBA_BLOCK = 256
MOBA_TOPK = 3
LORA = 64
SHIFT_W = 3 * BRANCH_W + 2 * LORA
N_CONV_IN = 3 * BRANCH_W
N_ATT_IN = 4 * BRANCH_W
N_RWKV_IN = SHIFT_W + BRANCH_W
N_MERGE_IN = 3 * D_MODEL
N_IN = N_CONV_IN + N_ATT_IN + N_RWKV_IN + N_MERGE_IN
NORM_EPS = 1e-6
LN_EPS = 1e-5
GN_EPS = 64e-5
NEG = -1e30
LOG2_E = 1.4426950408889634
DECAY_SCALE = 0.6065306597126334

LANE = 128
TOK_TILE = 256
V_ROWS = 80
BIAS_ROWS = 16
CHUNK = 64
HIST = 32
CONV_ROWS = 32
PAGES_PER_STEP = 16
VMEM_LIMIT = 56 * 1024 * 1024

F32 = jnp.float32
BF16 = jnp.bfloat16


def _bdot(a, b):
    return jnp.dot(a.astype(BF16), b.astype(BF16), preferred_element_type=F32)


def _bdot_nt(a, b):
    return lax.dot_general(a.astype(BF16), b.astype(BF16), (((1,), (1,)), ((), ())),
                           preferred_element_type=F32)


def _split_dot(x, m_bf16):
    hi = x.astype(BF16)
    lo = (x - hi.astype(F32)).astype(BF16)
    return (jnp.dot(hi, m_bf16, preferred_element_type=F32)
            + jnp.dot(lo, m_bf16, preferred_element_type=F32))


def _split3_dot_left(m_bf16, x):
    hi = x.astype(BF16)
    r1 = x - hi.astype(F32)
    mid = r1.astype(BF16)
    lo = (r1 - mid.astype(F32)).astype(BF16)
    return (jnp.dot(m_bf16, hi, preferred_element_type=F32)
            + jnp.dot(m_bf16, mid, preferred_element_type=F32)
            + jnp.dot(m_bf16, lo, preferred_element_type=F32))


def _sigmoid(x):
    return 1.0 / (1.0 + jnp.exp(-x))


def _silu(x):
    return x * _sigmoid(x)


def _rms(x, g):
    return x * lax.rsqrt(jnp.mean(x * x, axis=-1, keepdims=True) + NORM_EPS) * g


def _params(*sem):
    return pltpu.CompilerParams(dimension_semantics=sem, vmem_limit_bytes=VMEM_LIMIT)


def _full(shape):
    nd = len(shape)
    return pl.BlockSpec(shape, lambda *_: (0,) * nd)


def _conv_branch(ext_ref, base, rows, cw_ref, cb, lg, lb, g_conv):
    acc = jnp.zeros((rows, BRANCH_W), F32) + cb
    first = base - (CONV_W - 1)
    for s in range(8):
        z = None
        for j in range(CONV_W):
            if (first + j) % 8 == s:
                start = first + j - s
                term = ext_ref[start:start + rows + 8, :] * cw_ref[j:j + 1, :]
                z = term if z is None else z + term
        acc = acc + z[s:s + rows, :]
    mu = jnp.mean(acc, axis=-1, keepdims=True)
    d = acc - mu
    var = jnp.mean(d * d, axis=-1, keepdims=True)
    y = d * lax.rsqrt(var + LN_EPS) * lg + lb
    return (_silu(y) * _silu(g_conv)).astype(BF16)


def _conv_kernel(x_ref, g_ref, w_ref, cw_ref, cb_ref, lg_ref, lb_ref, y_ref, cs_ref, z_scr, ext_scr):
    tt = x_ref.shape[0]

    @pl.when(pl.program_id(1) == 0)
    def _():
        ext_scr[0:HIST, :] = jnp.zeros((HIST, BRANCH_W), F32)
        ext_scr[HIST + tt:HIST + tt + 8, :] = jnp.zeros((8, BRANCH_W), F32)

    z_scr[...] = _bdot(_rms(x_ref[...], g_ref[...]), w_ref[...])
    ext_scr[HIST:HIST + tt, :] = z_scr[:, 0:BRANCH_W] * _sigmoid(z_scr[:, BRANCH_W:2 * BRANCH_W])
    cb, lg, lb = cb_ref[...], lg_ref[...], lb_ref[...]
    for r in range(0, tt, CONV_ROWS):
        y_ref[r:r + CONV_ROWS, :] = _conv_branch(
            ext_scr, HIST + r, CONV_ROWS, cw_ref, cb, lg, lb,
            z_scr[r:r + CONV_ROWS, 2 * BRANCH_W:3 * BRANCH_W])
    cs_ref[...] = ext_scr[HIST + tt - (CONV_W - 1):HIST + tt, :]
    ext_scr[0:HIST, :] = ext_scr[tt:tt + HIST, :]


def _conv_call(x, norm_g, w_c, conv_w, conv_b, ln_g, ln_b):
    b, t, _ = x.shape
    tt = TOK_TILE
    return pl.pallas_call(
        _conv_kernel,
        grid=(b, t // tt),
        in_specs=[
            pl.BlockSpec((None, tt, D_MODEL), lambda i, j: (i, j, 0)),
            _full((1, D_MODEL)), _full((D_MODEL, N_CONV_IN)), _full((CONV_W, BRANCH_W)),
            _full((1, BRANCH_W)), _full((1, BRANCH_W)), _full((1, BRANCH_W)),
        ],
        out_specs=[
            pl.BlockSpec((None, tt, BRANCH_W), lambda i, j: (i, j, 0)),
            pl.BlockSpec((None, CONV_W - 1, BRANCH_W), lambda i, j: (i, 0, 0)),
        ],
        out_shape=[
            jax.ShapeDtypeStruct((b, t, BRANCH_W), BF16),
            jax.ShapeDtypeStruct((b, CONV_W - 1, BRANCH_W), F32),
        ],
        scratch_shapes=[pltpu.VMEM((tt, N_CONV_IN), F32),
                        pltpu.VMEM((HIST + tt + 8, BRANCH_W), F32)],
        compiler_params=_params("parallel", "arbitrary"),
        name="conv_branch",
    )(x, norm_g, w_c, conv_w, conv_b, ln_g, ln_b)


def _rope_tables(pos):
    half = ROT_DIM // 2
    inv = jnp.power(ROPE_THETA, -jnp.arange(half, dtype=F32) * (2.0 / ROT_DIM))
    ang = pos.astype(F32)[:, None] * inv[None, :]
    cos, sin = jnp.cos(ang), jnp.sin(ang)
    n = pos.shape[0]
    pad = jnp.zeros((n, HEAD_DIM - ROT_DIM), F32)
    cos64 = jnp.concatenate([cos, cos, pad + 1.0], axis=1)
    sa64 = jnp.concatenate([-sin, jnp.zeros_like(sin), pad], axis=1)
    sb64 = jnp.concatenate([jnp.zeros_like(sin), sin, pad], axis=1)
    return tuple(jnp.concatenate([a, a], axis=1) for a in (cos64, sa64, sb64))


def _rope(a, cos, sa, sb):
    w = a.shape[1]
    return a * cos + pltpu.roll(a, w - ROT_DIM // 2, 1) * sa + pltpu.roll(a, ROT_DIM // 2, 1) * sb


def _tile_lanes(a, n):
    return jnp.concatenate([a] * n, axis=1)


def _att1_kernel(x_ref, g_ref, w_ref, cos_ref, sa_ref, sb_ref,
                 k_ref, v_ref, qt_ref, kb_ref, vt_ref, sgt_ref, km_ref, z_scr):
    z_scr[...] = _bdot(_rms(x_ref[...], g_ref[...]), w_ref[...])
    n = BRANCH_W // LANE
    cos, sa, sb = (_tile_lanes(r[...], n) for r in (cos_ref, sa_ref, sb_ref))
    q = _rope(z_scr[:, 0:BRANCH_W], cos, sa, sb) * (LOG2_E * HEAD_DIM ** -0.5)
    k = _rope(z_scr[:, BRANCH_W:2 * BRANCH_W], cos, sa, sb)
    v = z_scr[:, 2 * BRANCH_W:3 * BRANCH_W]
    k_ref[...] = k
    v_ref[...] = v
    qt_ref[...] = q.T.astype(BF16)
    lane = lax.broadcasted_iota(jnp.int32, (k.shape[0], LANE), 1)
    one_hot = jnp.where(lane == HEAD_DIM, 1.0, 0.0)
    for p in range(N_PAIR):
        slab = k[:, p * LANE:(p + 1) * LANE]
        kb_ref[2 * p] = jnp.where(lane < HEAD_DIM, slab, one_hot).astype(BF16)
        kb_ref[2 * p + 1] = jnp.where(lane < HEAD_DIM, pltpu.roll(slab, HEAD_DIM, 1),
                                      one_hot).astype(BF16)
    v_t = v.T
    tail = jnp.where(lax.broadcasted_iota(jnp.int32, (V_ROWS - HEAD_DIM, v_t.shape[1]), 0) == 0,
                     1.0, 0.0)
    for h in range(N_HEAD):
        vt_ref[h * V_ROWS:(h + 1) * V_ROWS, :] = jnp.concatenate(
            [v_t[h * HEAD_DIM:(h + 1) * HEAD_DIM, :], tail], axis=0).astype(BF16)
    sgt_ref[...] = _silu(z_scr[:, 3 * BRANCH_W:4 * BRANCH_W]).T
    km_ref[...] = jnp.mean(k, axis=0, keepdims=True)


def _att1_call(x, norm_g, w_a, tables):
    b, t, _ = x.shape
    tt = TOK_TILE
    nb = t // tt
    tok = lambda i, j: (i, j, 0)
    tr = lambda i, j: (i, 0, j)
    return pl.pallas_call(
        _att1_kernel,
        grid=(b, nb),
        in_specs=[
            pl.BlockSpec((None, tt, D_MODEL), tok),
            _full((1, D_MODEL)), _full((D_MODEL, N_ATT_IN)),
            pl.BlockSpec((tt, LANE), lambda i, j: (j, 0)),
            pl.BlockSpec((tt, LANE), lambda i, j: (j, 0)),
            pl.BlockSpec((tt, LANE), lambda i, j: (j, 0)),
        ],
        out_specs=[
            pl.BlockSpec((None, tt, BRANCH_W), tok),
            pl.BlockSpec((None, tt, BRANCH_W), tok),
            pl.BlockSpec((None, BRANCH_W, tt), tr),
            pl.BlockSpec((None, N_HEAD, tt, LANE), lambda i, j: (i, 0, j, 0)),
            pl.BlockSpec((None, None, N_HEAD * V_ROWS, tt), lambda i, j: (i, j, 0, 0)),
            pl.BlockSpec((None, BRANCH_W, tt), tr),
            pl.BlockSpec((None, None, 1, BRANCH_W), lambda i, j: (i, j, 0, 0)),
        ],
        out_shape=[
            jax.ShapeDtypeStruct((b, t, BRANCH_W), F32),
            jax.ShapeDtypeStruct((b, t, BRANCH_W), F32),
            jax.ShapeDtypeStruct((b, BRANCH_W, t), BF16),
            jax.ShapeDtypeStruct((b, N_HEAD, t, LANE), BF16),
            jax.ShapeDtypeStruct((b, nb, N_HEAD * V_ROWS, tt), BF16),
            jax.ShapeDtypeStruct((b, BRANCH_W, t), F32),
            jax.ShapeDtypeStruct((b, nb, 1, BRANCH_W), F32),
        ],
        scratch_shapes=[pltpu.VMEM((tt, N_ATT_IN), F32)],
        compiler_params=_params("parallel", "parallel"),
        name="att_project",
    )(x, norm_g, w_a, *tables)


def _top_blocks(gate, n_slots):
    nb = gate.shape[0]
    row = lax.broadcasted_iota(jnp.int32, gate.shape, 0).astype(F32)
    sel = jnp.zeros(gate.shape, F32)
    for j in range(MOBA_TOPK):
        m = jnp.max(gate, axis=0, keepdims=True)
        idx = jnp.min(jnp.where(gate == m, row, float(nb)), axis=0, keepdims=True)
        hit = row == jnp.where(j < n_slots, idx, -1.0)
        sel = jnp.where(hit, 1.0, sel)
        gate = jnp.where(row == idx, -jnp.inf, gate)
    return sel


def _att2_kernel(qt_ref, kb_ref, vt_ref, km_ref, sgt_ref, yt_ref,
                 qp_scr, bias_scr, s_scr, e_scr, alpha_scr, m_scr, acc_scr):
    i = pl.program_id(1)
    tq = qt_ref.shape[1]
    nb = km_ref.shape[1]
    blk = lax.broadcasted_iota(jnp.int32, (nb, tq), 0)
    qp_scr[...] = jnp.zeros(qp_scr.shape, BF16)
    for h in range(N_HEAD):
        q_h = qt_ref[h * HEAD_DIM:(h + 1) * HEAD_DIM, :]
        qp_scr[h, 0:HEAD_DIM, :] = q_h
        gate = jnp.dot(km_ref[h].astype(BF16), q_h, preferred_element_type=F32)
        sel = _top_blocks(jnp.where(blk < i, gate, -jnp.inf), i)
        bias_scr[h] = jnp.where(sel > 0.0, 0.0, NEG)
    m_scr[...] = jnp.full(m_scr.shape, NEG, F32)
    acc_scr[...] = jnp.zeros(acc_scr.shape, F32)
    bias_row0 = lax.broadcasted_iota(jnp.int32, (BIAS_ROWS, tq), 0) == 0

    def scores(n, slot, bias_of, extra=None):
        start = pl.multiple_of(n * MOBA_BLOCK, MOBA_BLOCK)
        for h in range(N_HEAD):
            qp_scr[h, HEAD_DIM:HEAD_DIM + BIAS_ROWS, :] = jnp.where(
                bias_row0, bias_of(h), 0.0).astype(BF16)
            s = jnp.dot(kb_ref[h, pl.ds(start, MOBA_BLOCK), :], qp_scr[h],
                        preferred_element_type=F32)
            s_scr[slot, h] = s if extra is None else s + extra

    def softmax(slot):
        for h in range(N_HEAD):
            m = m_scr[h:h + 1, :]
            m_new = jnp.maximum(m, jnp.max(s_scr[slot, h], axis=0, keepdims=True))
            m_scr[h:h + 1, :] = m_new
            alpha_scr[slot, h:h + 1, :] = jnp.exp2(m - m_new)
            e_scr[slot, h] = jnp.exp2(s_scr[slot, h] - m_new).astype(BF16)

    def values(n, slot):
        for h in range(N_HEAD):
            pv = jnp.dot(vt_ref[n, h * V_ROWS:(h + 1) * V_ROWS, :], e_scr[slot, h],
                         preferred_element_type=F32)
            acc_scr[h] = alpha_scr[slot, h:h + 1, :] * acc_scr[h] + pv

    def row_bias(n):
        return lambda h: bias_scr[h, pl.ds(n, 1), :]

    def block_of(v):
        return jnp.where(v == 0, i, jnp.minimum(v - 1, nb - 1))

    def step(t, slot):
        nxt = block_of(t + 1)
        scores(nxt, 1 - slot, row_bias(nxt))
        values(block_of(t - 1), 1 - slot)
        softmax(slot)

    causal = jnp.where(lax.broadcasted_iota(jnp.int32, (MOBA_BLOCK, tq), 0)
                       <= lax.broadcasted_iota(jnp.int32, (MOBA_BLOCK, tq), 1), 0.0, NEG)
    scores(i, 0, lambda h: jnp.zeros((1, tq), F32), extra=causal)
    softmax(0)
    scores(0, 1, row_bias(0))

    def body(j, carry):
        step(2 * j + 1, 1)
        step(2 * j + 2, 0)
        return carry

    trips = (i + 1) // 2
    lax.fori_loop(0, trips, body, 0)
    values(block_of(2 * trips), 0)
    for h in range(N_HEAD):
        rows = slice(h * HEAD_DIM, (h + 1) * HEAD_DIM)
        denom = acc_scr[h, HEAD_DIM:HEAD_DIM + 1, :]
        yt_ref[rows, :] = acc_scr[h, 0:HEAD_DIM, :] / denom * sgt_ref[rows, :]


def _att2_call(qt, kb, vt, km, sgt):
    b, _, t = qt.shape
    nb = vt.shape[1]
    tq = MOBA_BLOCK
    return pl.pallas_call(
        _att2_kernel,
        grid=(b, t // tq),
        in_specs=[
            pl.BlockSpec((None, BRANCH_W, tq), lambda bi, i: (bi, 0, i)),
            pl.BlockSpec((None, N_HEAD, t, LANE), lambda bi, i: (bi, 0, 0, 0),
                         pipeline_mode=pl.Buffered(1)),
            pl.BlockSpec((None, nb, N_HEAD * V_ROWS, MOBA_BLOCK), lambda bi, i: (bi, 0, 0, 0),
                         pipeline_mode=pl.Buffered(1)),
            pl.BlockSpec((None, N_HEAD, nb, HEAD_DIM), lambda bi, i: (bi, 0, 0, 0)),
            pl.BlockSpec((None, BRANCH_W, tq), lambda bi, i: (bi, 0, i)),
        ],
        out_specs=pl.BlockSpec((None, BRANCH_W, tq), lambda bi, i: (bi, 0, i)),
        out_shape=jax.ShapeDtypeStruct((b, BRANCH_W, t), F32),
        scratch_shapes=[
            pltpu.VMEM((N_HEAD, LANE, tq), BF16), pltpu.VMEM((N_HEAD, nb, tq), F32),
            pltpu.VMEM((2, N_HEAD, MOBA_BLOCK, tq), F32), pltpu.VMEM((2, N_HEAD, MOBA_BLOCK, tq), BF16),
            pltpu.VMEM((2, N_HEAD, tq), F32), pltpu.VMEM((N_HEAD, tq), F32),
            pltpu.VMEM((N_HEAD, V_ROWS, tq), F32),
        ],
        compiler_params=_params("parallel", "arbitrary"),
        name="moba_attention",
    )(qt, kb, vt, km, sgt)


def _rwkv_columns(mixed, w0, wlb_ref, a0, alb_ref, kk_scale, k_a, bd_ref):
    r = mixed[:, 0:BRANCH_W]
    k = mixed[:, BRANCH_W:2 * BRANCH_W]
    v = mixed[:, 2 * BRANCH_W:3 * BRANCH_W]
    lora = mixed[:, 3 * BRANCH_W:3 * BRANCH_W + 2 * LORA]
    lane = lax.broadcasted_iota(jnp.int32, lora.shape, 1)
    lora = jnp.where(lane < LORA, jnp.tanh(lora), lora)
    log_decay = -(DECAY_SCALE * _sigmoid(w0 + _bdot(lora, wlb_ref[...])))
    a_lr = _sigmoid(a0 + _bdot(lora, alb_ref[...]))
    kk = k * kk_scale
    kk = kk * lax.rsqrt(jnp.maximum(_bdot(kk * kk, bd_ref[...]), 1e-24))
    k = k * (1.0 + (a_lr - 1.0) * k_a)
    return r, k, v, log_decay, kk, a_lr


def _rwkv_finish(y, r, k, v, r_k, gn_g, gn_b, g_rwkv, bd_ref):
    m = _split_dot(y, bd_ref[...]) * (1.0 / HEAD_DIM)
    d = y - m
    var = _bdot(d * d, bd_ref[...]) * (1.0 / HEAD_DIM)
    yn = d * lax.rsqrt(var + GN_EPS) * gn_g + gn_b
    bonus = _bdot(r * k * r_k, bd_ref[...]) * v
    return ((yn + bonus) * _silu(g_rwkv)).astype(BF16)


def _stack_heads(a, mask_a):
    return jnp.concatenate([jnp.where(mask_a, a, 0.0), jnp.where(mask_a, 0.0, a)], axis=0)


def _rwkv_kernel(x_ref, g_ref, w_ref, mu_ref, w0_ref, wlb_ref, a0_ref, alb_ref, kk_ref, ka_ref,
                 rk_ref, gg_ref, gb_ref, bd_ref,
                 y_ref, st_ref, sh_ref,
                 z_scr, shs_scr, r_scr, k_scr, v_scr, ld_scr, a_scr, b_scr, y_scr, s_scr,
                 cum_scr, ar_scr, bk_scr, at_scr, vs_scr, vst_scr, bend_scr, kend_scr, pw_scr, akrk_scr,
                 arb_scr, wu_scr, wut_scr, g_scr, q_scr, x_scr, yv_scr, kv_scr, h_scr, y0_scr, pt_scr):
    tt = x_ref.shape[0]

    @pl.when(pl.program_id(1) == 0)
    def _():
        shs_scr[0:8, :] = jnp.zeros((8, SHIFT_W), F32)
        s_scr[...] = jnp.zeros(s_scr.shape, F32)

    z_scr[...] = _bdot(_rms(x_ref[...], g_ref[...]), w_ref[...])
    shs_scr[8:8 + tt, :] = z_scr[:, 0:SHIFT_W]
    cur = z_scr[:, 0:SHIFT_W]
    mixed = cur + (shs_scr[7:7 + tt, :] - cur) * mu_ref[...]
    sh_ref[...] = shs_scr[7 + tt:8 + tt, :]
    shs_scr[7:8, :] = shs_scr[7 + tt:8 + tt, :]

    r, k, v, log_decay, kk, a_lr = _rwkv_columns(
        mixed, w0_ref[...], wlb_ref, a0_ref[...], alb_ref, kk_ref[...], ka_ref[...], bd_ref)
    r_scr[...] = r
    k_scr[...] = k
    v_scr[...] = v
    ld_scr[...] = log_decay
    a_scr[...] = -kk
    b_scr[...] = kk * a_lr

    n2 = 2 * CHUNK
    row = lax.broadcasted_iota(jnp.int32, (n2, n2), 0)
    col = lax.broadcasted_iota(jnp.int32, (n2, n2), 1)
    same = (row >= CHUNK) == (col >= CHUNK)
    rt = jnp.where(row >= CHUNK, row - CHUNK, row)
    ct = jnp.where(col >= CHUNK, col - CHUNK, col)
    strict = same & (rt > ct)
    incl = same & (rt >= ct)
    crow = lax.broadcasted_iota(jnp.int32, (CHUNK, CHUNK), 0)
    ccol = lax.broadcasted_iota(jnp.int32, (CHUNK, CHUNK), 1)
    tri = jnp.where(crow >= ccol, 1.0, 0.0).astype(BF16)
    mask_a = lax.broadcasted_iota(jnp.int32, (CHUNK, LANE), 1) < HEAD_DIM

    n_chunk = tt // CHUNK
    units = [(c, p) for c in range(n_chunk) for p in range(N_PAIR)]
    n_unit = range(len(units))
    dot = functools.partial(jnp.dot, preferred_element_type=F32)

    def nt(a, b):
        return lax.dot_general(a, b, (((1,), (1,)), ((), ())), preferred_element_type=F32)

    def blk(ref, c, p):
        return ref[c * CHUNK:(c + 1) * CHUNK, p * LANE:(p + 1) * LANE]

    for c in range(n_chunk):
        rows = slice(c * CHUNK, (c + 1) * CHUNK)
        cum_scr[rows, :] = _split3_dot_left(tri, ld_scr[rows, :])
    for u, (c, p) in enumerate(units):
        cum = blk(cum_scr, c, p)
        last = cum[CHUNK - 1:CHUNK, :]
        p_inv = jnp.exp(-cum)
        p_end = jnp.exp(last - cum)
        bb = blk(b_scr, c, p)
        kc = blk(k_scr, c, p)
        v_s = _stack_heads(blk(v_scr, c, p), mask_a)
        a_s = _stack_heads(blk(a_scr, c, p) * jnp.exp(cum - blk(ld_scr, c, p)),
                           mask_a).astype(BF16)
        ar_scr[u, 0:n2, :] = a_s
        ar_scr[u, n2:2 * n2, :] = _stack_heads(blk(r_scr, c, p) * jnp.exp(cum), mask_a).astype(BF16)
        bk_scr[u, 0:n2, :] = _stack_heads(bb * p_inv, mask_a).astype(BF16)
        bk_scr[u, n2:2 * n2, :] = _stack_heads(kc * p_inv, mask_a).astype(BF16)
        at_scr[u, :, 0:n2] = a_s
        vs_scr[u] = v_s.astype(BF16)
        vst_scr[u] = v_s.T.astype(BF16)
        bend_scr[u] = _stack_heads(bb * p_end, mask_a).astype(BF16)
        kend_scr[u] = _stack_heads(kc * p_end, mask_a).astype(BF16)
        pt_scr[u] = jnp.broadcast_to(jnp.exp(last), (8, LANE))
    eye = jnp.where(row == col, 1.0, 0.0)
    for u in n_unit:
        quad = nt(ar_scr[u], bk_scr[u])
        a_ab = jnp.where(strict, quad[0:n2, 0:n2], 0.0)
        x_scr[u] = eye + a_ab
        pw_scr[u] = a_ab.astype(BF16)
        akrk_scr[u, 0:n2, :] = jnp.where(strict, quad[0:n2, n2:2 * n2], 0.0).astype(BF16)
        akrk_scr[u, n2:2 * n2, :] = jnp.where(incl, quad[n2:2 * n2, n2:2 * n2], 0.0).astype(BF16)
        arb_scr[u] = jnp.where(incl, quad[n2:2 * n2, 0:n2], 0.0).astype(BF16)
    for _ in range(CHUNK.bit_length() - 2):
        for u in n_unit:
            pw_scr[u] = dot(pw_scr[u], pw_scr[u]).astype(BF16)
        for u in n_unit:
            x_scr[u] = x_scr[u] + dot(x_scr[u].astype(BF16), pw_scr[u])
    for u in n_unit:
        ty = dot(akrk_scr[u], vs_scr[u])
        at_scr[u, :, n2:2 * n2] = ty[0:n2, :].astype(BF16)
        yv_scr[u] = ty[n2:2 * n2, :]
    for u in n_unit:
        kv_scr[u] = dot(vst_scr[u], kend_scr[u])
    for u in n_unit:
        wu = dot(x_scr[u].astype(BF16), at_scr[u])
        wu_scr[u] = wu.astype(BF16)
        wut_scr[u] = wu.T.astype(BF16)
    for u in n_unit:
        gh = dot(wut_scr[u], bend_scr[u])
        g_scr[u] = gh[0:n2, :].astype(BF16)
        h_scr[u] = gh[n2:2 * n2, :] + kv_scr[u]
    for u in n_unit:
        qy = dot(arb_scr[u], wu_scr[u])
        q_scr[u] = (ar_scr[u, n2:2 * n2, :].astype(F32) + qy[:, 0:n2]).astype(BF16)
        y0_scr[u] = qy[:, n2:2 * n2] + yv_scr[u]
    for c in range(n_chunk):
        rows = slice(c * CHUNK, (c + 1) * CHUNK)
        us = [c * N_PAIR + p for p in range(N_PAIR)]
        s0 = [s_scr[p] for p in range(N_PAIR)]
        s0b = [a.astype(BF16) for a in s0]
        for p, u in enumerate(us):
            s_scr[p] = s0[p] * pt_scr[u, 0:1, :] + dot(s0b[p], g_scr[u]) + h_scr[u]
        for p, u in enumerate(us):
            y_s = nt(q_scr[u], s0b[p]) + y0_scr[u]
            y_scr[rows, p * LANE:(p + 1) * LANE] = y_s[0:CHUNK, :] + y_s[CHUNK:n2, :]

    y_ref[...] = _rwkv_finish(y_scr[...], r_scr[...], k_scr[...], v_scr[...], rk_ref[...],
                              gg_ref[...], gb_ref[...], z_scr[:, SHIFT_W:SHIFT_W + BRANCH_W], bd_ref)
    for p in range(N_PAIR):
        st_ref[2 * p] = s_scr[p, 0:HEAD_DIM, 0:HEAD_DIM]
        st_ref[2 * p + 1] = s_scr[p, HEAD_DIM:LANE, HEAD_DIM:LANE]


def _head_block_diag():
    i = jnp.arange(BRANCH_W) // HEAD_DIM
    return (i[:, None] == i[None, :]).astype(BF16)


def _rwkv_call(x, norm_g, w_r, lp):
    b, t, _ = x.shape
    tt = TOK_TILE
    vec = _full((1, BRANCH_W))
    scr = lambda w: pltpu.VMEM((tt, w), F32)
    units = (tt // CHUNK) * N_PAIR
    unit = lambda dt, r=1, c=1: pltpu.VMEM((units, r * LANE, c * LANE), dt)
    return pl.pallas_call(
        _rwkv_kernel,
        grid=(b, t // tt),
        in_specs=[
            pl.BlockSpec((None, tt, D_MODEL), lambda i, j: (i, j, 0)),
            _full((1, D_MODEL)), _full((D_MODEL, N_RWKV_IN)), _full((1, SHIFT_W)),
            vec, _full((2 * LORA, BRANCH_W)), vec, _full((2 * LORA, BRANCH_W)), vec, vec,
            vec, vec, vec, _full((BRANCH_W, BRANCH_W)),
        ],
        out_specs=[
            pl.BlockSpec((None, tt, BRANCH_W), lambda i, j: (i, j, 0)),
            pl.BlockSpec((None, N_HEAD, HEAD_DIM, HEAD_DIM), lambda i, j: (i, 0, 0, 0)),
            pl.BlockSpec((None, 1, SHIFT_W), lambda i, j: (i, 0, 0)),
        ],
        out_shape=[
            jax.ShapeDtypeStruct((b, t, BRANCH_W), BF16),
            jax.ShapeDtypeStruct((b, N_HEAD, HEAD_DIM, HEAD_DIM), F32),
            jax.ShapeDtypeStruct((b, 1, SHIFT_W), F32),
        ],
        scratch_shapes=[
            scr(N_RWKV_IN), pltpu.VMEM((8 + tt, SHIFT_W), F32),
            scr(BRANCH_W), scr(BRANCH_W), scr(BRANCH_W), scr(BRANCH_W), scr(BRANCH_W), scr(BRANCH_W),
            scr(BRANCH_W), pltpu.VMEM((N_PAIR, LANE, LANE), F32),
            scr(BRANCH_W), unit(BF16, 2), unit(BF16, 2), unit(BF16, 1, 2), *([unit(BF16)] * 5),
            unit(BF16, 2), unit(BF16), unit(BF16, 1, 2), unit(BF16, 2), unit(BF16), unit(BF16),
            *([unit(F32)] * 5), pltpu.VMEM((units, 8, LANE), F32),
        ],
        compiler_params=_params("parallel", "arbitrary"),
        name="rwkv_branch",
    )(x, norm_g, w_r, lp["shift_mu"], lp["w0"], lp["w_lora_b"], lp["a0"], lp["a_lora_b"],
      lp["k_k"], lp["k_a"], lp["r_k"], lp["gn_g"], lp["gn_b"], lp["head_bd"])


def _merge_kernel(x_ref, yc_ref, ya_ref, yr_ref, pe_ref, g_ref, wm_ref, wb_ref, wo_ref, pg_ref,
                  pp_ref, fg_ref, o_ref, mg_scr, *, att_transposed, final_norm):
    x = x_ref[...]
    mg_scr[...] = _bdot(_rms(x, g_ref[...]), wm_ref[...])
    ya = ya_ref[...].T if att_transposed else ya_ref[...]
    s = (_sigmoid(mg_scr[:, 0:D_MODEL]) * _bdot(yc_ref[...], wb_ref[0])
         + _sigmoid(mg_scr[:, D_MODEL:2 * D_MODEL]) * _bdot(ya, wb_ref[1])
         + _sigmoid(mg_scr[:, 2 * D_MODEL:3 * D_MODEL]) * _bdot(yr_ref[...], wb_ref[2]))
    x = x + _bdot(s, wo_ref[...])
    x = x + _sigmoid(_bdot(x, pg_ref[...])) * _bdot(pe_ref[...], pp_ref[...])
    o_ref[...] = _rms(x, fg_ref[...]) if final_norm else x


def _merge_call(x, y_conv, y_att, y_rw, pe, lp, final_g, *, att_transposed, final_norm):
    b, t, _ = x.shape
    tm = min(TOK_TILE, t)
    tok = lambda i, j: (i, j, 0)
    att_spec = (pl.BlockSpec((None, BRANCH_W, tm), lambda i, j: (i, 0, j)) if att_transposed
                else pl.BlockSpec((None, tm, BRANCH_W), tok))
    return pl.pallas_call(
        functools.partial(_merge_kernel, att_transposed=att_transposed, final_norm=final_norm),
        grid=(b, t // tm),
        in_specs=[
            pl.BlockSpec((None, tm, D_MODEL), tok),
            pl.BlockSpec((None, tm, BRANCH_W), tok),
            att_spec,
            pl.BlockSpec((None, tm, BRANCH_W), tok),
            pl.BlockSpec((None, tm, P_DIM), tok),
            _full((1, D_MODEL)), _full((D_MODEL, N_MERGE_IN)), _full((3, BRANCH_W, D_MODEL)),
            _full((D_MODEL, D_MODEL)), _full((D_MODEL, D_MODEL)), _full((P_DIM, D_MODEL)),
            _full((1, D_MODEL)),
        ],
        out_specs=pl.BlockSpec((None, tm, D_MODEL), tok),
        out_shape=jax.ShapeDtypeStruct((b, t, D_MODEL), F32),
        scratch_shapes=[pltpu.VMEM((tm, N_MERGE_IN), F32)],
        compiler_params=_params("parallel", "parallel"),
        name="merge",
    )(x, y_conv, y_att, y_rw, pe, lp["norm_g"], lp["w_merge"], lp["w_branch"], lp["w_out"],
      lp["ple_gate"], lp["ple_proj"], final_g)


def _proj_kernel(x_ref, g_ref, w_ref, z_ref):
    z_ref[...] = _bdot(_rms(x_ref[...], g_ref[...]), w_ref[...])


def _proj_call(x, norm_g, w_in):
    m = x.shape[0]
    n_tiles = 3
    tn = N_IN // n_tiles
    return pl.pallas_call(
        _proj_kernel,
        grid=(n_tiles,),
        in_specs=[_full((m, D_MODEL)), _full((1, D_MODEL)),
                  pl.BlockSpec((D_MODEL, tn), lambda j: (0, j))],
        out_specs=pl.BlockSpec((m, tn), lambda j: (0, j)),
        out_shape=jax.ShapeDtypeStruct((m, N_IN), F32),
        compiler_params=_params("parallel"),
        name="decode_project",
    )(x, norm_g, w_in)


def _row_to_col(row_vec):
    n = row_vec.shape[1]
    eye = lax.broadcasted_iota(jnp.int32, (n, n), 0) == lax.broadcasted_iota(jnp.int32, (n, n), 1)
    return jnp.sum(jnp.where(eye, row_vec, 0.0), axis=1, keepdims=True)


def _col_to_row(col_vec):
    n = col_vec.shape[0]
    eye = lax.broadcasted_iota(jnp.int32, (n, n), 0) == lax.broadcasted_iota(jnp.int32, (n, n), 1)
    return jnp.sum(jnp.where(eye, col_vec, 0.0), axis=0, keepdims=True)


def _mix_kernel(z_ref, buf_ref, st_ref, sh0_ref, cw_ref, cb_ref, lg_ref, lb_ref, cos_ref, sa_ref,
                sb_ref, mu_ref, w0_ref, wlb_ref, a0_ref, alb_ref, kk_ref, ka_ref, rk_ref, gg_ref,
                gb_ref, bd_ref,
                yc_ref, cs_ref, q_ref, k_ref, v_ref, sg_ref, yr_ref, so_ref, sho_ref, ext_scr):
    o_att = N_CONV_IN
    o_rw = N_CONV_IN + N_ATT_IN
    u = z_ref[:, 0:BRANCH_W] * _sigmoid(z_ref[:, BRANCH_W:2 * BRANCH_W])
    ext_scr[0:8, :] = jnp.zeros((8, BRANCH_W), F32)
    ext_scr[2:HIST, :] = buf_ref[...]
    ext_scr[HIST:HIST + 16, :] = jnp.broadcast_to(u, (16, BRANCH_W))
    yc_ref[...] = _conv_branch(ext_scr, HIST, 8, cw_ref, cb_ref[...], lg_ref[...], lb_ref[...],
                               z_ref[:, 2 * BRANCH_W:3 * BRANCH_W])[0:1, :]
    cs_ref[...] = ext_scr[3:HIST + 1, :]
    n = BRANCH_W // LANE
    cos, sa, sb = (_tile_lanes(r[...], n) for r in (cos_ref, sa_ref, sb_ref))
    q_ref[...] = _rope(z_ref[:, o_att:o_att + BRANCH_W], cos, sa, sb) * (HEAD_DIM ** -0.5)
    k_ref[...] = _rope(z_ref[:, o_att + BRANCH_W:o_att + 2 * BRANCH_W], cos, sa, sb)
    v_ref[...] = z_ref[:, o_att + 2 * BRANCH_W:o_att + 3 * BRANCH_W]
    sg_ref[...] = _silu(z_ref[:, o_att + 3 * BRANCH_W:o_att + 4 * BRANCH_W])
    cur = z_ref[:, o_rw:o_rw + SHIFT_W]
    sho_ref[...] = cur
    mixed = cur + (sh0_ref[...] - cur) * mu_ref[...]
    r, k, v, log_decay, kk, a_lr = _rwkv_columns(
        mixed, w0_ref[...], wlb_ref, a0_ref[...], alb_ref, kk_ref[...], ka_ref[...], bd_ref)
    decay = jnp.exp(log_decay)
    b_vec = kk * a_lr
    ys = []
    for h in range(N_HEAD):
        lanes = slice(h * HEAD_DIM, (h + 1) * HEAD_DIM)
        s = st_ref[h]
        sa_col = jnp.sum(s * (-kk[:, lanes]), axis=1, keepdims=True)
        s = s * decay[:, lanes] + sa_col * b_vec[:, lanes] + _row_to_col(v[:, lanes]) * k[:, lanes]
        so_ref[h] = s
        ys.append(_col_to_row(jnp.sum(s * r[:, lanes], axis=1, keepdims=True)))
    y = jnp.concatenate(ys, axis=1)
    yr_ref[...] = _rwkv_finish(y, r, k, v, rk_ref[...], gg_ref[...], gb_ref[...],
                               z_ref[:, o_rw + SHIFT_W:o_rw + SHIFT_W + BRANCH_W], bd_ref)


def _mix_call(z, buf, state, shift0, lp, tables):
    b = z.shape[0]
    row = lambda w: pl.BlockSpec((None, 1, w), lambda i: (i, 0, 0))
    vec = _full((1, BRANCH_W))
    tab = _full((1, LANE))
    o = lambda w, dt=F32: jax.ShapeDtypeStruct((b, 1, w), dt)
    return pl.pallas_call(
        _mix_kernel,
        grid=(b,),
        in_specs=[
            row(N_IN),
            pl.BlockSpec((None, CONV_W - 1, BRANCH_W), lambda i: (i, 0, 0)),
            pl.BlockSpec((None, N_HEAD, HEAD_DIM, HEAD_DIM), lambda i: (i, 0, 0, 0)),
            row(SHIFT_W),
            _full((CONV_W, BRANCH_W)), vec, vec, vec, tab, tab, tab,
            _full((1, SHIFT_W)), vec, _full((2 * LORA, BRANCH_W)), vec, _full((2 * LORA, BRANCH_W)),
            vec, vec, vec, vec, vec, _full((BRANCH_W, BRANCH_W)),
        ],
        out_specs=[
            row(BRANCH_W),
            pl.BlockSpec((None, CONV_W - 1, BRANCH_W), lambda i: (i, 0, 0)),
            row(BRANCH_W), row(BRANCH_W), row(BRANCH_W), row(BRANCH_W), row(BRANCH_W),
            pl.BlockSpec((None, N_HEAD, HEAD_DIM, HEAD_DIM), lambda i: (i, 0, 0, 0)),
            row(SHIFT_W),
        ],
        out_shape=[
            o(BRANCH_W, BF16), jax.ShapeDtypeStruct((b, CONV_W - 1, BRANCH_W), F32),
            o(BRANCH_W), o(BRANCH_W), o(BRANCH_W), o(BRANCH_W), o(BRANCH_W, BF16),
            jax.ShapeDtypeStruct((b, N_HEAD, HEAD_DIM, HEAD_DIM), F32), o(SHIFT_W),
        ],
        scratch_shapes=[pltpu.VMEM((HIST + 16, BRANCH_W), F32)],
        compiler_params=_params("parallel"),
        name="decode_mix",
    )(z.reshape(b, 1, N_IN), buf, state, shift0.reshape(b, 1, SHIFT_W),
      lp["conv_w"], lp["conv_b"], lp["conv_ln_g"], lp["conv_ln_b"], *tables,
      lp["shift_mu"], lp["w0"], lp["w_lora_b"], lp["a0"], lp["a_lora_b"], lp["k_k"], lp["k_a"],
      lp["r_k"], lp["gn_g"], lp["gn_b"], lp["head_bd"])


def _head_rows(q_row):
    lane = lax.broadcasted_iota(jnp.int32, (N_HEAD, BRANCH_W), 1)
    row = lax.broadcasted_iota(jnp.int32, (N_HEAD, BRANCH_W), 0)
    return jnp.where((lane >= row * HEAD_DIM) & (lane < (row + 1) * HEAD_DIM), q_row, 0.0)


def _score_kernel(pt_ref, q_ref, *refs):
    del pt_ref
    pages = refs[:PAGES_PER_STEP]
    s_ref, gate_ref = refs[PAGES_PER_STEP:]
    j = pl.program_id(1)
    page_rows = pages[0].shape[1]

    @pl.when(j == 0)
    def _():
        gate_ref[...] = jnp.zeros(gate_ref.shape, F32)

    qh = _head_rows(q_ref[...])
    lane = lax.broadcasted_iota(jnp.int32, gate_ref.shape, 1)
    pages_per_block = MOBA_BLOCK // page_rows
    gate = gate_ref[...]
    for r in range(PAGES_PER_STEP):
        s = _bdot(qh, pages[r][...])
        s_ref[:, r * page_rows:(r + 1) * page_rows] = s
        blk = (j * PAGES_PER_STEP + r) // pages_per_block
        gate = gate + jnp.where(lane == blk, jnp.sum(s, axis=1, keepdims=True), 0.0)
    gate_ref[...] = gate


def _score_call(page_table, q, cache, layer, n_pool):
    b, n_pages = page_table.shape
    page_rows = cache.shape[2]
    past = n_pages * page_rows
    assert n_pages % PAGES_PER_STEP == 0 and MOBA_BLOCK % page_rows == 0
    assert past // MOBA_BLOCK <= LANE

    def page_spec(r):
        return pl.BlockSpec(
            (None, BRANCH_W, page_rows),
            lambda i, j, pt: (layer * n_pool + pt[i, j * PAGES_PER_STEP + r], 0, 0))

    grid_spec = pltpu.PrefetchScalarGridSpec(
        num_scalar_prefetch=1,
        grid=(b, n_pages // PAGES_PER_STEP),
        in_specs=[pl.BlockSpec((None, 1, BRANCH_W), lambda i, j, pt: (i, 0, 0))]
        + [page_spec(r) for r in range(PAGES_PER_STEP)],
        out_specs=[
            pl.BlockSpec((None, N_HEAD, PAGES_PER_STEP * page_rows), lambda i, j, pt: (i, 0, j)),
            pl.BlockSpec((None, N_HEAD, LANE), lambda i, j, pt: (i, 0, 0)),
        ],
    )
    return pl.pallas_call(
        _score_kernel,
        grid_spec=grid_spec,
        out_shape=[jax.ShapeDtypeStruct((b, N_HEAD, past), F32),
                   jax.ShapeDtypeStruct((b, N_HEAD, LANE), F32)],
        compiler_params=_params("parallel", "arbitrary"),
        name="decode_scores",
    )(page_table, q, *([cache] * PAGES_PER_STEP))


def _select_kernel(s_ref, gate_ref, q_ref, k_ref, p_ref, idx_ref, pown_ref, *, n_blocks):
    lane = lax.broadcasted_iota(jnp.int32, gate_ref.shape, 1).astype(F32)
    gate = jnp.where(lane < n_blocks, gate_ref[...], -jnp.inf)
    key_blk = jnp.right_shift(lax.broadcasted_iota(jnp.int32, s_ref.shape, 1),
                              MOBA_BLOCK.bit_length() - 1).astype(F32)
    sel = jnp.zeros(s_ref.shape, F32)
    idx_out = jnp.zeros(gate_ref.shape, F32)
    for j in range(MOBA_TOPK):
        m = jnp.max(gate, axis=1, keepdims=True)
        idx = jnp.min(jnp.where(gate == m, lane, float(LANE)), axis=1, keepdims=True)
        sel = jnp.where(key_blk == idx, 1.0, sel)
        idx_out = jnp.where(lane == j, idx, idx_out)
        gate = jnp.where(lane == idx, -jnp.inf, gate)
    s_own = jnp.sum(_head_rows(q_ref[...]) * k_ref[...], axis=1, keepdims=True)
    s = jnp.where(sel > 0.0, s_ref[...], NEG)
    m = jnp.maximum(jnp.max(s, axis=1, keepdims=True), s_own)
    e = jnp.exp(s - m)
    e_own = jnp.exp(s_own - m)
    l = jnp.sum(e, axis=1, keepdims=True) + e_own
    p_ref[...] = e / l
    pown_ref[...] = jnp.broadcast_to(e_own / l, pown_ref.shape)
    idx_ref[...] = idx_out.astype(jnp.int32)


def _select_call(scores, gate, q, k_new):
    b, _, past = scores.shape
    n_blocks = past // MOBA_BLOCK
    assert n_blocks >= MOBA_TOPK
    head = lambda w: pl.BlockSpec((None, N_HEAD, w), lambda i: (i, 0, 0))
    row = pl.BlockSpec((None, 1, BRANCH_W), lambda i: (i, 0, 0))
    return pl.pallas_call(
        functools.partial(_select_kernel, n_blocks=n_blocks),
        grid=(b,),
        in_specs=[head(past), head(LANE), row, row],
        out_specs=[head(past), head(LANE), head(LANE)],
        out_shape=[jax.ShapeDtypeStruct((b, N_HEAD, past), F32),
                   jax.ShapeDtypeStruct((b, N_HEAD, LANE), jnp.int32),
                   jax.ShapeDtypeStruct((b, N_HEAD, LANE), F32)],
        compiler_params=_params("parallel"),
        name="decode_select",
    )(scores, gate, q, k_new)


def _gather_kernel(pt_ref, ix_ref, pown_ref, vnew_ref, sg_ref, *refs, n_sel):
    del pt_ref, ix_ref
    p_rows = refs[:n_sel]
    v_pages = refs[n_sel:2 * n_sel]
    y_ref = refs[2 * n_sel]
    h = pl.program_id(1)

    @pl.when(h == 0)
    def _():
        y_ref[...] = jnp.zeros(y_ref.shape, F32)

    acc = pown_ref[pl.ds(h, 1), 0:1] * vnew_ref[...]
    for r in range(n_sel):
        p8 = jnp.broadcast_to(p_rows[r][...], (8, p_rows[r].shape[1]))
        acc = acc + _bdot_nt(p8, v_pages[r][...])[0:1, :]
    lane = lax.broadcasted_iota(jnp.int32, acc.shape, 1)
    mine = (lane >= h * HEAD_DIM) & (lane < (h + 1) * HEAD_DIM)
    y_ref[...] = y_ref[...] + jnp.where(mine, acc * sg_ref[...], 0.0)


def _gather_call(page_table, idx, probs, p_own, v_new, sg, cache, layer, n_pool):
    b, n_pages = page_table.shape
    page_rows = cache.shape[2]
    ppb = MOBA_BLOCK // page_rows
    n_sel = MOBA_TOPK * ppb
    probs = probs.reshape(b, N_HEAD, n_pages, 1, page_rows)

    def seq_page(i, h, ix, r):
        return ix[i, h * MOBA_TOPK + r // ppb] * ppb + r % ppb

    def p_spec(r):
        return pl.BlockSpec((None, None, None, 1, page_rows),
                            lambda i, h, pt, ix: (i, h, seq_page(i, h, ix, r), 0, 0))

    def v_spec(r):
        return pl.BlockSpec((None, BRANCH_W, page_rows),
                            lambda i, h, pt, ix: (layer * n_pool + pt[i, seq_page(i, h, ix, r)], 0, 0))

    row = pl.BlockSpec((None, 1, BRANCH_W), lambda i, h, pt, ix: (i, 0, 0))
    grid_spec = pltpu.PrefetchScalarGridSpec(
        num_scalar_prefetch=2,
        grid=(b, N_HEAD),
        in_specs=[pl.BlockSpec((None, N_HEAD, LANE), lambda i, h, pt, ix: (i, 0, 0)), row, row]
        + [p_spec(r) for r in range(n_sel)] + [v_spec(r) for r in range(n_sel)],
        out_specs=row,
    )
    return pl.pallas_call(
        functools.partial(_gather_kernel, n_sel=n_sel),
        grid_spec=grid_spec,
        out_shape=jax.ShapeDtypeStruct((b, 1, BRANCH_W), F32),
        compiler_params=_params("parallel", "arbitrary"),
        name="decode_gather",
    )(page_table, idx, p_own, v_new, sg, *([probs] * n_sel), *([cache] * n_sel))


def _layer_params(i, norm_g, w_in, conv_w, conv_b, conv_ln_g, conv_ln_b, shift_mu, w0, w_lora_b, a0,
                  a_lora_b, k_k, k_a, r_k, gn_g, gn_b, w_branch, w_out, ple_proj, ple_gate):
    w = w_in[i].astype(BF16)
    o1, o2, o3 = N_CONV_IN, N_CONV_IN + N_ATT_IN, N_CONV_IN + N_ATT_IN + N_RWKV_IN
    row = lambda a: a[i].reshape(1, -1)
    zeros = jnp.zeros((LORA, BRANCH_W), F32)
    return dict(
        norm_g=row(norm_g), w_in=w, w_conv=w[:, :o1], w_att=w[:, o1:o2], w_rwkv=w[:, o2:o3],
        w_merge=w[:, o3:],
        conv_w=conv_w[i], conv_b=row(conv_b), conv_ln_g=row(conv_ln_g), conv_ln_b=row(conv_ln_b),
        shift_mu=row(shift_mu), w0=row(w0), a0=row(a0),
        w_lora_b=jnp.concatenate([w_lora_b[i], zeros], axis=0).astype(BF16),
        a_lora_b=jnp.concatenate([zeros, a_lora_b[i]], axis=0).astype(BF16),
        k_k=row(k_k), k_a=row(k_a), r_k=row(r_k), gn_g=row(gn_g), gn_b=row(gn_b),
        w_branch=w_branch[i].astype(BF16), w_out=w_out[i].astype(BF16),
        ple_proj=ple_proj[i].astype(BF16), ple_gate=ple_gate[i].astype(BF16),
        head_bd=_head_block_diag(),
    )


def _prompt_layer(x, pe, lp, tables, final_g, final_norm):
    b, t, _ = x.shape
    nb = t // MOBA_BLOCK
    y_conv, conv_new = _conv_call(x, lp["norm_g"], lp["w_conv"], lp["conv_w"], lp["conv_b"],
                                  lp["conv_ln_g"], lp["conv_ln_b"])
    k, v, qt, kb, vt, sgt, km = _att1_call(x, lp["norm_g"], lp["w_att"], tables)
    km = km.reshape(b, nb, N_HEAD, HEAD_DIM).transpose(0, 2, 1, 3)
    y_att_t = _att2_call(qt, kb, vt, km, sgt)
    y_rw, wkv, shift = _rwkv_call(x, lp["norm_g"], lp["w_rwkv"], lp)
    x = _merge_call(x, y_conv, y_att_t, y_rw, pe, lp, final_g,
                    att_transposed=True, final_norm=final_norm)
    return x, k, v, conv_new, wkv, shift.reshape(b, SHIFT_W)


def _sample_layer(x, pe, lp, tables, final_g, final_norm, layer, n_pool, cache_k, cache_v,
                  page_table, buf, state, shift0):
    b = x.shape[0]
    z = _proj_call(x.reshape(b, D_MODEL), lp["norm_g"], lp["w_in"])
    y_conv, conv_new, q, k, v, sg, y_rw, wkv, shift = _mix_call(z, buf, state, shift0, lp, tables)
    scores, gate = _score_call(page_table, q, cache_k, layer, n_pool)
    probs, idx, p_own = _select_call(scores, gate, q, k)
    idx = idx[:, :, :MOBA_TOPK].reshape(b, N_HEAD * MOBA_TOPK)
    y_att = _gather_call(page_table, idx, probs, p_own, v, sg, cache_v, layer, n_pool)
    tok = lambda a: a.reshape(1, b, -1)
    x = _merge_call(tok(x), tok(y_conv), tok(y_att), tok(y_rw), tok(pe), lp, final_g,
                    att_transposed=False, final_norm=final_norm)
    return (x.reshape(b, 1, D_MODEL), k, v, conv_new, wkv, shift.reshape(b, SHIFT_W))


def kernel(x_prompt, x_sample, cache_k, cache_v, page_table, state_conv, state_wkv, state_shift,
           p_prompt, p_sample, norm_g, w_in, conv_w, conv_b, conv_ln_g, conv_ln_b, shift_mu, w0,
           w_lora_b, a0, a_lora_b, k_k, k_a, r_k, gn_g, gn_b, w_branch, w_out, ple_proj, ple_gate,
           final_norm_g):
    depth = w_in.shape[0]
    b_p, t_p, _ = x_prompt.shape
    b_s, t_s, _ = x_sample.shape
    assert t_s == 1 and t_p % TOK_TILE == 0 and TOK_TILE == MOBA_BLOCK
    n_pool, page_rows = cache_k.shape[1], cache_k.shape[2]
    past_len = page_table.shape[1] * page_rows
    assert past_len % MOBA_BLOCK == 0
    tables_p = _rope_tables(jnp.arange(t_p, dtype=jnp.int32))
    tables_s = _rope_tables(past_len + jnp.arange(1, dtype=jnp.int32))
    by_page = lambda c: c.transpose(0, 1, 3, 4, 2).reshape(depth * n_pool, BRANCH_W, page_rows)
    cache_k, cache_v = by_page(cache_k), by_page(cache_v)
    final_g = final_norm_g.reshape(1, D_MODEL)
    xp, xs = x_prompt, x_sample
    outs = [[] for _ in range(10)]
    for i in range(depth):
        lp = _layer_params(i, norm_g, w_in, conv_w, conv_b, conv_ln_g, conv_ln_b, shift_mu, w0,
                           w_lora_b, a0, a_lora_b, k_k, k_a, r_k, gn_g, gn_b, w_branch, w_out,
                           ple_proj, ple_gate)
        last = i == depth - 1
        xp, kp, vp, cp, wp, sp = _prompt_layer(xp, p_prompt[i], lp, tables_p, final_g, last)
        xs, ks, vs, cs, ws, ss = _sample_layer(
            xs, p_sample[i], lp, tables_s, final_g, last, i, n_pool, cache_k, cache_v, page_table,
            state_conv[i], state_wkv[i], state_shift[i])
        heads_p = lambda a: a.reshape(b_p, t_p, N_HEAD, HEAD_DIM)
        heads_s = lambda a: a.reshape(b_s, 1, N_HEAD, HEAD_DIM)
        for lst, a in zip(outs, (heads_p(kp), heads_p(vp), heads_s(ks), heads_s(vs),
                                 cp, cs, wp, ws, sp, ss)):
            lst.append(a)
    return (xp, xs) + tuple(jnp.stack(lst) for lst in outs)
```

```python
import functools

import jax
import jax.numpy as jnp
from jax import lax
from jax.experimental import pallas as pl
from jax.experimental.pallas import tpu as pltpu

D_MODEL = 1024
P_DIM = 256
HEAD_DIM = 64
BRANCH_W = 512
N_HEAD = 8
N_PAIR = 4
CONV_W = 31
ROT_DIM = 16
ROPE_THETA = 500000.0
MOBA_BLOCK = 256
MOBA_TOPK = 3
LORA = 64
SHIFT_W = 3 * BRANCH_W + 2 * LORA
N_CONV_IN = 3 * BRANCH_W
N_ATT_IN = 4 * BRANCH_W
N_RWKV_IN = SHIFT_W + BRANCH_W
N_MERGE_IN = 3 * D_MODEL
N_IN = N_CONV_IN + N_ATT_IN + N_RWKV_IN + N_MERGE_IN
NORM_EPS = 1e-6
LN_EPS = 1e-5
GN_EPS = 64e-5
NEG = -1e30
LOG2_E = 1.4426950408889634
DECAY_SCALE = 0.6065306597126334

LANE = 128
TOK_TILE = 256
V_ROWS = 80
BIAS_ROWS = 16
CHUNK = 64
HIST = 32
CONV_ROWS = 128
PAGES_PER_STEP = 32
VMEM_LIMIT = 56 * 1024 * 1024

F32 = jnp.float32
BF16 = jnp.bfloat16


def _bdot(a, b):
    return jnp.dot(a.astype(BF16), b.astype(BF16), preferred_element_type=F32)


def _bdot_nt(a, b):
    return lax.dot_general(a.astype(BF16), b.astype(BF16), (((1,), (1,)), ((), ())),
                           preferred_element_type=F32)


def _split_dot(x, m_bf16):
    hi = x.astype(BF16)
    lo = (x - hi.astype(F32)).astype(BF16)
    return (jnp.dot(hi, m_bf16, preferred_element_type=F32)
            + jnp.dot(lo, m_bf16, preferred_element_type=F32))


def _split3_dot_left(m_bf16, x):
    hi = x.astype(BF16)
    r1 = x - hi.astype(F32)
    mid = r1.astype(BF16)
    lo = (r1 - mid.astype(F32)).astype(BF16)
    return (jnp.dot(m_bf16, hi, preferred_element_type=F32)
            + jnp.dot(m_bf16, mid, preferred_element_type=F32)
            + jnp.dot(m_bf16, lo, preferred_element_type=F32))


def _sigmoid(x):
    return 1.0 / (1.0 + jnp.exp(-x))


def _silu(x):
    return x * _sigmoid(x)


def _rms(x, g):
    return x * lax.rsqrt(jnp.mean(x * x, axis=-1, keepdims=True) + NORM_EPS) * g


def _params(*sem):
    return pltpu.CompilerParams(dimension_semantics=sem, vmem_limit_bytes=VMEM_LIMIT)


def _full(shape):
    nd = len(shape)
    return pl.BlockSpec(shape, lambda *_: (0,) * nd)


def _conv_branch(ext_ref, base, rows, cw_ref, cb, lg, lb, g_conv):
    acc = jnp.zeros((rows, BRANCH_W), F32) + cb
    first = base - (CONV_W - 1)
    for s in range(8):
        z = None
        for j in range(CONV_W):
            if (first + j) % 8 == s:
                start = first + j - s
                term = ext_ref[start:start + rows + 8, :] * cw_ref[j:j + 1, :]
                z = term if z is None else z + term
        acc = acc + z[s:s + rows, :]
    mu = jnp.mean(acc, axis=-1, keepdims=True)
    d = acc - mu
    var = jnp.mean(d * d, axis=-1, keepdims=True)
    y = d * lax.rsqrt(var + LN_EPS) * lg + lb
    return (_silu(y) * _silu(g_conv)).astype(BF16)


def _conv_kernel(x_ref, g_ref, w_ref, cw_ref, cb_ref, lg_ref, lb_ref, y_ref, cs_ref, z_scr, ext_scr):
    tt = x_ref.shape[0]

    @pl.when(pl.program_id(1) == 0)
    def _():
        ext_scr[0:HIST, :] = jnp.zeros((HIST, BRANCH_W), F32)
        ext_scr[HIST + tt:HIST + tt + 8, :] = jnp.zeros((8, BRANCH_W), F32)

    z_scr[...] = _bdot(_rms(x_ref[...], g_ref[...]), w_ref[...])
    ext_scr[HIST:HIST + tt, :] = z_scr[:, 0:BRANCH_W] * _sigmoid(z_scr[:, BRANCH_W:2 * BRANCH_W])
    cb, lg, lb = cb_ref[...], lg_ref[...], lb_ref[...]
    for r in range(0, tt, CONV_ROWS):
        y_ref[r:r + CONV_ROWS, :] = _conv_branch(
            ext_scr, HIST + r, CONV_ROWS, cw_ref, cb, lg, lb,
            z_scr[r:r + CONV_ROWS, 2 * BRANCH_W:3 * BRANCH_W])
    cs_ref[...] = ext_scr[HIST + tt - (CONV_W - 1):HIST + tt, :]
    ext_scr[0:HIST, :] = ext_scr[tt:tt + HIST, :]


def _conv_call(x, norm_g, w_c, conv_w, conv_b, ln_g, ln_b):
    b, t, _ = x.shape
    tt = TOK_TILE
    return pl.pallas_call(
        _conv_kernel,
        grid=(b, t // tt),
        in_specs=[
            pl.BlockSpec((None, tt, D_MODEL), lambda i, j: (i, j, 0)),
            _full((1, D_MODEL)), _full((D_MODEL, N_CONV_IN)), _full((CONV_W, BRANCH_W)),
            _full((1, BRANCH_W)), _full((1, BRANCH_W)), _full((1, BRANCH_W)),
        ],
        out_specs=[
            pl.BlockSpec((None, tt, BRANCH_W), lambda i, j: (i, j, 0)),
            pl.BlockSpec((None, CONV_W - 1, BRANCH_W), lambda i, j: (i, 0, 0)),
        ],
        out_shape=[
            jax.ShapeDtypeStruct((b, t, BRANCH_W), BF16),
            jax.ShapeDtypeStruct((b, CONV_W - 1, BRANCH_W), F32),
        ],
        scratch_shapes=[pltpu.VMEM((tt, N_CONV_IN), F32),
                        pltpu.VMEM((HIST + tt + 8, BRANCH_W), F32)],
        compiler_params=_params("parallel", "arbitrary"),
        name="conv_branch",
    )(x, norm_g, w_c, conv_w, conv_b, ln_g, ln_b)


def _rope_tables(pos):
    half = ROT_DIM // 2
    inv = jnp.power(ROPE_THETA, -jnp.arange(half, dtype=F32) * (2.0 / ROT_DIM))
    ang = pos.astype(F32)[:, None] * inv[None, :]
    cos, sin = jnp.cos(ang), jnp.sin(ang)
    n = pos.shape[0]
    pad = jnp.zeros((n, HEAD_DIM - ROT_DIM), F32)
    cos64 = jnp.concatenate([cos, cos, pad + 1.0], axis=1)
    sa64 = jnp.concatenate([-sin, jnp.zeros_like(sin), pad], axis=1)
    sb64 = jnp.concatenate([jnp.zeros_like(sin), sin, pad], axis=1)
    return tuple(jnp.concatenate([a, a], axis=1) for a in (cos64, sa64, sb64))


def _rope(a, cos, sa, sb):
    w = a.shape[1]
    return a * cos + pltpu.roll(a, w - ROT_DIM // 2, 1) * sa + pltpu.roll(a, ROT_DIM // 2, 1) * sb


def _tile_lanes(a, n):
    return jnp.concatenate([a] * n, axis=1)


def _att1_kernel(x_ref, g_ref, w_ref, cos_ref, sa_ref, sb_ref, *refs):
    k_ref, v_ref, qt_ref, kb_ref, vt_ref, sgt_ref, km_ref, z_scr = refs[-8:]
    z_scr[...] = _bdot(_rms(x_ref[...], g_ref[...]), w_ref[...])
    n = BRANCH_W // LANE
    cos, sa, sb = (_tile_lanes(r[...], n) for r in (cos_ref, sa_ref, sb_ref))
    q = _rope(z_scr[:, 0:BRANCH_W], cos, sa, sb) * (LOG2_E * HEAD_DIM ** -0.5)
    k = _rope(z_scr[:, BRANCH_W:2 * BRANCH_W], cos, sa, sb)
    v = z_scr[:, 2 * BRANCH_W:3 * BRANCH_W]
    v_t = v.T
    k_ref[...] = k.T
    v_ref[...] = v_t
    qt_ref[...] = q.T.astype(BF16)
    lane = lax.broadcasted_iota(jnp.int32, (k.shape[0], LANE), 1)
    one_hot = jnp.where(lane == HEAD_DIM, 1.0, 0.0)
    for p in range(N_PAIR):
        slab = k[:, p * LANE:(p + 1) * LANE]
        kb_ref[2 * p] = jnp.where(lane < HEAD_DIM, slab, one_hot).astype(BF16)
        kb_ref[2 * p + 1] = jnp.where(lane < HEAD_DIM, pltpu.roll(slab, HEAD_DIM, 1),
                                      one_hot).astype(BF16)
    tail = jnp.where(lax.broadcasted_iota(jnp.int32, (V_ROWS - HEAD_DIM, v_t.shape[1]), 0) == 0,
                     1.0, 0.0)
    for h in range(N_HEAD):
        vt_ref[h * V_ROWS:(h + 1) * V_ROWS, :] = jnp.concatenate(
            [v_t[h * HEAD_DIM:(h + 1) * HEAD_DIM, :], tail], axis=0).astype(BF16)
    sgt_ref[...] = _silu(z_scr[:, 3 * BRANCH_W:4 * BRANCH_W]).T
    km_ref[...] = jnp.mean(k, axis=0, keepdims=True)


def _att1_call(x, norm_g, w_a, tables, layer, depth, kv_t):
    b, t, _ = x.shape
    tt = TOK_TILE
    nb = t // tt
    tok = lambda i, j: (i, j, 0)
    tr = lambda i, j: (i, 0, j)
    kv_spec = pl.BlockSpec((None, None, BRANCH_W, tt), lambda i, j: (layer, i, 0, j))
    kv_shape = jax.ShapeDtypeStruct((depth, b, BRANCH_W, t), F32)
    carried = [] if kv_t is None else [pl.BlockSpec(memory_space=pl.ANY)] * 2
    return pl.pallas_call(
        _att1_kernel,
        grid=(b, nb),
        in_specs=[
            pl.BlockSpec((None, tt, D_MODEL), tok),
            _full((1, D_MODEL)), _full((D_MODEL, N_ATT_IN)),
            pl.BlockSpec((tt, LANE), lambda i, j: (j, 0)),
            pl.BlockSpec((tt, LANE), lambda i, j: (j, 0)),
            pl.BlockSpec((tt, LANE), lambda i, j: (j, 0)),
        ] + carried,
        input_output_aliases={} if kv_t is None else {6: 0, 7: 1},
        out_specs=[
            kv_spec,
            kv_spec,
            pl.BlockSpec((None, BRANCH_W, tt), tr),
            pl.BlockSpec((None, N_HEAD, tt, LANE), lambda i, j: (i, 0, j, 0)),
            pl.BlockSpec((None, None, N_HEAD * V_ROWS, tt), lambda i, j: (i, j, 0, 0)),
            pl.BlockSpec((None, BRANCH_W, tt), tr),
            pl.BlockSpec((None, None, 1, BRANCH_W), lambda i, j: (i, j, 0, 0)),
        ],
        out_shape=[
            kv_shape,
            kv_shape,
            jax.ShapeDtypeStruct((b, BRANCH_W, t), BF16),
            jax.ShapeDtypeStruct((b, N_HEAD, t, LANE), BF16),
            jax.ShapeDtypeStruct((b, nb, N_HEAD * V_ROWS, tt), BF16),
            jax.ShapeDtypeStruct((b, BRANCH_W, t), F32),
            jax.ShapeDtypeStruct((b, nb, 1, BRANCH_W), F32),
        ],
        scratch_shapes=[pltpu.VMEM((tt, N_ATT_IN), F32)],
        compiler_params=_params("parallel", "parallel"),
        name="att_project",
    )(x, norm_g, w_a, *tables, *(kv_t or ()))


def _top_blocks(gate, n_slots):
    nb = gate.shape[0]
    row = lax.broadcasted_iota(jnp.int32, gate.shape, 0).astype(F32)
    sel = jnp.zeros(gate.shape, F32)
    for j in range(MOBA_TOPK):
        m = jnp.max(gate, axis=0, keepdims=True)
        idx = jnp.min(jnp.where(gate == m, row, float(nb)), axis=0, keepdims=True)
        hit = row == jnp.where(j < n_slots, idx, -1.0)
        sel = jnp.where(hit, 1.0, sel)
        gate = jnp.where(row == idx, -jnp.inf, gate)
    return sel


def _att2_kernel(qt_ref, kb_ref, vt_ref, km_ref, sgt_ref, yt_ref,
                 qp_scr, bias_scr, s_scr, e_scr, alpha_scr, m_scr, acc_scr):
    i = pl.program_id(1)
    tq = qt_ref.shape[1]
    nb = km_ref.shape[1]
    blk = lax.broadcasted_iota(jnp.int32, (nb, tq), 0)
    qp_scr[...] = jnp.zeros(qp_scr.shape, BF16)
    for h in range(N_HEAD):
        q_h = qt_ref[h * HEAD_DIM:(h + 1) * HEAD_DIM, :]
        qp_scr[h, 0:HEAD_DIM, :] = q_h
        gate = jnp.dot(km_ref[h].astype(BF16), q_h, preferred_element_type=F32)
        sel = _top_blocks(jnp.where(blk < i, gate, -jnp.inf), i)
        bias_scr[h] = jnp.where(sel > 0.0, 0.0, NEG)
    m_scr[...] = jnp.full(m_scr.shape, NEG, F32)
    acc_scr[...] = jnp.zeros(acc_scr.shape, F32)
    bias_row0 = lax.broadcasted_iota(jnp.int32, (BIAS_ROWS, tq), 0) == 0

    def scores(n, slot, bias_of, extra=None):
        start = pl.multiple_of(n * MOBA_BLOCK, MOBA_BLOCK)
        for h in range(N_HEAD):
            qp_scr[h, HEAD_DIM:HEAD_DIM + BIAS_ROWS, :] = jnp.where(
                bias_row0, bias_of(h), 0.0).astype(BF16)
            s = jnp.dot(kb_ref[h, pl.ds(start, MOBA_BLOCK), :], qp_scr[h],
                        preferred_element_type=F32)
            s_scr[slot, h] = s if extra is None else s + extra

    def softmax(slot):
        for h in range(N_HEAD):
            m = m_scr[h:h + 1, :]
            m_new = jnp.maximum(m, jnp.max(s_scr[slot, h], axis=0, keepdims=True))
            m_scr[h:h + 1, :] = m_new
            alpha_scr[slot, h:h + 1, :] = jnp.exp2(m - m_new)
            e_scr[slot, h] = jnp.exp2(s_scr[slot, h] - m_new).astype(BF16)

    def values(n, slot):
        for h in range(N_HEAD):
            pv = jnp.dot(vt_ref[n, h * V_ROWS:(h + 1) * V_ROWS, :], e_scr[slot, h],
                         preferred_element_type=F32)
            acc_scr[h] = alpha_scr[slot, h:h + 1, :] * acc_scr[h] + pv

    def row_bias(n):
        return lambda h: bias_scr[h, pl.ds(n, 1), :]

    def block_of(v):
        return jnp.where(v == 0, i, jnp.minimum(v - 1, nb - 1))

    def step(t, slot):
        nxt = block_of(t + 1)
        scores(nxt, 1 - slot, row_bias(nxt))
        values(block_of(t - 1), 1 - slot)
        softmax(slot)

    causal = jnp.where(lax.broadcasted_iota(jnp.int32, (MOBA_BLOCK, tq), 0)
                       <= lax.broadcasted_iota(jnp.int32, (MOBA_BLOCK, tq), 1), 0.0, NEG)
    scores(i, 0, lambda h: jnp.zeros((1, tq), F32), extra=causal)
    softmax(0)
    scores(0, 1, row_bias(0))

    def body(j, carry):
        step(2 * j + 1, 1)
        step(2 * j + 2, 0)
        return carry

    trips = (i + 1) // 2
    lax.fori_loop(0, trips, body, 0)
    values(block_of(2 * trips), 0)
    for h in range(N_HEAD):
        rows = slice(h * HEAD_DIM, (h + 1) * HEAD_DIM)
        denom = acc_scr[h, HEAD_DIM:HEAD_DIM + 1, :]
        yt_ref[rows, :] = acc_scr[h, 0:HEAD_DIM, :] / denom * sgt_ref[rows, :]


def _att2_call(qt, kb, vt, km, sgt):
    b, _, t = qt.shape
    nb = vt.shape[1]
    tq = MOBA_BLOCK
    return pl.pallas_call(
        _att2_kernel,
        grid=(b, t // tq),
        in_specs=[
            pl.BlockSpec((None, BRANCH_W, tq), lambda bi, i: (bi, 0, i)),
            pl.BlockSpec((None, N_HEAD, t, LANE), lambda bi, i: (bi, 0, 0, 0),
                         pipeline_mode=pl.Buffered(1)),
            pl.BlockSpec((None, nb, N_HEAD * V_ROWS, MOBA_BLOCK), lambda bi, i: (bi, 0, 0, 0),
                         pipeline_mode=pl.Buffered(1)),
            pl.BlockSpec((None, N_HEAD, nb, HEAD_DIM), lambda bi, i: (bi, 0, 0, 0)),
            pl.BlockSpec((None, BRANCH_W, tq), lambda bi, i: (bi, 0, i)),
        ],
        out_specs=pl.BlockSpec((None, BRANCH_W, tq), lambda bi, i: (bi, 0, i)),
        out_shape=jax.ShapeDtypeStruct((b, BRANCH_W, t), F32),
        scratch_shapes=[
            pltpu.VMEM((N_HEAD, LANE, tq), BF16), pltpu.VMEM((N_HEAD, nb, tq), F32),
            pltpu.VMEM((2, N_HEAD, MOBA_BLOCK, tq), F32), pltpu.VMEM((2, N_HEAD, MOBA_BLOCK, tq), BF16),
            pltpu.VMEM((2, N_HEAD, tq), F32), pltpu.VMEM((N_HEAD, tq), F32),
            pltpu.VMEM((N_HEAD, V_ROWS, tq), F32),
        ],
        compiler_params=_params("parallel", "arbitrary"),
        name="moba_attention",
    )(qt, kb, vt, km, sgt)


def _rwkv_columns(mixed, w0, wlb_ref, a0, alb_ref, kk_scale, k_a, bd_ref):
    r = mixed[:, 0:BRANCH_W]
    k = mixed[:, BRANCH_W:2 * BRANCH_W]
    v = mixed[:, 2 * BRANCH_W:3 * BRANCH_W]
    lora = mixed[:, 3 * BRANCH_W:3 * BRANCH_W + 2 * LORA]
    lane = lax.broadcasted_iota(jnp.int32, lora.shape, 1)
    lora = jnp.where(lane < LORA, jnp.tanh(lora), lora)
    log_decay = -(DECAY_SCALE * _sigmoid(w0 + _bdot(lora, wlb_ref[...])))
    a_lr = _sigmoid(a0 + _bdot(lora, alb_ref[...]))
    kk = k * kk_scale
    kk = kk * lax.rsqrt(jnp.maximum(_bdot(kk * kk, bd_ref[...]), 1e-24))
    k = k * (1.0 + (a_lr - 1.0) * k_a)
    return r, k, v, log_decay, kk, a_lr


def _rwkv_finish(y, r, k, v, r_k, gn_g, gn_b, g_rwkv, bd_ref):
    m = _split_dot(y, bd_ref[...]) * (1.0 / HEAD_DIM)
    d = y - m
    var = _bdot(d * d, bd_ref[...]) * (1.0 / HEAD_DIM)
    yn = d * lax.rsqrt(var + GN_EPS) * gn_g + gn_b
    bonus = _bdot(r * k * r_k, bd_ref[...]) * v
    return ((yn + bonus) * _silu(g_rwkv)).astype(BF16)


def _stack_heads(a, mask_a):
    return jnp.concatenate([jnp.where(mask_a, a, 0.0), jnp.where(mask_a, 0.0, a)], axis=0)


def _rwkv_kernel(x_ref, g_ref, w_ref, mu_ref, w0_ref, wlb_ref, a0_ref, alb_ref, kk_ref, ka_ref,
                 rk_ref, gg_ref, gb_ref, bd_ref,
                 y_ref, st_ref, sh_ref,
                 z_scr, shs_scr, r_scr, k_scr, v_scr, ld_scr, a_scr, b_scr, y_scr, s_scr,
                 cum_scr, ar_scr, bk_scr, at_scr, vs_scr, vst_scr, bend_scr, kend_scr, pw_scr, akrk_scr,
                 arb_scr, wu_scr, wut_scr, g_scr, q_scr, x_scr, yv_scr, kv_scr, h_scr, y0_scr, pt_scr):
    tt = x_ref.shape[0]

    @pl.when(pl.program_id(1) == 0)
    def _():
        shs_scr[0:8, :] = jnp.zeros((8, SHIFT_W), F32)
        s_scr[...] = jnp.zeros(s_scr.shape, F32)

    z_scr[...] = _bdot(_rms(x_ref[...], g_ref[...]), w_ref[...])
    shs_scr[8:8 + tt, :] = z_scr[:, 0:SHIFT_W]
    cur = z_scr[:, 0:SHIFT_W]
    mixed = cur + (shs_scr[7:7 + tt, :] - cur) * mu_ref[...]
    sh_ref[...] = shs_scr[7 + tt:8 + tt, :]
    shs_scr[7:8, :] = shs_scr[7 + tt:8 + tt, :]

    r, k, v, log_decay, kk, a_lr = _rwkv_columns(
        mixed, w0_ref[...], wlb_ref, a0_ref[...], alb_ref, kk_ref[...], ka_ref[...], bd_ref)
    r_scr[...] = r
    k_scr[...] = k
    v_scr[...] = v
    ld_scr[...] = log_decay
    a_scr[...] = -kk
    b_scr[...] = kk * a_lr

    n2 = 2 * CHUNK
    row = lax.broadcasted_iota(jnp.int32, (n2, n2), 0)
    col = lax.broadcasted_iota(jnp.int32, (n2, n2), 1)
    same = (row >= CHUNK) == (col >= CHUNK)
    rt = jnp.where(row >= CHUNK, row - CHUNK, row)
    ct = jnp.where(col >= CHUNK, col - CHUNK, col)
    strict = same & (rt > ct)
    incl = same & (rt >= ct)
    crow = lax.broadcasted_iota(jnp.int32, (CHUNK, CHUNK), 0)
    ccol = lax.broadcasted_iota(jnp.int32, (CHUNK, CHUNK), 1)
    tri = jnp.where(crow >= ccol, 1.0, 0.0).astype(BF16)
    mask_a = lax.broadcasted_iota(jnp.int32, (CHUNK, LANE), 1) < HEAD_DIM

    n_chunk = tt // CHUNK
    units = [(c, p) for c in range(n_chunk) for p in range(N_PAIR)]
    n_unit = range(len(units))
    dot = functools.partial(jnp.dot, preferred_element_type=F32)

    def nt(a, b):
        return lax.dot_general(a, b, (((1,), (1,)), ((), ())), preferred_element_type=F32)

    def blk(ref, c, p):
        return ref[c * CHUNK:(c + 1) * CHUNK, p * LANE:(p + 1) * LANE]

    for c in range(n_chunk):
        rows = slice(c * CHUNK, (c + 1) * CHUNK)
        cum_scr[rows, :] = _split3_dot_left(tri, ld_scr[rows, :])
    for u, (c, p) in enumerate(units):
        cum = blk(cum_scr, c, p)
        last = cum[CHUNK - 1:CHUNK, :]
        p_inv = jnp.exp(-cum)
        p_end = jnp.exp(last - cum)
        bb = blk(b_scr, c, p)
        kc = blk(k_scr, c, p)
        v_s = _stack_heads(blk(v_scr, c, p), mask_a)
        a_s = _stack_heads(blk(a_scr, c, p) * jnp.exp(cum - blk(ld_scr, c, p)),
                           mask_a).astype(BF16)
        ar_scr[u, 0:n2, :] = a_s
        ar_scr[u, n2:2 * n2, :] = _stack_heads(blk(r_scr, c, p) * jnp.exp(cum), mask_a).astype(BF16)
        bk_scr[u, 0:n2, :] = _stack_heads(bb * p_inv, mask_a).astype(BF16)
        bk_scr[u, n2:2 * n2, :] = _stack_heads(kc * p_inv, mask_a).astype(BF16)
        at_scr[u, :, 0:n2] = a_s
        vs_scr[u] = v_s.astype(BF16)
        vst_scr[u] = v_s.T.astype(BF16)
        bend_scr[u] = _stack_heads(bb * p_end, mask_a).astype(BF16)
        kend_scr[u] = _stack_heads(kc * p_end, mask_a).astype(BF16)
        pt_scr[u] = jnp.broadcast_to(jnp.exp(last), (8, LANE))
    eye = jnp.where(row == col, 1.0, 0.0)
    for u in n_unit:
        quad = nt(ar_scr[u], bk_scr[u])
        a_ab = jnp.where(strict, quad[0:n2, 0:n2], 0.0)
        x_scr[u] = eye + a_ab
        pw_scr[u] = a_ab.astype(BF16)
        akrk_scr[u, 0:n2, :] = jnp.where(strict, quad[0:n2, n2:2 * n2], 0.0).astype(BF16)
        akrk_scr[u, n2:2 * n2, :] = jnp.where(incl, quad[n2:2 * n2, n2:2 * n2], 0.0).astype(BF16)
        arb_scr[u] = jnp.where(incl, quad[n2:2 * n2, 0:n2], 0.0).astype(BF16)
    for _ in range(CHUNK.bit_length() - 2):
        for u in n_unit:
            pw_scr[u] = dot(pw_scr[u], pw_scr[u]).astype(BF16)
        for u in n_unit:
            x_scr[u] = x_scr[u] + dot(x_scr[u].astype(BF16), pw_scr[u])
    for u in n_unit:
        ty = dot(akrk_scr[u], vs_scr[u])
        at_scr[u, :, n2:2 * n2] = ty[0:n2, :].astype(BF16)
        yv_scr[u] = ty[n2:2 * n2, :]
    for u in n_unit:
        kv_scr[u] = dot(vst_scr[u], kend_scr[u])
    for u in n_unit:
        wu = dot(x_scr[u].astype(BF16), at_scr[u])
        wu_scr[u] = wu.astype(BF16)
        wut_scr[u] = wu.T.astype(BF16)
    for u in n_unit:
        gh = dot(wut_scr[u], bend_scr[u])
        g_scr[u] = gh[0:n2, :].astype(BF16)
        h_scr[u] = gh[n2:2 * n2, :] + kv_scr[u]
    for u in n_unit:
        qy = dot(arb_scr[u], wu_scr[u])
        q_scr[u] = (ar_scr[u, n2:2 * n2, :].astype(F32) + qy[:, 0:n2]).astype(BF16)
        y0_scr[u] = qy[:, n2:2 * n2] + yv_scr[u]
    for c in range(n_chunk):
        rows = slice(c * CHUNK, (c + 1) * CHUNK)
        us = [c * N_PAIR + p for p in range(N_PAIR)]
        s0 = [s_scr[p] for p in range(N_PAIR)]
        s0b = [a.astype(BF16) for a in s0]
        for p, u in enumerate(us):
            s_scr[p] = s0[p] * pt_scr[u, 0:1, :] + dot(s0b[p], g_scr[u]) + h_scr[u]
        for p, u in enumerate(us):
            y_s = nt(q_scr[u], s0b[p]) + y0_scr[u]
            y_scr[rows, p * LANE:(p + 1) * LANE] = y_s[0:CHUNK, :] + y_s[CHUNK:n2, :]

    y_ref[...] = _rwkv_finish(y_scr[...], r_scr[...], k_scr[...], v_scr[...], rk_ref[...],
                              gg_ref[...], gb_ref[...], z_scr[:, SHIFT_W:SHIFT_W + BRANCH_W], bd_ref)
    for p in range(N_PAIR):
        st_ref[2 * p] = s_scr[p, 0:HEAD_DIM, 0:HEAD_DIM]
        st_ref[2 * p + 1] = s_scr[p, HEAD_DIM:LANE, HEAD_DIM:LANE]


def _head_block_diag():
    i = jnp.arange(BRANCH_W) // HEAD_DIM
    return (i[:, None] == i[None, :]).astype(BF16)


def _rwkv_call(x, norm_g, w_r, lp):
    b, t, _ = x.shape
    tt = TOK_TILE
    vec = _full((1, BRANCH_W))
    scr = lambda w: pltpu.VMEM((tt, w), F32)
    units = (tt // CHUNK) * N_PAIR
    unit = lambda dt, r=1, c=1: pltpu.VMEM((units, r * LANE, c * LANE), dt)
    return pl.pallas_call(
        _rwkv_kernel,
        grid=(b, t // tt),
        in_specs=[
            pl.BlockSpec((None, tt, D_MODEL), lambda i, j: (i, j, 0)),
            _full((1, D_MODEL)), _full((D_MODEL, N_RWKV_IN)), _full((1, SHIFT_W)),
            vec, _full((2 * LORA, BRANCH_W)), vec, _full((2 * LORA, BRANCH_W)), vec, vec,
            vec, vec, vec, _full((BRANCH_W, BRANCH_W)),
        ],
        out_specs=[
            pl.BlockSpec((None, tt, BRANCH_W), lambda i, j: (i, j, 0)),
            pl.BlockSpec((None, N_HEAD, HEAD_DIM, HEAD_DIM), lambda i, j: (i, 0, 0, 0)),
            pl.BlockSpec((None, 1, SHIFT_W), lambda i, j: (i, 0, 0)),
        ],
        out_shape=[
            jax.ShapeDtypeStruct((b, t, BRANCH_W), BF16),
            jax.ShapeDtypeStruct((b, N_HEAD, HEAD_DIM, HEAD_DIM), F32),
            jax.ShapeDtypeStruct((b, 1, SHIFT_W), F32),
        ],
        scratch_shapes=[
            scr(N_RWKV_IN), pltpu.VMEM((8 + tt, SHIFT_W), F32),
            scr(BRANCH_W), scr(BRANCH_W), scr(BRANCH_W), scr(BRANCH_W), scr(BRANCH_W), scr(BRANCH_W),
            scr(BRANCH_W), pltpu.VMEM((N_PAIR, LANE, LANE), F32),
            scr(BRANCH_W), unit(BF16, 2), unit(BF16, 2), unit(BF16, 1, 2), *([unit(BF16)] * 5),
            unit(BF16, 2), unit(BF16), unit(BF16, 1, 2), unit(BF16, 2), unit(BF16), unit(BF16),
            *([unit(F32)] * 5), pltpu.VMEM((units, 8, LANE), F32),
        ],
        compiler_params=_params("parallel", "arbitrary"),
        name="rwkv_branch",
    )(x, norm_g, w_r, lp["shift_mu"], lp["w0"], lp["w_lora_b"], lp["a0"], lp["a_lora_b"],
      lp["k_k"], lp["k_a"], lp["r_k"], lp["gn_g"], lp["gn_b"], lp["head_bd"])


def _merge_kernel(x_ref, yc_ref, ya_ref, yr_ref, pe_ref, g_ref, wm_ref, wb_ref, wo_ref, pg_ref,
                  pp_ref, fg_ref, o_ref, mg_scr, *, att_transposed, final_norm):
    x = x_ref[...]
    mg_scr[...] = _bdot(_rms(x, g_ref[...]), wm_ref[...])
    ya = ya_ref[...].T if att_transposed else ya_ref[...]
    s = (_sigmoid(mg_scr[:, 0:D_MODEL]) * _bdot(yc_ref[...], wb_ref[0])
         + _sigmoid(mg_scr[:, D_MODEL:2 * D_MODEL]) * _bdot(ya, wb_ref[1])
         + _sigmoid(mg_scr[:, 2 * D_MODEL:3 * D_MODEL]) * _bdot(yr_ref[...], wb_ref[2]))
    x = x + _bdot(s, wo_ref[...])
    x = x + _sigmoid(_bdot(x, pg_ref[...])) * _bdot(pe_ref[...], pp_ref[...])
    o_ref[...] = _rms(x, fg_ref[...]) if final_norm else x


def _merge_call(x, y_conv, y_att, y_rw, pe, lp, final_g, *, att_transposed, final_norm):
    b, t, _ = x.shape
    tm = min(TOK_TILE, t)
    tok = lambda i, j: (i, j, 0)
    att_spec = (pl.BlockSpec((None, BRANCH_W, tm), lambda i, j: (i, 0, j)) if att_transposed
                else pl.BlockSpec((None, tm, BRANCH_W), tok))
    return pl.pallas_call(
        functools.partial(_merge_kernel, att_transposed=att_transposed, final_norm=final_norm),
        grid=(b, t // tm),
        in_specs=[
            pl.BlockSpec((None, tm, D_MODEL), tok),
            pl.BlockSpec((None, tm, BRANCH_W), tok),
            att_spec,
            pl.BlockSpec((None, tm, BRANCH_W), tok),
            pl.BlockSpec((None, tm, P_DIM), tok),
            _full((1, D_MODEL)), _full((D_MODEL, N_MERGE_IN)), _full((3, BRANCH_W, D_MODEL)),
            _full((D_MODEL, D_MODEL)), _full((D_MODEL, D_MODEL)), _full((P_DIM, D_MODEL)),
            _full((1, D_MODEL)),
        ],
        out_specs=pl.BlockSpec((None, tm, D_MODEL), tok),
        out_shape=jax.ShapeDtypeStruct((b, t, D_MODEL), F32),
        scratch_shapes=[pltpu.VMEM((tm, N_MERGE_IN), F32)],
        compiler_params=_params("parallel", "parallel"),
        name="merge",
    )(x, y_conv, y_att, y_rw, pe, lp["norm_g"], lp["w_merge"], lp["w_branch"], lp["w_out"],
      lp["ple_gate"], lp["ple_proj"], final_g)


def _proj_kernel(x_ref, g_ref, w_ref, z_ref):
    z_ref[...] = _bdot(_rms(x_ref[...], g_ref[...]), w_ref[...])


def _proj_call(x, norm_g, w_in):
    m = x.shape[0]
    n_tiles = 3
    tn = N_IN // n_tiles
    return pl.pallas_call(
        _proj_kernel,
        grid=(n_tiles,),
        in_specs=[_full((m, D_MODEL)), _full((1, D_MODEL)),
                  pl.BlockSpec((D_MODEL, tn), lambda j: (0, j))],
        out_specs=pl.BlockSpec((m, tn), lambda j: (0, j)),
        out_shape=jax.ShapeDtypeStruct((m, N_IN), F32),
        compiler_params=_params("parallel"),
        name="decode_project",
    )(x, norm_g, w_in)


def _row_to_col(row_vec):
    n = row_vec.shape[1]
    eye = lax.broadcasted_iota(jnp.int32, (n, n), 0) == lax.broadcasted_iota(jnp.int32, (n, n), 1)
    return jnp.sum(jnp.where(eye, row_vec, 0.0), axis=1, keepdims=True)


def _col_to_row(col_vec):
    n = col_vec.shape[0]
    eye = lax.broadcasted_iota(jnp.int32, (n, n), 0) == lax.broadcasted_iota(jnp.int32, (n, n), 1)
    return jnp.sum(jnp.where(eye, col_vec, 0.0), axis=0, keepdims=True)


def _mix_kernel(z_ref, buf_ref, st_ref, sh0_ref, cw_ref, cb_ref, lg_ref, lb_ref, cos_ref, sa_ref,
                sb_ref, mu_ref, w0_ref, wlb_ref, a0_ref, alb_ref, kk_ref, ka_ref, rk_ref, gg_ref,
                gb_ref, bd_ref,
                yc_ref, cs_ref, q_ref, k_ref, v_ref, sg_ref, yr_ref, so_ref, sho_ref, ext_scr):
    o_att = N_CONV_IN
    o_rw = N_CONV_IN + N_ATT_IN
    u = z_ref[:, 0:BRANCH_W] * _sigmoid(z_ref[:, BRANCH_W:2 * BRANCH_W])
    ext_scr[0:8, :] = jnp.zeros((8, BRANCH_W), F32)
    ext_scr[2:HIST, :] = buf_ref[...]
    ext_scr[HIST:HIST + 16, :] = jnp.broadcast_to(u, (16, BRANCH_W))
    yc_ref[...] = _conv_branch(ext_scr, HIST, 8, cw_ref, cb_ref[...], lg_ref[...], lb_ref[...],
                               z_ref[:, 2 * BRANCH_W:3 * BRANCH_W])[0:1, :]
    cs_ref[...] = ext_scr[3:HIST + 1, :]
    n = BRANCH_W // LANE
    cos, sa, sb = (_tile_lanes(r[...], n) for r in (cos_ref, sa_ref, sb_ref))
    q_ref[...] = _rope(z_ref[:, o_att:o_att + BRANCH_W], cos, sa, sb) * (HEAD_DIM ** -0.5)
    k_ref[...] = _rope(z_ref[:, o_att + BRANCH_W:o_att + 2 * BRANCH_W], cos, sa, sb)
    v_ref[...] = z_ref[:, o_att + 2 * BRANCH_W:o_att + 3 * BRANCH_W]
    sg_ref[...] = _silu(z_ref[:, o_att + 3 * BRANCH_W:o_att + 4 * BRANCH_W])
    cur = z_ref[:, o_rw:o_rw + SHIFT_W]
    sho_ref[...] = cur
    mixed = cur + (sh0_ref[...] - cur) * mu_ref[...]
    r, k, v, log_decay, kk, a_lr = _rwkv_columns(
        mixed, w0_ref[...], wlb_ref, a0_ref[...], alb_ref, kk_ref[...], ka_ref[...], bd_ref)
    decay = jnp.exp(log_decay)
    b_vec = kk * a_lr
    ys = []
    for h in range(N_HEAD):
        lanes = slice(h * HEAD_DIM, (h + 1) * HEAD_DIM)
        s = st_ref[h]
        sa_col = jnp.sum(s * (-kk[:, lanes]), axis=1, keepdims=True)
        s = s * decay[:, lanes] + sa_col * b_vec[:, lanes] + _row_to_col(v[:, lanes]) * k[:, lanes]
        so_ref[h] = s
        ys.append(_col_to_row(jnp.sum(s * r[:, lanes], axis=1, keepdims=True)))
    y = jnp.concatenate(ys, axis=1)
    yr_ref[...] = _rwkv_finish(y, r, k, v, rk_ref[...], gg_ref[...], gb_ref[...],
                               z_ref[:, o_rw + SHIFT_W:o_rw + SHIFT_W + BRANCH_W], bd_ref)


def _mix_call(z, buf, state, shift0, lp, tables):
    b = z.shape[0]
    row = lambda w: pl.BlockSpec((None, 1, w), lambda i: (i, 0, 0))
    vec = _full((1, BRANCH_W))
    tab = _full((1, LANE))
    o = lambda w, dt=F32: jax.ShapeDtypeStruct((b, 1, w), dt)
    return pl.pallas_call(
        _mix_kernel,
        grid=(b,),
        in_specs=[
            row(N_IN),
            pl.BlockSpec((None, CONV_W - 1, BRANCH_W), lambda i: (i, 0, 0)),
            pl.BlockSpec((None, N_HEAD, HEAD_DIM, HEAD_DIM), lambda i: (i, 0, 0, 0)),
            row(SHIFT_W),
            _full((CONV_W, BRANCH_W)), vec, vec, vec, tab, tab, tab,
            _full((1, SHIFT_W)), vec, _full((2 * LORA, BRANCH_W)), vec, _full((2 * LORA, BRANCH_W)),
            vec, vec, vec, vec, vec, _full((BRANCH_W, BRANCH_W)),
        ],
        out_specs=[
            row(BRANCH_W),
            pl.BlockSpec((None, CONV_W - 1, BRANCH_W), lambda i: (i, 0, 0)),
            row(BRANCH_W), row(BRANCH_W), row(BRANCH_W), row(BRANCH_W), row(BRANCH_W),
            pl.BlockSpec((None, N_HEAD, HEAD_DIM, HEAD_DIM), lambda i: (i, 0, 0, 0)),
            row(SHIFT_W),
        ],
        out_shape=[
            o(BRANCH_W, BF16), jax.ShapeDtypeStruct((b, CONV_W - 1, BRANCH_W), F32),
            o(BRANCH_W), o(BRANCH_W), o(BRANCH_W), o(BRANCH_W), o(BRANCH_W, BF16),
            jax.ShapeDtypeStruct((b, N_HEAD, HEAD_DIM, HEAD_DIM), F32), o(SHIFT_W),
        ],
        scratch_shapes=[pltpu.VMEM((HIST + 16, BRANCH_W), F32)],
        compiler_params=_params("parallel"),
        name="decode_mix",
    )(z.reshape(b, 1, N_IN), buf, state, shift0.reshape(b, 1, SHIFT_W),
      lp["conv_w"], lp["conv_b"], lp["conv_ln_g"], lp["conv_ln_b"], *tables,
      lp["shift_mu"], lp["w0"], lp["w_lora_b"], lp["a0"], lp["a_lora_b"], lp["k_k"], lp["k_a"],
      lp["r_k"], lp["gn_g"], lp["gn_b"], lp["head_bd"])


def _head_rows(q_row):
    lane = lax.broadcasted_iota(jnp.int32, (N_HEAD, BRANCH_W), 1)
    row = lax.broadcasted_iota(jnp.int32, (N_HEAD, BRANCH_W), 0)
    return jnp.where((lane >= row * HEAD_DIM) & (lane < (row + 1) * HEAD_DIM), q_row, 0.0)


def _score_kernel(pt_ref, q_ref, *refs):
    del pt_ref
    pages = refs[:PAGES_PER_STEP]
    s_ref, gate_ref = refs[PAGES_PER_STEP:]
    j = pl.program_id(1)
    page_rows = pages[0].shape[1]

    @pl.when(j == 0)
    def _():
        gate_ref[...] = jnp.zeros(gate_ref.shape, F32)

    qh = _head_rows(q_ref[...])
    lane = lax.broadcasted_iota(jnp.int32, gate_ref.shape, 1)
    pages_per_block = MOBA_BLOCK // page_rows
    gate = gate_ref[...]
    for r in range(PAGES_PER_STEP):
        s = _bdot(qh, pages[r][...])
        s_ref[:, r * page_rows:(r + 1) * page_rows] = s
        blk = (j * PAGES_PER_STEP + r) // pages_per_block
        gate = gate + jnp.where(lane == blk, jnp.sum(s, axis=1, keepdims=True), 0.0)
    gate_ref[...] = gate


def _score_call(page_table, q, cache, layer, n_pool):
    b, n_pages = page_table.shape
    page_rows = cache.shape[2]
    past = n_pages * page_rows
    assert n_pages % PAGES_PER_STEP == 0 and MOBA_BLOCK % page_rows == 0
    assert past // MOBA_BLOCK <= LANE

    def page_spec(r):
        return pl.BlockSpec(
            (None, BRANCH_W, page_rows),
            lambda i, j, pt: (layer * n_pool + pt[i, j * PAGES_PER_STEP + r], 0, 0))

    grid_spec = pltpu.PrefetchScalarGridSpec(
        num_scalar_prefetch=1,
        grid=(b, n_pages // PAGES_PER_STEP),
        in_specs=[pl.BlockSpec((None, 1, BRANCH_W), lambda i, j, pt: (i, 0, 0))]
        + [page_spec(r) for r in range(PAGES_PER_STEP)],
        out_specs=[
            pl.BlockSpec((None, N_HEAD, PAGES_PER_STEP * page_rows), lambda i, j, pt: (i, 0, j)),
            pl.BlockSpec((None, N_HEAD, LANE), lambda i, j, pt: (i, 0, 0)),
        ],
    )
    return pl.pallas_call(
        _score_kernel,
        grid_spec=grid_spec,
        out_shape=[jax.ShapeDtypeStruct((b, N_HEAD, past), F32),
                   jax.ShapeDtypeStruct((b, N_HEAD, LANE), F32)],
        compiler_params=_params("parallel", "arbitrary"),
        name="decode_scores",
    )(page_table, q, *([cache] * PAGES_PER_STEP))


def _select_kernel(s_ref, gate_ref, q_ref, k_ref, p_ref, idx_ref, pown_ref, *, n_blocks):
    lane = lax.broadcasted_iota(jnp.int32, gate_ref.shape, 1).astype(F32)
    gate = jnp.where(lane < n_blocks, gate_ref[...], -jnp.inf)
    key_blk = jnp.right_shift(lax.broadcasted_iota(jnp.int32, s_ref.shape, 1),
                              MOBA_BLOCK.bit_length() - 1).astype(F32)
    sel = jnp.zeros(s_ref.shape, F32)
    idx_out = jnp.zeros(gate_ref.shape, F32)
    for j in range(MOBA_TOPK):
        m = jnp.max(gate, axis=1, keepdims=True)
        idx = jnp.min(jnp.where(gate == m, lane, float(LANE)), axis=1, keepdims=True)
        sel = jnp.where(key_blk == idx, 1.0, sel)
        idx_out = jnp.where(lane == j, idx, idx_out)
        gate = jnp.where(lane == idx, -jnp.inf, gate)
    s_own = jnp.sum(_head_rows(q_ref[...]) * k_ref[...], axis=1, keepdims=True)
    s = jnp.where(sel > 0.0, s_ref[...], NEG)
    m = jnp.maximum(jnp.max(s, axis=1, keepdims=True), s_own)
    e = jnp.exp(s - m)
    e_own = jnp.exp(s_own - m)
    l = jnp.sum(e, axis=1, keepdims=True) + e_own
    p_ref[...] = e / l
    pown_ref[...] = jnp.broadcast_to(e_own / l, pown_ref.shape)
    idx_ref[...] = idx_out.astype(jnp.int32)


def _select_call(scores, gate, q, k_new):
    b, _, past = scores.shape
    n_blocks = past // MOBA_BLOCK
    assert n_blocks >= MOBA_TOPK
    head = lambda w: pl.BlockSpec((None, N_HEAD, w), lambda i: (i, 0, 0))
    row = pl.BlockSpec((None, 1, BRANCH_W), lambda i: (i, 0, 0))
    return pl.pallas_call(
        functools.partial(_select_kernel, n_blocks=n_blocks),
        grid=(b,),
        in_specs=[head(past), head(LANE), row, row],
        out_specs=[head(past), head(LANE), head(LANE)],
        out_shape=[jax.ShapeDtypeStruct((b, N_HEAD, past), F32),
                   jax.ShapeDtypeStruct((b, N_HEAD, LANE), jnp.int32),
                   jax.ShapeDtypeStruct((b, N_HEAD, LANE), F32)],
        compiler_params=_params("parallel"),
        name="decode_select",
    )(scores, gate, q, k_new)


def _gather_kernel(pt_ref, ix_ref, pown_ref, vnew_ref, sg_ref, *refs, n_sel):
    del pt_ref, ix_ref
    p_rows = refs[:n_sel]
    v_pages = refs[n_sel:2 * n_sel]
    y_ref = refs[2 * n_sel]
    h = pl.program_id(1)
    acc = pown_ref[pl.ds(h, 1), 0:1] * vnew_ref[...]
    for r in range(n_sel):
        p8 = jnp.broadcast_to(p_rows[r][...], (8, p_rows[r].shape[1]))
        acc = acc + _bdot_nt(p8, v_pages[r][...])[0:1, :]
    y_ref[...] = acc * sg_ref[...]


def _gather_call(page_table, idx, probs, p_own, v_new, sg, cache, layer, n_pool):
    b, n_pages = page_table.shape
    page_rows = cache.shape[2]
    ppb = MOBA_BLOCK // page_rows
    n_sel = MOBA_TOPK * ppb
    probs = probs.reshape(b, N_HEAD, n_pages, 1, page_rows)

    def seq_page(i, h, ix, r):
        return ix[i, h * MOBA_TOPK + r // ppb] * ppb + r % ppb

    def p_spec(r):
        return pl.BlockSpec((None, None, None, 1, page_rows),
                            lambda i, h, pt, ix: (i, h, seq_page(i, h, ix, r), 0, 0))

    def v_spec(r):
        return pl.BlockSpec((None, HEAD_DIM, page_rows),
                            lambda i, h, pt, ix: (layer * n_pool + pt[i, seq_page(i, h, ix, r)], h, 0))

    by_head = lambda a: a.reshape(b, N_HEAD, 1, HEAD_DIM)
    row = pl.BlockSpec((None, None, 1, HEAD_DIM), lambda i, h, pt, ix: (i, h, 0, 0))
    grid_spec = pltpu.PrefetchScalarGridSpec(
        num_scalar_prefetch=2,
        grid=(b, N_HEAD),
        in_specs=[pl.BlockSpec((None, N_HEAD, LANE), lambda i, h, pt, ix: (i, 0, 0)), row, row]
        + [p_spec(r) for r in range(n_sel)] + [v_spec(r) for r in range(n_sel)],
        out_specs=row,
    )
    return pl.pallas_call(
        functools.partial(_gather_kernel, n_sel=n_sel),
        grid_spec=grid_spec,
        out_shape=jax.ShapeDtypeStruct((b, N_HEAD, 1, HEAD_DIM), F32),
        compiler_params=_params("parallel", "parallel"),
        name="decode_gather",
    )(page_table, idx, p_own, by_head(v_new), by_head(sg), *([probs] * n_sel), *([cache] * n_sel))


def _layer_params(i, norm_g, w_in, conv_w, conv_b, conv_ln_g, conv_ln_b, shift_mu, w0, w_lora_b, a0,
                  a_lora_b, k_k, k_a, r_k, gn_g, gn_b, w_branch, w_out, ple_proj, ple_gate):
    w = w_in[i].astype(BF16)
    o1, o2, o3 = N_CONV_IN, N_CONV_IN + N_ATT_IN, N_CONV_IN + N_ATT_IN + N_RWKV_IN
    row = lambda a: a[i].reshape(1, -1)
    zeros = jnp.zeros((LORA, BRANCH_W), F32)
    return dict(
        norm_g=row(norm_g), w_in=w, w_conv=w[:, :o1], w_att=w[:, o1:o2], w_rwkv=w[:, o2:o3],
        w_merge=w[:, o3:],
        conv_w=conv_w[i], conv_b=row(conv_b), conv_ln_g=row(conv_ln_g), conv_ln_b=row(conv_ln_b),
        shift_mu=row(shift_mu), w0=row(w0), a0=row(a0),
        w_lora_b=jnp.concatenate([w_lora_b[i], zeros], axis=0).astype(BF16),
        a_lora_b=jnp.concatenate([zeros, a_lora_b[i]], axis=0).astype(BF16),
        k_k=row(k_k), k_a=row(k_a), r_k=row(r_k), gn_g=row(gn_g), gn_b=row(gn_b),
        w_branch=w_branch[i].astype(BF16), w_out=w_out[i].astype(BF16),
        ple_proj=ple_proj[i].astype(BF16), ple_gate=ple_gate[i].astype(BF16),
        head_bd=_head_block_diag(),
    )


def _prompt_layer(x, pe, lp, tables, final_g, final_norm, layer, depth, kv_t):
    b, t, _ = x.shape
    nb = t // MOBA_BLOCK
    y_conv, conv_new = _conv_call(x, lp["norm_g"], lp["w_conv"], lp["conv_w"], lp["conv_b"],
                                  lp["conv_ln_g"], lp["conv_ln_b"])
    k_t, v_t, qt, kb, vt, sgt, km = _att1_call(x, lp["norm_g"], lp["w_att"], tables, layer, depth, kv_t)
    km = km.reshape(b, nb, N_HEAD, HEAD_DIM).transpose(0, 2, 1, 3)
    y_att_t = _att2_call(qt, kb, vt, km, sgt)
    y_rw, wkv, shift = _rwkv_call(x, lp["norm_g"], lp["w_rwkv"], lp)
    x = _merge_call(x, y_conv, y_att_t, y_rw, pe, lp, final_g,
                    att_transposed=True, final_norm=final_norm)
    return x, (k_t, v_t), conv_new, wkv, shift.reshape(b, SHIFT_W)


def _sample_layer(x, pe, lp, tables, final_g, final_norm, layer, n_pool, cache_k, cache_v,
                  page_table, buf, state, shift0):
    b = x.shape[0]
    z = _proj_call(x.reshape(b, D_MODEL), lp["norm_g"], lp["w_in"])
    y_conv, conv_new, q, k, v, sg, y_rw, wkv, shift = _mix_call(z, buf, state, shift0, lp, tables)
    scores, gate = _score_call(page_table, q, cache_k, layer, n_pool)
    probs, idx, p_own = _select_call(scores, gate, q, k)
    idx = idx[:, :, :MOBA_TOPK].reshape(b, N_HEAD * MOBA_TOPK)
    y_att = _gather_call(page_table, idx, probs, p_own, v, sg, cache_v, layer, n_pool)
    tok = lambda a: a.reshape(1, b, -1)
    x = _merge_call(tok(x), tok(y_conv), tok(y_att), tok(y_rw), tok(pe), lp, final_g,
                    att_transposed=False, final_norm=final_norm)
    return (x.reshape(b, 1, D_MODEL), k, v, conv_new, wkv, shift.reshape(b, SHIFT_W))


def kernel(x_prompt, x_sample, cache_k, cache_v, page_table, state_conv, state_wkv, state_shift,
           p_prompt, p_sample, norm_g, w_in, conv_w, conv_b, conv_ln_g, conv_ln_b, shift_mu, w0,
           w_lora_b, a0, a_lora_b, k_k, k_a, r_k, gn_g, gn_b, w_branch, w_out, ple_proj, ple_gate,
           final_norm_g):
    depth = w_in.shape[0]
    b_p, t_p, _ = x_prompt.shape
    b_s, t_s, _ = x_sample.shape
    assert t_s == 1 and t_p % TOK_TILE == 0 and TOK_TILE == MOBA_BLOCK
    n_pool, page_rows = cache_k.shape[1], cache_k.shape[2]
    past_len = page_table.shape[1] * page_rows
    assert past_len % MOBA_BLOCK == 0
    tables_p = _rope_tables(jnp.arange(t_p, dtype=jnp.int32))
    tables_s = _rope_tables(past_len + jnp.arange(1, dtype=jnp.int32))
    by_page = lambda c: c.transpose(0, 1, 3, 4, 2).reshape(depth * n_pool, BRANCH_W, page_rows)
    cache_k, cache_v = by_page(cache_k), by_page(cache_v)
    final_g = final_norm_g.reshape(1, D_MODEL)
    xp, xs = x_prompt, x_sample
    outs = [[] for _ in range(8)]
    kv_t = None
    for i in range(depth):
        lp = _layer_params(i, norm_g, w_in, conv_w, conv_b, conv_ln_g, conv_ln_b, shift_mu, w0,
                           w_lora_b, a0, a_lora_b, k_k, k_a, r_k, gn_g, gn_b, w_branch, w_out,
                           ple_proj, ple_gate)
        last = i == depth - 1
        xp, kv_t, cp, wp, sp = _prompt_layer(xp, p_prompt[i], lp, tables_p, final_g, last, i,
                                             depth, kv_t)
        xs, ks, vs, cs, ws, ss = _sample_layer(
            xs, p_sample[i], lp, tables_s, final_g, last, i, n_pool, cache_k, cache_v, page_table,
            state_conv[i], state_wkv[i], state_shift[i])
        heads_s = lambda a: a.reshape(b_s, 1, N_HEAD, HEAD_DIM)
        for lst, a in zip(outs, (heads_s(ks), heads_s(vs), cp, cs, wp, ws, sp, ss)):
            lst.append(a)
    heads_p = lambda a: a.reshape(depth, b_p, N_HEAD, HEAD_DIM, t_p).transpose(0, 1, 4, 2, 3)
    return (xp, xs, heads_p(kv_t[0]), heads_p(kv_t[1])) + tuple(jnp.stack(lst) for lst in outs)
```

```python
import functools

import jax
import jax.numpy as jnp
from jax import lax
from jax.experimental import pallas as pl
from jax.experimental.pallas import tpu as pltpu

D_MODEL = 1024
P_DIM = 256
HEAD_DIM = 64
BRANCH_W = 512
N_HEAD = 8
N_PAIR = 4
CONV_W = 31
ROT_DIM = 16
ROPE_THETA = 500000.0
MOBA_BLOCK = 256
MOBA_TOPK = 3
LORA = 64
SHIFT_W = 3 * BRANCH_W + 2 * LORA
N_CONV_IN = 3 * BRANCH_W
N_ATT_IN = 4 * BRANCH_W
N_RWKV_IN = SHIFT_W + BRANCH_W
N_MERGE_IN = 3 * D_MODEL
N_IN = N_CONV_IN + N_ATT_IN + N_RWKV_IN + N_MERGE_IN
NORM_EPS = 1e-6
LN_EPS = 1e-5
GN_EPS = 64e-5
NEG = -1e30
LOG2_E = 1.4426950408889634
DECAY_SCALE = 0.6065306597126334

LANE = 128
TOK_TILE = 256
V_ROWS = 80
BIAS_ROWS = 16
CHUNK = 64
HIST = 32
CONV_ROWS = 64
PAGES_PER_STEP = 32
VMEM_LIMIT = 56 * 1024 * 1024

F32 = jnp.float32
BF16 = jnp.bfloat16


def _bdot(a, b):
    return jnp.dot(a.astype(BF16), b.astype(BF16), preferred_element_type=F32)


def _bdot_nt(a, b):
    return lax.dot_general(a.astype(BF16), b.astype(BF16), (((1,), (1,)), ((), ())),
                           preferred_element_type=F32)


def _split_dot(x, m_bf16):
    hi = x.astype(BF16)
    lo = (x - hi.astype(F32)).astype(BF16)
    return (jnp.dot(hi, m_bf16, preferred_element_type=F32)
            + jnp.dot(lo, m_bf16, preferred_element_type=F32))


def _split3_dot_left(m_bf16, x):
    hi = x.astype(BF16)
    r1 = x - hi.astype(F32)
    mid = r1.astype(BF16)
    lo = (r1 - mid.astype(F32)).astype(BF16)
    return (jnp.dot(m_bf16, hi, preferred_element_type=F32)
            + jnp.dot(m_bf16, mid, preferred_element_type=F32)
            + jnp.dot(m_bf16, lo, preferred_element_type=F32))


def _sigmoid(x):
    return 1.0 / (1.0 + jnp.exp(-x))


def _silu(x):
    return x * _sigmoid(x)


def _rms(x, g):
    return x * lax.rsqrt(jnp.mean(x * x, axis=-1, keepdims=True) + NORM_EPS) * g


def _params(*sem):
    return pltpu.CompilerParams(dimension_semantics=sem, vmem_limit_bytes=VMEM_LIMIT)


def _full(shape):
    nd = len(shape)
    return pl.BlockSpec(shape, lambda *_: (0,) * nd)


def _conv_branch(ext_ref, base, rows, cw_ref, cb, lg, lb, g_conv):
    acc = jnp.zeros((rows, BRANCH_W), F32) + cb
    first = base - (CONV_W - 1)
    for s in range(8):
        z = None
        for j in range(CONV_W):
            if (first + j) % 8 == s:
                start = first + j - s
                term = ext_ref[start:start + rows + 8, :] * cw_ref[j:j + 1, :]
                z = term if z is None else z + term
        acc = acc + z[s:s + rows, :]
    mu = jnp.mean(acc, axis=-1, keepdims=True)
    d = acc - mu
    var = jnp.mean(d * d, axis=-1, keepdims=True)
    y = d * lax.rsqrt(var + LN_EPS) * lg + lb
    return (_silu(y) * _silu(g_conv)).astype(BF16)


def _conv_init(ext_scr, tt):
    ext_scr[0:HIST, :] = jnp.zeros((HIST, BRANCH_W), F32)
    ext_scr[HIST + tt:HIST + tt + 8, :] = jnp.zeros((8, BRANCH_W), F32)


def _conv_tile(h, w_ref, cw_ref, cb_ref, lg_ref, lb_ref, y_ref, cs_ref, z_scr, ext_scr):
    tt = h.shape[0]
    z_scr[...] = jnp.dot(h, w_ref[...], preferred_element_type=F32)
    ext_scr[HIST:HIST + tt, :] = z_scr[:, 0:BRANCH_W] * _sigmoid(z_scr[:, BRANCH_W:2 * BRANCH_W])

    def rows(r):
        y_ref[r:r + CONV_ROWS, :] = _conv_branch(
            ext_scr, HIST + r, CONV_ROWS, cw_ref, cb_ref[...], lg_ref[...], lb_ref[...],
            z_scr[r:r + CONV_ROWS, 2 * BRANCH_W:3 * BRANCH_W])

    def finish():
        cs_ref[...] = ext_scr[HIST + tt - (CONV_W - 1):HIST + tt, :]
        ext_scr[0:HIST, :] = ext_scr[tt:tt + HIST, :]

    return [functools.partial(rows, r) for r in range(0, tt, CONV_ROWS)], finish


def _rope_tables(pos):
    half = ROT_DIM // 2
    inv = jnp.power(ROPE_THETA, -jnp.arange(half, dtype=F32) * (2.0 / ROT_DIM))
    ang = pos.astype(F32)[:, None] * inv[None, :]
    cos, sin = jnp.cos(ang), jnp.sin(ang)
    n = pos.shape[0]
    pad = jnp.zeros((n, HEAD_DIM - ROT_DIM), F32)
    cos64 = jnp.concatenate([cos, cos, pad + 1.0], axis=1)
    sa64 = jnp.concatenate([-sin, jnp.zeros_like(sin), pad], axis=1)
    sb64 = jnp.concatenate([jnp.zeros_like(sin), sin, pad], axis=1)
    return tuple(jnp.concatenate([a, a], axis=1) for a in (cos64, sa64, sb64))


def _rope(a, cos, sa, sb):
    w = a.shape[1]
    return a * cos + pltpu.roll(a, w - ROT_DIM // 2, 1) * sa + pltpu.roll(a, ROT_DIM // 2, 1) * sb


def _tile_lanes(a, n):
    return jnp.concatenate([a] * n, axis=1)


def _att1_kernel(x_ref, g_ref, w_ref, cos_ref, sa_ref, sb_ref, *refs):
    k_ref, v_ref, qt_ref, kb_ref, vt_ref, sgt_ref, km_ref, z_scr = refs[-8:]
    z_scr[...] = _bdot(_rms(x_ref[...], g_ref[...]), w_ref[...])
    n = BRANCH_W // LANE
    cos, sa, sb = (_tile_lanes(r[...], n) for r in (cos_ref, sa_ref, sb_ref))
    q = _rope(z_scr[:, 0:BRANCH_W], cos, sa, sb) * (LOG2_E * HEAD_DIM ** -0.5)
    k = _rope(z_scr[:, BRANCH_W:2 * BRANCH_W], cos, sa, sb)
    v = z_scr[:, 2 * BRANCH_W:3 * BRANCH_W]
    v_t = v.T
    k_ref[...] = k.T
    v_ref[...] = v_t
    qt_ref[...] = q.T.astype(BF16)
    lane = lax.broadcasted_iota(jnp.int32, (k.shape[0], LANE), 1)
    one_hot = jnp.where(lane == HEAD_DIM, 1.0, 0.0)
    for p in range(N_PAIR):
        slab = k[:, p * LANE:(p + 1) * LANE]
        kb_ref[2 * p] = jnp.where(lane < HEAD_DIM, slab, one_hot).astype(BF16)
        kb_ref[2 * p + 1] = jnp.where(lane < HEAD_DIM, pltpu.roll(slab, HEAD_DIM, 1),
                                      one_hot).astype(BF16)
    tail = jnp.where(lax.broadcasted_iota(jnp.int32, (V_ROWS - HEAD_DIM, v_t.shape[1]), 0) == 0,
                     1.0, 0.0)
    for h in range(N_HEAD):
        vt_ref[h * V_ROWS:(h + 1) * V_ROWS, :] = jnp.concatenate(
            [v_t[h * HEAD_DIM:(h + 1) * HEAD_DIM, :], tail], axis=0).astype(BF16)
    sgt_ref[...] = _silu(z_scr[:, 3 * BRANCH_W:4 * BRANCH_W]).T
    km_ref[...] = jnp.mean(k, axis=0, keepdims=True)


def _att1_call(x, norm_g, w_a, tables, layer, depth, kv_t):
    b, t, _ = x.shape
    tt = TOK_TILE
    nb = t // tt
    tok = lambda i, j: (i, j, 0)
    tr = lambda i, j: (i, 0, j)
    kv_spec = pl.BlockSpec((None, None, BRANCH_W, tt), lambda i, j: (layer, i, 0, j))
    kv_shape = jax.ShapeDtypeStruct((depth, b, BRANCH_W, t), F32)
    carried = [] if kv_t is None else [pl.BlockSpec(memory_space=pl.ANY)] * 2
    return pl.pallas_call(
        _att1_kernel,
        grid=(b, nb),
        in_specs=[
            pl.BlockSpec((None, tt, D_MODEL), tok),
            _full((1, D_MODEL)), _full((D_MODEL, N_ATT_IN)),
            pl.BlockSpec((tt, LANE), lambda i, j: (j, 0)),
            pl.BlockSpec((tt, LANE), lambda i, j: (j, 0)),
            pl.BlockSpec((tt, LANE), lambda i, j: (j, 0)),
        ] + carried,
        input_output_aliases={} if kv_t is None else {6: 0, 7: 1},
        out_specs=[
            kv_spec,
            kv_spec,
            pl.BlockSpec((None, BRANCH_W, tt), tr),
            pl.BlockSpec((None, N_HEAD, tt, LANE), lambda i, j: (i, 0, j, 0)),
            pl.BlockSpec((None, None, N_HEAD * V_ROWS, tt), lambda i, j: (i, j, 0, 0)),
            pl.BlockSpec((None, BRANCH_W, tt), tr),
            pl.BlockSpec((None, None, 1, BRANCH_W), lambda i, j: (i, j, 0, 0)),
        ],
        out_shape=[
            kv_shape,
            kv_shape,
            jax.ShapeDtypeStruct((b, BRANCH_W, t), BF16),
            jax.ShapeDtypeStruct((b, N_HEAD, t, LANE), BF16),
            jax.ShapeDtypeStruct((b, nb, N_HEAD * V_ROWS, tt), BF16),
            jax.ShapeDtypeStruct((b, BRANCH_W, t), F32),
            jax.ShapeDtypeStruct((b, nb, 1, BRANCH_W), F32),
        ],
        scratch_shapes=[pltpu.VMEM((tt, N_ATT_IN), F32)],
        compiler_params=_params("parallel", "parallel"),
        name="att_project",
    )(x, norm_g, w_a, *tables, *(kv_t or ()))


def _top_blocks(gate, n_slots):
    nb = gate.shape[0]
    row = lax.broadcasted_iota(jnp.int32, gate.shape, 0).astype(F32)
    sel = jnp.zeros(gate.shape, F32)
    for j in range(MOBA_TOPK):
        m = jnp.max(gate, axis=0, keepdims=True)
        idx = jnp.min(jnp.where(gate == m, row, float(nb)), axis=0, keepdims=True)
        hit = row == jnp.where(j < n_slots, idx, -1.0)
        sel = jnp.where(hit, 1.0, sel)
        gate = jnp.where(row == idx, -jnp.inf, gate)
    return sel


def _att2_kernel(qt_ref, kb_ref, vt_ref, km_ref, sgt_ref, yt_ref,
                 qp_scr, bias_scr, s_scr, e_scr, alpha_scr, m_scr, acc_scr):
    i = pl.program_id(1)
    tq = qt_ref.shape[1]
    nb = km_ref.shape[1]
    blk = lax.broadcasted_iota(jnp.int32, (nb, tq), 0)
    qp_scr[...] = jnp.zeros(qp_scr.shape, BF16)
    for h in range(N_HEAD):
        q_h = qt_ref[h * HEAD_DIM:(h + 1) * HEAD_DIM, :]
        qp_scr[h, 0:HEAD_DIM, :] = q_h
        gate = jnp.dot(km_ref[h].astype(BF16), q_h, preferred_element_type=F32)
        sel = _top_blocks(jnp.where(blk < i, gate, -jnp.inf), i)
        bias_scr[h] = jnp.where(sel > 0.0, 0.0, NEG)
    m_scr[...] = jnp.full(m_scr.shape, NEG, F32)
    acc_scr[...] = jnp.zeros(acc_scr.shape, F32)
    bias_row0 = lax.broadcasted_iota(jnp.int32, (BIAS_ROWS, tq), 0) == 0

    def scores(n, slot, bias_of, extra=None):
        start = pl.multiple_of(n * MOBA_BLOCK, MOBA_BLOCK)
        for h in range(N_HEAD):
            qp_scr[h, HEAD_DIM:HEAD_DIM + BIAS_ROWS, :] = jnp.where(
                bias_row0, bias_of(h), 0.0).astype(BF16)
            s = jnp.dot(kb_ref[h, pl.ds(start, MOBA_BLOCK), :], qp_scr[h],
                        preferred_element_type=F32)
            s_scr[slot, h] = s if extra is None else s + extra

    def softmax(slot):
        for h in range(N_HEAD):
            m = m_scr[h:h + 1, :]
            m_new = jnp.maximum(m, jnp.max(s_scr[slot, h], axis=0, keepdims=True))
            m_scr[h:h + 1, :] = m_new
            alpha_scr[slot, h:h + 1, :] = jnp.exp2(m - m_new)
            e_scr[slot, h] = jnp.exp2(s_scr[slot, h] - m_new).astype(BF16)

    def values(n, slot):
        for h in range(N_HEAD):
            pv = jnp.dot(vt_ref[n, h * V_ROWS:(h + 1) * V_ROWS, :], e_scr[slot, h],
                         preferred_element_type=F32)
            acc_scr[h] = alpha_scr[slot, h:h + 1, :] * acc_scr[h] + pv

    def row_bias(n):
        return lambda h: bias_scr[h, pl.ds(n, 1), :]

    def block_of(v):
        return jnp.where(v == 0, i, jnp.minimum(v - 1, nb - 1))

    def step(t, slot):
        nxt = block_of(t + 1)
        scores(nxt, 1 - slot, row_bias(nxt))
        values(block_of(t - 1), 1 - slot)
        softmax(slot)

    causal = jnp.where(lax.broadcasted_iota(jnp.int32, (MOBA_BLOCK, tq), 0)
                       <= lax.broadcasted_iota(jnp.int32, (MOBA_BLOCK, tq), 1), 0.0, NEG)
    scores(i, 0, lambda h: jnp.zeros((1, tq), F32), extra=causal)
    softmax(0)
    scores(0, 1, row_bias(0))

    def body(j, carry):
        step(2 * j + 1, 1)
        step(2 * j + 2, 0)
        return carry

    trips = (i + 1) // 2
    lax.fori_loop(0, trips, body, 0)
    values(block_of(2 * trips), 0)
    for h in range(N_HEAD):
        rows = slice(h * HEAD_DIM, (h + 1) * HEAD_DIM)
        denom = acc_scr[h, HEAD_DIM:HEAD_DIM + 1, :]
        yt_ref[rows, :] = acc_scr[h, 0:HEAD_DIM, :] / denom * sgt_ref[rows, :]


def _att2_call(qt, kb, vt, km, sgt):
    b, _, t = qt.shape
    nb = vt.shape[1]
    tq = MOBA_BLOCK
    return pl.pallas_call(
        _att2_kernel,
        grid=(b, t // tq),
        in_specs=[
            pl.BlockSpec((None, BRANCH_W, tq), lambda bi, i: (bi, 0, i)),
            pl.BlockSpec((None, N_HEAD, t, LANE), lambda bi, i: (bi, 0, 0, 0),
                         pipeline_mode=pl.Buffered(1)),
            pl.BlockSpec((None, nb, N_HEAD * V_ROWS, MOBA_BLOCK), lambda bi, i: (bi, 0, 0, 0),
                         pipeline_mode=pl.Buffered(1)),
            pl.BlockSpec((None, N_HEAD, nb, HEAD_DIM), lambda bi, i: (bi, 0, 0, 0)),
            pl.BlockSpec((None, BRANCH_W, tq), lambda bi, i: (bi, 0, i)),
        ],
        out_specs=pl.BlockSpec((None, BRANCH_W, tq), lambda bi, i: (bi, 0, i)),
        out_shape=jax.ShapeDtypeStruct((b, BRANCH_W, t), F32),
        scratch_shapes=[
            pltpu.VMEM((N_HEAD, LANE, tq), BF16), pltpu.VMEM((N_HEAD, nb, tq), F32),
            pltpu.VMEM((2, N_HEAD, MOBA_BLOCK, tq), F32), pltpu.VMEM((2, N_HEAD, MOBA_BLOCK, tq), BF16),
            pltpu.VMEM((2, N_HEAD, tq), F32), pltpu.VMEM((N_HEAD, tq), F32),
            pltpu.VMEM((N_HEAD, V_ROWS, tq), F32),
        ],
        compiler_params=_params("parallel", "arbitrary"),
        name="moba_attention",
    )(qt, kb, vt, km, sgt)


def _rwkv_columns(mixed, w0, wlb_ref, a0, alb_ref, kk_scale, k_a, bd_ref):
    r = mixed[:, 0:BRANCH_W]
    k = mixed[:, BRANCH_W:2 * BRANCH_W]
    v = mixed[:, 2 * BRANCH_W:3 * BRANCH_W]
    lora = mixed[:, 3 * BRANCH_W:3 * BRANCH_W + 2 * LORA]
    lane = lax.broadcasted_iota(jnp.int32, lora.shape, 1)
    lora = jnp.where(lane < LORA, jnp.tanh(lora), lora)
    log_decay = -(DECAY_SCALE * _sigmoid(w0 + _bdot(lora, wlb_ref[...])))
    a_lr = _sigmoid(a0 + _bdot(lora, alb_ref[...]))
    kk = k * kk_scale
    kk = kk * lax.rsqrt(jnp.maximum(_bdot(kk * kk, bd_ref[...]), 1e-24))
    k = k * (1.0 + (a_lr - 1.0) * k_a)
    return r, k, v, log_decay, kk, a_lr


def _rwkv_finish(y, r, k, v, r_k, gn_g, gn_b, g_rwkv, bd_ref):
    m = _split_dot(y, bd_ref[...]) * (1.0 / HEAD_DIM)
    d = y - m
    var = _bdot(d * d, bd_ref[...]) * (1.0 / HEAD_DIM)
    yn = d * lax.rsqrt(var + GN_EPS) * gn_g + gn_b
    bonus = _bdot(r * k * r_k, bd_ref[...]) * v
    return ((yn + bonus) * _silu(g_rwkv)).astype(BF16)


def _stack_heads(a, mask_a):
    return jnp.concatenate([jnp.where(mask_a, a, 0.0), jnp.where(mask_a, 0.0, a)], axis=0)


def _rwkv_init(shs_scr, s_scr):
    shs_scr[0:8, :] = jnp.zeros((8, SHIFT_W), F32)
    s_scr[...] = jnp.zeros(s_scr.shape, F32)


def _rwkv_tile(h, fillers, w_ref, mu_ref, w0_ref, wlb_ref, a0_ref, alb_ref, kk_ref, ka_ref,
               rk_ref, gg_ref, gb_ref, bd_ref,
               y_ref, st_ref, sh_ref,
               z_scr, shs_scr, r_scr, k_scr, v_scr, ld_scr, a_scr, b_scr, y_scr, s_scr,
               cum_scr, ar_scr, bk_scr, at_scr, vs_scr, vst_scr, bend_scr, kend_scr, pw_scr, akrk_scr,
               arb_scr, wu_scr, wut_scr, g_scr, q_scr, x_scr, yv_scr, kv_scr, h_scr, y0_scr, pt_scr):
    tt = h.shape[0]
    fillers = list(fillers)

    def fill(n=1):
        for _ in range(min(n, len(fillers))):
            fillers.pop(0)()

    z_scr[...] = jnp.dot(h, w_ref[...], preferred_element_type=F32)
    fill(2)
    shs_scr[8:8 + tt, :] = z_scr[:, 0:SHIFT_W]
    cur = z_scr[:, 0:SHIFT_W]
    mixed = cur + (shs_scr[7:7 + tt, :] - cur) * mu_ref[...]
    sh_ref[...] = shs_scr[7 + tt:8 + tt, :]
    shs_scr[7:8, :] = shs_scr[7 + tt:8 + tt, :]

    r, k, v, log_decay, kk, a_lr = _rwkv_columns(
        mixed, w0_ref[...], wlb_ref, a0_ref[...], alb_ref, kk_ref[...], ka_ref[...], bd_ref)
    r_scr[...] = r
    k_scr[...] = k
    v_scr[...] = v
    ld_scr[...] = log_decay
    a_scr[...] = -kk
    b_scr[...] = kk * a_lr

    n2 = 2 * CHUNK
    row = lax.broadcasted_iota(jnp.int32, (n2, n2), 0)
    col = lax.broadcasted_iota(jnp.int32, (n2, n2), 1)
    same = (row >= CHUNK) == (col >= CHUNK)
    rt = jnp.where(row >= CHUNK, row - CHUNK, row)
    ct = jnp.where(col >= CHUNK, col - CHUNK, col)
    strict = same & (rt > ct)
    incl = same & (rt >= ct)
    crow = lax.broadcasted_iota(jnp.int32, (CHUNK, CHUNK), 0)
    ccol = lax.broadcasted_iota(jnp.int32, (CHUNK, CHUNK), 1)
    tri = jnp.where(crow >= ccol, 1.0, 0.0).astype(BF16)
    mask_a = lax.broadcasted_iota(jnp.int32, (CHUNK, LANE), 1) < HEAD_DIM

    n_chunk = tt // CHUNK
    units = [(c, p) for c in range(n_chunk) for p in range(N_PAIR)]
    n_unit = range(len(units))
    dot = functools.partial(jnp.dot, preferred_element_type=F32)

    def nt(a, b):
        return lax.dot_general(a, b, (((1,), (1,)), ((), ())), preferred_element_type=F32)

    def blk(ref, c, p):
        return ref[c * CHUNK:(c + 1) * CHUNK, p * LANE:(p + 1) * LANE]

    for c in range(n_chunk):
        rows = slice(c * CHUNK, (c + 1) * CHUNK)
        cum_scr[rows, :] = _split3_dot_left(tri, ld_scr[rows, :])
    for u, (c, p) in enumerate(units):
        cum = blk(cum_scr, c, p)
        last = cum[CHUNK - 1:CHUNK, :]
        p_inv = jnp.exp(-cum)
        p_end = jnp.exp(last - cum)
        bb = blk(b_scr, c, p)
        kc = blk(k_scr, c, p)
        v_s = _stack_heads(blk(v_scr, c, p), mask_a)
        a_s = _stack_heads(blk(a_scr, c, p) * jnp.exp(cum - blk(ld_scr, c, p)),
                           mask_a).astype(BF16)
        ar_scr[u, 0:n2, :] = a_s
        ar_scr[u, n2:2 * n2, :] = _stack_heads(blk(r_scr, c, p) * jnp.exp(cum), mask_a).astype(BF16)
        bk_scr[u, 0:n2, :] = _stack_heads(bb * p_inv, mask_a).astype(BF16)
        bk_scr[u, n2:2 * n2, :] = _stack_heads(kc * p_inv, mask_a).astype(BF16)
        at_scr[u, :, 0:n2] = a_s
        vs_scr[u] = v_s.astype(BF16)
        vst_scr[u] = v_s.T.astype(BF16)
        bend_scr[u] = _stack_heads(bb * p_end, mask_a).astype(BF16)
        kend_scr[u] = _stack_heads(kc * p_end, mask_a).astype(BF16)
        pt_scr[u] = jnp.broadcast_to(jnp.exp(last), (8, LANE))
    eye = jnp.where(row == col, 1.0, 0.0)
    for u in n_unit:
        quad = nt(ar_scr[u], bk_scr[u])
        a_ab = jnp.where(strict, quad[0:n2, 0:n2], 0.0)
        x_scr[u] = eye + a_ab
        pw_scr[u] = a_ab.astype(BF16)
        akrk_scr[u, 0:n2, :] = jnp.where(strict, quad[0:n2, n2:2 * n2], 0.0).astype(BF16)
        akrk_scr[u, n2:2 * n2, :] = jnp.where(incl, quad[n2:2 * n2, n2:2 * n2], 0.0).astype(BF16)
        arb_scr[u] = jnp.where(incl, quad[n2:2 * n2, 0:n2], 0.0).astype(BF16)
    for _ in range(CHUNK.bit_length() - 2):
        for u in n_unit:
            pw_scr[u] = dot(pw_scr[u], pw_scr[u]).astype(BF16)
        for u in n_unit:
            x_scr[u] = x_scr[u] + dot(x_scr[u].astype(BF16), pw_scr[u])
        fill()
    for u in n_unit:
        ty = dot(akrk_scr[u], vs_scr[u])
        at_scr[u, :, n2:2 * n2] = ty[0:n2, :].astype(BF16)
        yv_scr[u] = ty[n2:2 * n2, :]
    fill()
    for u in n_unit:
        kv_scr[u] = dot(vst_scr[u], kend_scr[u])
    fill(len(fillers))
    for u in n_unit:
        wu = dot(x_scr[u].astype(BF16), at_scr[u])
        wu_scr[u] = wu.astype(BF16)
        wut_scr[u] = wu.T.astype(BF16)
    for u in n_unit:
        gh = dot(wut_scr[u], bend_scr[u])
        g_scr[u] = gh[0:n2, :].astype(BF16)
        h_scr[u] = gh[n2:2 * n2, :] + kv_scr[u]
    for u in n_unit:
        qy = dot(arb_scr[u], wu_scr[u])
        q_scr[u] = (ar_scr[u, n2:2 * n2, :].astype(F32) + qy[:, 0:n2]).astype(BF16)
        y0_scr[u] = qy[:, n2:2 * n2] + yv_scr[u]
    for c in range(n_chunk):
        rows = slice(c * CHUNK, (c + 1) * CHUNK)
        us = [c * N_PAIR + p for p in range(N_PAIR)]
        s0 = [s_scr[p] for p in range(N_PAIR)]
        s0b = [a.astype(BF16) for a in s0]
        for p, u in enumerate(us):
            s_scr[p] = s0[p] * pt_scr[u, 0:1, :] + dot(s0b[p], g_scr[u]) + h_scr[u]
        for p, u in enumerate(us):
            y_s = nt(q_scr[u], s0b[p]) + y0_scr[u]
            y_scr[rows, p * LANE:(p + 1) * LANE] = y_s[0:CHUNK, :] + y_s[CHUNK:n2, :]

    y_ref[...] = _rwkv_finish(y_scr[...], r_scr[...], k_scr[...], v_scr[...], rk_ref[...],
                              gg_ref[...], gb_ref[...], z_scr[:, SHIFT_W:SHIFT_W + BRANCH_W], bd_ref)
    for p in range(N_PAIR):
        st_ref[2 * p] = s_scr[p, 0:HEAD_DIM, 0:HEAD_DIM]
        st_ref[2 * p + 1] = s_scr[p, HEAD_DIM:LANE, HEAD_DIM:LANE]


def _head_block_diag():
    i = jnp.arange(BRANCH_W) // HEAD_DIM
    return (i[:, None] == i[None, :]).astype(BF16)


N_CONV_REFS = (5, 2, 2)
N_RWKV_REFS = (12, 3, 31)


def _conv_rwkv_kernel(x_ref, g_ref, *refs):
    ci, co, cs = N_CONV_REFS
    ri, ro, rs = N_RWKV_REFS
    conv_in, refs = refs[:ci], refs[ci:]
    rwkv_in, refs = refs[:ri], refs[ri:]
    conv_out, refs = refs[:co], refs[co:]
    rwkv_out, refs = refs[:ro], refs[ro:]
    conv_scr, rwkv_scr = refs[:cs], refs[cs:]
    assert len(rwkv_scr) == rs
    tt = x_ref.shape[0]

    @pl.when(pl.program_id(1) == 0)
    def _():
        _conv_init(conv_scr[1], tt)
        _rwkv_init(rwkv_scr[1], rwkv_scr[9])

    h = _rms(x_ref[...], g_ref[...]).astype(BF16)
    conv_chunks, conv_finish = _conv_tile(h, *conv_in, *conv_out, *conv_scr)
    _rwkv_tile(h, conv_chunks, *rwkv_in, *rwkv_out, *rwkv_scr)
    conv_finish()


def _conv_rwkv_call(x, lp):
    b, t, _ = x.shape
    tt = TOK_TILE
    vec = _full((1, BRANCH_W))
    scr = lambda w: pltpu.VMEM((tt, w), F32)
    units = (tt // CHUNK) * N_PAIR
    unit = lambda dt, r=1, c=1: pltpu.VMEM((units, r * LANE, c * LANE), dt)
    tok = pl.BlockSpec((None, tt, BRANCH_W), lambda i, j: (i, j, 0))
    return pl.pallas_call(
        _conv_rwkv_kernel,
        grid=(b, t // tt),
        in_specs=[
            pl.BlockSpec((None, tt, D_MODEL), lambda i, j: (i, j, 0)), _full((1, D_MODEL)),
            _full((D_MODEL, N_CONV_IN)), _full((CONV_W, BRANCH_W)), vec, vec, vec,
            _full((D_MODEL, N_RWKV_IN)), _full((1, SHIFT_W)),
            vec, _full((2 * LORA, BRANCH_W)), vec, _full((2 * LORA, BRANCH_W)), vec, vec,
            vec, vec, vec, _full((BRANCH_W, BRANCH_W)),
        ],
        out_specs=[
            tok,
            pl.BlockSpec((None, CONV_W - 1, BRANCH_W), lambda i, j: (i, 0, 0)),
            tok,
            pl.BlockSpec((None, N_HEAD, HEAD_DIM, HEAD_DIM), lambda i, j: (i, 0, 0, 0)),
            pl.BlockSpec((None, 1, SHIFT_W), lambda i, j: (i, 0, 0)),
        ],
        out_shape=[
            jax.ShapeDtypeStruct((b, t, BRANCH_W), BF16),
            jax.ShapeDtypeStruct((b, CONV_W - 1, BRANCH_W), F32),
            jax.ShapeDtypeStruct((b, t, BRANCH_W), BF16),
            jax.ShapeDtypeStruct((b, N_HEAD, HEAD_DIM, HEAD_DIM), F32),
            jax.ShapeDtypeStruct((b, 1, SHIFT_W), F32),
        ],
        scratch_shapes=[
            scr(N_CONV_IN), pltpu.VMEM((HIST + tt + 8, BRANCH_W), F32),
            scr(N_RWKV_IN), pltpu.VMEM((8 + tt, SHIFT_W), F32),
            scr(BRANCH_W), scr(BRANCH_W), scr(BRANCH_W), scr(BRANCH_W), scr(BRANCH_W), scr(BRANCH_W),
            scr(BRANCH_W), pltpu.VMEM((N_PAIR, LANE, LANE), F32),
            scr(BRANCH_W), unit(BF16, 2), unit(BF16, 2), unit(BF16, 1, 2), *([unit(BF16)] * 5),
            unit(BF16, 2), unit(BF16), unit(BF16, 1, 2), unit(BF16, 2), unit(BF16), unit(BF16),
            *([unit(F32)] * 5), pltpu.VMEM((units, 8, LANE), F32),
        ],
        compiler_params=_params("parallel", "arbitrary"),
        name="conv_rwkv_branches",
    )(x, lp["norm_g"], lp["w_conv"], lp["conv_w"], lp["conv_b"], lp["conv_ln_g"], lp["conv_ln_b"],
      lp["w_rwkv"], lp["shift_mu"], lp["w0"], lp["w_lora_b"], lp["a0"], lp["a_lora_b"],
      lp["k_k"], lp["k_a"], lp["r_k"], lp["gn_g"], lp["gn_b"], lp["head_bd"])


def _merge_kernel(x_ref, yc_ref, ya_ref, yr_ref, pe_ref, g_ref, wm_ref, wb_ref, wo_ref, pg_ref,
                  pp_ref, fg_ref, o_ref, mg_scr, *, att_transposed, final_norm):
    x = x_ref[...]
    mg_scr[...] = _bdot(_rms(x, g_ref[...]), wm_ref[...])
    ya = ya_ref[...].T if att_transposed else ya_ref[...]
    s = (_sigmoid(mg_scr[:, 0:D_MODEL]) * _bdot(yc_ref[...], wb_ref[0])
         + _sigmoid(mg_scr[:, D_MODEL:2 * D_MODEL]) * _bdot(ya, wb_ref[1])
         + _sigmoid(mg_scr[:, 2 * D_MODEL:3 * D_MODEL]) * _bdot(yr_ref[...], wb_ref[2]))
    x = x + _bdot(s, wo_ref[...])
    x = x + _sigmoid(_bdot(x, pg_ref[...])) * _bdot(pe_ref[...], pp_ref[...])
    o_ref[...] = _rms(x, fg_ref[...]) if final_norm else x


def _merge_call(x, y_conv, y_att, y_rw, pe, lp, final_g, *, att_transposed, final_norm):
    b, t, _ = x.shape
    tm = min(TOK_TILE, t)
    tok = lambda i, j: (i, j, 0)
    att_spec = (pl.BlockSpec((None, BRANCH_W, tm), lambda i, j: (i, 0, j)) if att_transposed
                else pl.BlockSpec((None, tm, BRANCH_W), tok))
    return pl.pallas_call(
        functools.partial(_merge_kernel, att_transposed=att_transposed, final_norm=final_norm),
        grid=(b, t // tm),
        in_specs=[
            pl.BlockSpec((None, tm, D_MODEL), tok),
            pl.BlockSpec((None, tm, BRANCH_W), tok),
            att_spec,
            pl.BlockSpec((None, tm, BRANCH_W), tok),
            pl.BlockSpec((None, tm, P_DIM), tok),
            _full((1, D_MODEL)), _full((D_MODEL, N_MERGE_IN)), _full((3, BRANCH_W, D_MODEL)),
            _full((D_MODEL, D_MODEL)), _full((D_MODEL, D_MODEL)), _full((P_DIM, D_MODEL)),
            _full((1, D_MODEL)),
        ],
        out_specs=pl.BlockSpec((None, tm, D_MODEL), tok),
        out_shape=jax.ShapeDtypeStruct((b, t, D_MODEL), F32),
        scratch_shapes=[pltpu.VMEM((tm, N_MERGE_IN), F32)],
        compiler_params=_params("parallel", "parallel"),
        name="merge",
    )(x, y_conv, y_att, y_rw, pe, lp["norm_g"], lp["w_merge"], lp["w_branch"], lp["w_out"],
      lp["ple_gate"], lp["ple_proj"], final_g)


def _proj_kernel(x_ref, g_ref, w_ref, z_ref):
    z_ref[...] = _bdot(_rms(x_ref[...], g_ref[...]), w_ref[...])


def _proj_call(x, norm_g, w_in):
    m = x.shape[0]
    n_tiles = 3
    tn = N_IN // n_tiles
    return pl.pallas_call(
        _proj_kernel,
        grid=(n_tiles,),
        in_specs=[_full((m, D_MODEL)), _full((1, D_MODEL)),
                  pl.BlockSpec((D_MODEL, tn), lambda j: (0, j))],
        out_specs=pl.BlockSpec((m, tn), lambda j: (0, j)),
        out_shape=jax.ShapeDtypeStruct((m, N_IN), F32),
        compiler_params=_params("parallel"),
        name="decode_project",
    )(x, norm_g, w_in)


def _row_to_col(row_vec):
    n = row_vec.shape[1]
    eye = lax.broadcasted_iota(jnp.int32, (n, n), 0) == lax.broadcasted_iota(jnp.int32, (n, n), 1)
    return jnp.sum(jnp.where(eye, row_vec, 0.0), axis=1, keepdims=True)


def _col_to_row(col_vec):
    n = col_vec.shape[0]
    eye = lax.broadcasted_iota(jnp.int32, (n, n), 0) == lax.broadcasted_iota(jnp.int32, (n, n), 1)
    return jnp.sum(jnp.where(eye, col_vec, 0.0), axis=0, keepdims=True)


def _mix_kernel(z_ref, buf_ref, st_ref, sh0_ref, cw_ref, cb_ref, lg_ref, lb_ref, cos_ref, sa_ref,
                sb_ref, mu_ref, w0_ref, wlb_ref, a0_ref, alb_ref, kk_ref, ka_ref, rk_ref, gg_ref,
                gb_ref, bd_ref,
                yc_ref, cs_ref, q_ref, k_ref, v_ref, sg_ref, yr_ref, so_ref, sho_ref, ext_scr):
    o_att = N_CONV_IN
    o_rw = N_CONV_IN + N_ATT_IN
    u = z_ref[:, 0:BRANCH_W] * _sigmoid(z_ref[:, BRANCH_W:2 * BRANCH_W])
    ext_scr[0:8, :] = jnp.zeros((8, BRANCH_W), F32)
    ext_scr[2:HIST, :] = buf_ref[...]
    ext_scr[HIST:HIST + 16, :] = jnp.broadcast_to(u, (16, BRANCH_W))
    yc_ref[...] = _conv_branch(ext_scr, HIST, 8, cw_ref, cb_ref[...], lg_ref[...], lb_ref[...],
                               z_ref[:, 2 * BRANCH_W:3 * BRANCH_W])[0:1, :]
    cs_ref[...] = ext_scr[3:HIST + 1, :]
    n = BRANCH_W // LANE
    cos, sa, sb = (_tile_lanes(r[...], n) for r in (cos_ref, sa_ref, sb_ref))
    q_ref[...] = _rope(z_ref[:, o_att:o_att + BRANCH_W], cos, sa, sb) * (HEAD_DIM ** -0.5)
    k_ref[...] = _rope(z_ref[:, o_att + BRANCH_W:o_att + 2 * BRANCH_W], cos, sa, sb)
    v_ref[...] = z_ref[:, o_att + 2 * BRANCH_W:o_att + 3 * BRANCH_W]
    sg_ref[...] = _silu(z_ref[:, o_att + 3 * BRANCH_W:o_att + 4 * BRANCH_W])
    cur = z_ref[:, o_rw:o_rw + SHIFT_W]
    sho_ref[...] = cur
    mixed = cur + (sh0_ref[...] - cur) * mu_ref[...]
    r, k, v, log_decay, kk, a_lr = _rwkv_columns(
        mixed, w0_ref[...], wlb_ref, a0_ref[...], alb_ref, kk_ref[...], ka_ref[...], bd_ref)
    decay = jnp.exp(log_decay)
    b_vec = kk * a_lr
    ys = []
    for h in range(N_HEAD):
        lanes = slice(h * HEAD_DIM, (h + 1) * HEAD_DIM)
        s = st_ref[h]
        sa_col = jnp.sum(s * (-kk[:, lanes]), axis=1, keepdims=True)
        s = s * decay[:, lanes] + sa_col * b_vec[:, lanes] + _row_to_col(v[:, lanes]) * k[:, lanes]
        so_ref[h] = s
        ys.append(_col_to_row(jnp.sum(s * r[:, lanes], axis=1, keepdims=True)))
    y = jnp.concatenate(ys, axis=1)
    yr_ref[...] = _rwkv_finish(y, r, k, v, rk_ref[...], gg_ref[...], gb_ref[...],
                               z_ref[:, o_rw + SHIFT_W:o_rw + SHIFT_W + BRANCH_W], bd_ref)


def _mix_call(z, buf, state, shift0, lp, tables):
    b = z.shape[0]
    row = lambda w: pl.BlockSpec((None, 1, w), lambda i: (i, 0, 0))
    vec = _full((1, BRANCH_W))
    tab = _full((1, LANE))
    o = lambda w, dt=F32: jax.ShapeDtypeStruct((b, 1, w), dt)
    return pl.pallas_call(
        _mix_kernel,
        grid=(b,),
        in_specs=[
            row(N_IN),
            pl.BlockSpec((None, CONV_W - 1, BRANCH_W), lambda i: (i, 0, 0)),
            pl.BlockSpec((None, N_HEAD, HEAD_DIM, HEAD_DIM), lambda i: (i, 0, 0, 0)),
            row(SHIFT_W),
            _full((CONV_W, BRANCH_W)), vec, vec, vec, tab, tab, tab,
            _full((1, SHIFT_W)), vec, _full((2 * LORA, BRANCH_W)), vec, _full((2 * LORA, BRANCH_W)),
            vec, vec, vec, vec, vec, _full((BRANCH_W, BRANCH_W)),
        ],
        out_specs=[
            row(BRANCH_W),
            pl.BlockSpec((None, CONV_W - 1, BRANCH_W), lambda i: (i, 0, 0)),
            row(BRANCH_W), row(BRANCH_W), row(BRANCH_W), row(BRANCH_W), row(BRANCH_W),
            pl.BlockSpec((None, N_HEAD, HEAD_DIM, HEAD_DIM), lambda i: (i, 0, 0, 0)),
            row(SHIFT_W),
        ],
        out_shape=[
            o(BRANCH_W, BF16), jax.ShapeDtypeStruct((b, CONV_W - 1, BRANCH_W), F32),
            o(BRANCH_W), o(BRANCH_W), o(BRANCH_W), o(BRANCH_W), o(BRANCH_W, BF16),
            jax.ShapeDtypeStruct((b, N_HEAD, HEAD_DIM, HEAD_DIM), F32), o(SHIFT_W),
        ],
        scratch_shapes=[pltpu.VMEM((HIST + 16, BRANCH_W), F32)],
        compiler_params=_params("parallel"),
        name="decode_mix",
    )(z.reshape(b, 1, N_IN), buf, state, shift0.reshape(b, 1, SHIFT_W),
      lp["conv_w"], lp["conv_b"], lp["conv_ln_g"], lp["conv_ln_b"], *tables,
      lp["shift_mu"], lp["w0"], lp["w_lora_b"], lp["a0"], lp["a_lora_b"], lp["k_k"], lp["k_a"],
      lp["r_k"], lp["gn_g"], lp["gn_b"], lp["head_bd"])


def _head_rows(q_row):
    lane = lax.broadcasted_iota(jnp.int32, (N_HEAD, BRANCH_W), 1)
    row = lax.broadcasted_iota(jnp.int32, (N_HEAD, BRANCH_W), 0)
    return jnp.where((lane >= row * HEAD_DIM) & (lane < (row + 1) * HEAD_DIM), q_row, 0.0)


def _score_kernel(pt_ref, q_ref, *refs):
    del pt_ref
    pages = refs[:PAGES_PER_STEP]
    s_ref, gate_ref = refs[PAGES_PER_STEP:]
    j = pl.program_id(1)
    page_rows = pages[0].shape[1]

    @pl.when(j == 0)
    def _():
        gate_ref[...] = jnp.zeros(gate_ref.shape, F32)

    qh = _head_rows(q_ref[...])
    lane = lax.broadcasted_iota(jnp.int32, gate_ref.shape, 1)
    pages_per_block = MOBA_BLOCK // page_rows
    gate = gate_ref[...]
    for r in range(PAGES_PER_STEP):
        s = _bdot(qh, pages[r][...])
        s_ref[:, r * page_rows:(r + 1) * page_rows] = s
        blk = (j * PAGES_PER_STEP + r) // pages_per_block
        gate = gate + jnp.where(lane == blk, jnp.sum(s, axis=1, keepdims=True), 0.0)
    gate_ref[...] = gate


def _score_call(page_table, q, cache, layer, n_pool):
    b, n_pages = page_table.shape
    page_rows = cache.shape[2]
    past = n_pages * page_rows
    assert n_pages % PAGES_PER_STEP == 0 and MOBA_BLOCK % page_rows == 0
    assert past // MOBA_BLOCK <= LANE

    def page_spec(r):
        return pl.BlockSpec(
            (None, BRANCH_W, page_rows),
            lambda i, j, pt: (layer * n_pool + pt[i, j * PAGES_PER_STEP + r], 0, 0))

    grid_spec = pltpu.PrefetchScalarGridSpec(
        num_scalar_prefetch=1,
        grid=(b, n_pages // PAGES_PER_STEP),
        in_specs=[pl.BlockSpec((None, 1, BRANCH_W), lambda i, j, pt: (i, 0, 0))]
        + [page_spec(r) for r in range(PAGES_PER_STEP)],
        out_specs=[
            pl.BlockSpec((None, N_HEAD, PAGES_PER_STEP * page_rows), lambda i, j, pt: (i, 0, j)),
            pl.BlockSpec((None, N_HEAD, LANE), lambda i, j, pt: (i, 0, 0)),
        ],
    )
    return pl.pallas_call(
        _score_kernel,
        grid_spec=grid_spec,
        out_shape=[jax.ShapeDtypeStruct((b, N_HEAD, past), F32),
                   jax.ShapeDtypeStruct((b, N_HEAD, LANE), F32)],
        compiler_params=_params("parallel", "arbitrary"),
        name="decode_scores",
    )(page_table, q, *([cache] * PAGES_PER_STEP))


def _select_kernel(s_ref, gate_ref, q_ref, k_ref, p_ref, idx_ref, pown_ref, *, n_blocks):
    lane = lax.broadcasted_iota(jnp.int32, gate_ref.shape, 1).astype(F32)
    gate = jnp.where(lane < n_blocks, gate_ref[...], -jnp.inf)
    key_blk = jnp.right_shift(lax.broadcasted_iota(jnp.int32, s_ref.shape, 1),
                              MOBA_BLOCK.bit_length() - 1).astype(F32)
    sel = jnp.zeros(s_ref.shape, F32)
    idx_out = jnp.zeros(gate_ref.shape, F32)
    for j in range(MOBA_TOPK):
        m = jnp.max(gate, axis=1, keepdims=True)
        idx = jnp.min(jnp.where(gate == m, lane, float(LANE)), axis=1, keepdims=True)
        sel = jnp.where(key_blk == idx, 1.0, sel)
        idx_out = jnp.where(lane == j, idx, idx_out)
        gate = jnp.where(lane == idx, -jnp.inf, gate)
    s_own = jnp.sum(_head_rows(q_ref[...]) * k_ref[...], axis=1, keepdims=True)
    s = jnp.where(sel > 0.0, s_ref[...], NEG)
    m = jnp.maximum(jnp.max(s, axis=1, keepdims=True), s_own)
    e = jnp.exp(s - m)
    e_own = jnp.exp(s_own - m)
    l = jnp.sum(e, axis=1, keepdims=True) + e_own
    p_ref[...] = e / l
    pown_ref[...] = jnp.broadcast_to(e_own / l, pown_ref.shape)
    idx_ref[...] = idx_out.astype(jnp.int32)


def _select_call(scores, gate, q, k_new):
    b, _, past = scores.shape
    n_blocks = past // MOBA_BLOCK
    assert n_blocks >= MOBA_TOPK
    head = lambda w: pl.BlockSpec((None, N_HEAD, w), lambda i: (i, 0, 0))
    row = pl.BlockSpec((None, 1, BRANCH_W), lambda i: (i, 0, 0))
    return pl.pallas_call(
        functools.partial(_select_kernel, n_blocks=n_blocks),
        grid=(b,),
        in_specs=[head(past), head(LANE), row, row],
        out_specs=[head(past), head(LANE), head(LANE)],
        out_shape=[jax.ShapeDtypeStruct((b, N_HEAD, past), F32),
                   jax.ShapeDtypeStruct((b, N_HEAD, LANE), jnp.int32),
                   jax.ShapeDtypeStruct((b, N_HEAD, LANE), F32)],
        compiler_params=_params("parallel"),
        name="decode_select",
    )(scores, gate, q, k_new)


def _gather_kernel(pt_ref, ix_ref, pown_ref, vnew_ref, sg_ref, *refs, n_sel):
    del pt_ref, ix_ref
    p_rows = refs[:n_sel]
    v_pages = refs[n_sel:2 * n_sel]
    y_ref = refs[2 * n_sel]
    h = pl.program_id(1)
    acc = pown_ref[pl.ds(h, 1), 0:1] * vnew_ref[...]
    for r in range(n_sel):
        p8 = jnp.broadcast_to(p_rows[r][pl.ds(h, 1), :], (8, p_rows[r].shape[1]))
        acc = acc + _bdot_nt(p8, v_pages[r][...])[0:1, :]
    y_ref[...] = acc * sg_ref[...]


def _gather_call(page_table, idx, probs, p_own, v_new, sg, cache, layer, n_pool):
    b, n_pages = page_table.shape
    page_rows = cache.shape[2]
    ppb = MOBA_BLOCK // page_rows
    n_sel = MOBA_TOPK * ppb

    def seq_page(i, h, ix, r):
        return ix[i, h * MOBA_TOPK + r // ppb] * ppb + r % ppb

    def p_spec(r):
        return pl.BlockSpec((None, N_HEAD, page_rows),
                            lambda i, h, pt, ix: (i, 0, seq_page(i, h, ix, r)))

    def v_spec(r):
        return pl.BlockSpec((None, HEAD_DIM, page_rows),
                            lambda i, h, pt, ix: (layer * n_pool + pt[i, seq_page(i, h, ix, r)], h, 0))

    by_head = lambda a: a.reshape(b, N_HEAD, 1, HEAD_DIM)
    row = pl.BlockSpec((None, None, 1, HEAD_DIM), lambda i, h, pt, ix: (i, h, 0, 0))
    grid_spec = pltpu.PrefetchScalarGridSpec(
        num_scalar_prefetch=2,
        grid=(b, N_HEAD),
        in_specs=[pl.BlockSpec((None, N_HEAD, LANE), lambda i, h, pt, ix: (i, 0, 0)), row, row]
        + [p_spec(r) for r in range(n_sel)] + [v_spec(r) for r in range(n_sel)],
        out_specs=row,
    )
    return pl.pallas_call(
        functools.partial(_gather_kernel, n_sel=n_sel),
        grid_spec=grid_spec,
        out_shape=jax.ShapeDtypeStruct((b, N_HEAD, 1, HEAD_DIM), F32),
        compiler_params=_params("parallel", "parallel"),
        name="decode_gather",
    )(page_table, idx, p_own, by_head(v_new), by_head(sg), *([probs] * n_sel), *([cache] * n_sel))


def _layer_params(i, norm_g, w_in, conv_w, conv_b, conv_ln_g, conv_ln_b, shift_mu, w0, w_lora_b, a0,
                  a_lora_b, k_k, k_a, r_k, gn_g, gn_b, w_branch, w_out, ple_proj, ple_gate):
    w = w_in[i].astype(BF16)
    o1, o2, o3 = N_CONV_IN, N_CONV_IN + N_ATT_IN, N_CONV_IN + N_ATT_IN + N_RWKV_IN
    row = lambda a: a[i].reshape(1, -1)
    zeros = jnp.zeros((LORA, BRANCH_W), F32)
    return dict(
        norm_g=row(norm_g), w_in=w, w_conv=w[:, :o1], w_att=w[:, o1:o2], w_rwkv=w[:, o2:o3],
        w_merge=w[:, o3:],
        conv_w=conv_w[i], conv_b=row(conv_b), conv_ln_g=row(conv_ln_g), conv_ln_b=row(conv_ln_b),
        shift_mu=row(shift_mu), w0=row(w0), a0=row(a0),
        w_lora_b=jnp.concatenate([w_lora_b[i], zeros], axis=0).astype(BF16),
        a_lora_b=jnp.concatenate([zeros, a_lora_b[i]], axis=0).astype(BF16),
        k_k=row(k_k), k_a=row(k_a), r_k=row(r_k), gn_g=row(gn_g), gn_b=row(gn_b),
        w_branch=w_branch[i].astype(BF16), w_out=w_out[i].astype(BF16),
        ple_proj=ple_proj[i].astype(BF16), ple_gate=ple_gate[i].astype(BF16),
        head_bd=_head_block_diag(),
    )


def _prompt_layer(x, pe, lp, tables, final_g, final_norm, layer, depth, kv_t):
    b, t, _ = x.shape
    nb = t // MOBA_BLOCK
    y_conv, conv_new, y_rw, wkv, shift = _conv_rwkv_call(x, lp)
    k_t, v_t, qt, kb, vt, sgt, km = _att1_call(x, lp["norm_g"], lp["w_att"], tables, layer, depth, kv_t)
    km = km.reshape(b, nb, N_HEAD, HEAD_DIM).transpose(0, 2, 1, 3)
    y_att_t = _att2_call(qt, kb, vt, km, sgt)
    x = _merge_call(x, y_conv, y_att_t, y_rw, pe, lp, final_g,
                    att_transposed=True, final_norm=final_norm)
    return x, (k_t, v_t), conv_new, wkv, shift.reshape(b, SHIFT_W)


def _sample_layer(x, pe, lp, tables, final_g, final_norm, layer, n_pool, cache_k, cache_v,
                  page_table, buf, state, shift0):
    b = x.shape[0]
    z = _proj_call(x.reshape(b, D_MODEL), lp["norm_g"], lp["w_in"])
    y_conv, conv_new, q, k, v, sg, y_rw, wkv, shift = _mix_call(z, buf, state, shift0, lp, tables)
    scores, gate = _score_call(page_table, q, cache_k, layer, n_pool)
    probs, idx, p_own = _select_call(scores, gate, q, k)
    idx = idx[:, :, :MOBA_TOPK].reshape(b, N_HEAD * MOBA_TOPK)
    y_att = _gather_call(page_table, idx, probs, p_own, v, sg, cache_v, layer, n_pool)
    tok = lambda a: a.reshape(1, b, -1)
    x = _merge_call(tok(x), tok(y_conv), tok(y_att), tok(y_rw), tok(pe), lp, final_g,
                    att_transposed=False, final_norm=final_norm)
    return (x.reshape(b, 1, D_MODEL), k, v, conv_new, wkv, shift.reshape(b, SHIFT_W))


def kernel(x_prompt, x_sample, cache_k, cache_v, page_table, state_conv, state_wkv, state_shift,
           p_prompt, p_sample, norm_g, w_in, conv_w, conv_b, conv_ln_g, conv_ln_b, shift_mu, w0,
           w_lora_b, a0, a_lora_b, k_k, k_a, r_k, gn_g, gn_b, w_branch, w_out, ple_proj, ple_gate,
           final_norm_g):
    depth = w_in.shape[0]
    b_p, t_p, _ = x_prompt.shape
    b_s, t_s, _ = x_sample.shape
    assert t_s == 1 and t_p % TOK_TILE == 0 and TOK_TILE == MOBA_BLOCK
    n_pool, page_rows = cache_k.shape[1], cache_k.shape[2]
    past_len = page_table.shape[1] * page_rows
    assert past_len % MOBA_BLOCK == 0
    tables_p = _rope_tables(jnp.arange(t_p, dtype=jnp.int32))
    tables_s = _rope_tables(past_len + jnp.arange(1, dtype=jnp.int32))
    by_page = lambda c: c.transpose(0, 1, 3, 4, 2).reshape(depth * n_pool, BRANCH_W, page_rows)
    cache_k, cache_v = by_page(cache_k), by_page(cache_v)
    final_g = final_norm_g.reshape(1, D_MODEL)
    xp, xs = x_prompt, x_sample
    outs = [[] for _ in range(8)]
    kv_t = None
    for i in range(depth):
        lp = _layer_params(i, norm_g, w_in, conv_w, conv_b, conv_ln_g, conv_ln_b, shift_mu, w0,
                           w_lora_b, a0, a_lora_b, k_k, k_a, r_k, gn_g, gn_b, w_branch, w_out,
                           ple_proj, ple_gate)
        last = i == depth - 1
        xp, kv_t, cp, wp, sp = _prompt_layer(xp, p_prompt[i], lp, tables_p, final_g, last, i,
                                             depth, kv_t)
        xs, ks, vs, cs, ws, ss = _sample_layer(
            xs, p_sample[i], lp, tables_s, final_g, last, i, n_pool, cache_k, cache_v, page_table,
            state_conv[i], state_wkv[i], state_shift[i])
        heads_s = lambda a: a.reshape(b_s, 1, N_HEAD, HEAD_DIM)
        for lst, a in zip(outs, (heads_s(ks), heads_s(vs), cp, cs, wp, ws, sp, ss)):
            lst.append(a)
    heads_p = lambda a: a.reshape(depth, b_p, N_HEAD, HEAD_DIM, t_p).transpose(0, 1, 4, 2, 3)
    return (xp, xs, heads_p(kv_t[0]), heads_p(kv_t[1])) + tuple(jnp.stack(lst) for lst in outs)
```

```python
import functools

import jax
import jax.numpy as jnp
from jax import lax
from jax.experimental import pallas as pl
from jax.experimental.pallas import tpu as pltpu

D_MODEL = 1024
P_DIM = 256
HEAD_DIM = 64
BRANCH_W = 512
N_HEAD = 8
N_PAIR = 4
CONV_W = 31
ROT_DIM = 16
ROPE_THETA = 500000.0
MOBA_BLOCK = 256
MOBA_TOPK = 3
LORA = 64
SHIFT_W = 3 * BRANCH_W + 2 * LORA
N_CONV_IN = 3 * BRANCH_W
N_ATT_IN = 4 * BRANCH_W
N_RWKV_IN = SHIFT_W + BRANCH_W
N_MERGE_IN = 3 * D_MODEL
N_IN = N_CONV_IN + N_ATT_IN + N_RWKV_IN + N_MERGE_IN
NORM_EPS = 1e-6
LN_EPS = 1e-5
GN_EPS = 64e-5
NEG = -1e30
LOG2_E = 1.4426950408889634
DECAY_SCALE = 0.6065306597126334

LANE = 128
TOK_TILE = 256
V_ROWS = 80
BIAS_ROWS = 16
CHUNK = 64
HIST = 32
CONV_ROWS = 64
GATHER_HEADS = 4
PAGES_PER_STEP = 32
VMEM_LIMIT = 56 * 1024 * 1024

F32 = jnp.float32
BF16 = jnp.bfloat16


def _bdot(a, b):
    return jnp.dot(a.astype(BF16), b.astype(BF16), preferred_element_type=F32)


def _bdot_nt(a, b):
    return lax.dot_general(a.astype(BF16), b.astype(BF16), (((1,), (1,)), ((), ())),
                           preferred_element_type=F32)


def _split_dot(x, m_bf16):
    hi = x.astype(BF16)
    lo = (x - hi.astype(F32)).astype(BF16)
    return (jnp.dot(hi, m_bf16, preferred_element_type=F32)
            + jnp.dot(lo, m_bf16, preferred_element_type=F32))


def _split3_dot_left(m_bf16, x):
    hi = x.astype(BF16)
    r1 = x - hi.astype(F32)
    mid = r1.astype(BF16)
    lo = (r1 - mid.astype(F32)).astype(BF16)
    return (jnp.dot(m_bf16, hi, preferred_element_type=F32)
            + jnp.dot(m_bf16, mid, preferred_element_type=F32)
            + jnp.dot(m_bf16, lo, preferred_element_type=F32))


def _sigmoid(x):
    return 1.0 / (1.0 + jnp.exp(-x))


def _silu(x):
    return x * _sigmoid(x)


def _rms(x, g):
    return x * lax.rsqrt(jnp.mean(x * x, axis=-1, keepdims=True) + NORM_EPS) * g


def _params(*sem):
    return pltpu.CompilerParams(dimension_semantics=sem, vmem_limit_bytes=VMEM_LIMIT)


def _full(shape):
    nd = len(shape)
    return pl.BlockSpec(shape, lambda *_: (0,) * nd)


def _conv_branch(ext_ref, base, rows, cw_ref, cb, lg, lb, g_conv):
    acc = jnp.zeros((rows, BRANCH_W), F32) + cb
    first = base - (CONV_W - 1)
    for s in range(8):
        z = None
        for j in range(CONV_W):
            if (first + j) % 8 == s:
                start = first + j - s
                term = ext_ref[start:start + rows + 8, :] * cw_ref[j:j + 1, :]
                z = term if z is None else z + term
        acc = acc + z[s:s + rows, :]
    mu = jnp.mean(acc, axis=-1, keepdims=True)
    d = acc - mu
    var = jnp.mean(d * d, axis=-1, keepdims=True)
    y = d * lax.rsqrt(var + LN_EPS) * lg + lb
    return (_silu(y) * _silu(g_conv)).astype(BF16)


def _conv_init(ext_scr, tt):
    ext_scr[0:HIST, :] = jnp.zeros((HIST, BRANCH_W), F32)
    ext_scr[HIST + tt:HIST + tt + 8, :] = jnp.zeros((8, BRANCH_W), F32)


def _conv_tile(h, w_ref, cw_ref, cb_ref, lg_ref, lb_ref, y_ref, cs_ref, z_scr, ext_scr):
    tt = h.shape[0]
    z_scr[...] = jnp.dot(h, w_ref[...], preferred_element_type=F32)
    ext_scr[HIST:HIST + tt, :] = z_scr[:, 0:BRANCH_W] * _sigmoid(z_scr[:, BRANCH_W:2 * BRANCH_W])

    def rows(r):
        y_ref[r:r + CONV_ROWS, :] = _conv_branch(
            ext_scr, HIST + r, CONV_ROWS, cw_ref, cb_ref[...], lg_ref[...], lb_ref[...],
            z_scr[r:r + CONV_ROWS, 2 * BRANCH_W:3 * BRANCH_W])

    def finish():
        cs_ref[...] = ext_scr[HIST + tt - (CONV_W - 1):HIST + tt, :]
        ext_scr[0:HIST, :] = ext_scr[tt:tt + HIST, :]

    return [functools.partial(rows, r) for r in range(0, tt, CONV_ROWS)], finish


def _rope_tables(pos):
    half = ROT_DIM // 2
    inv = jnp.power(ROPE_THETA, -jnp.arange(half, dtype=F32) * (2.0 / ROT_DIM))
    ang = pos.astype(F32)[:, None] * inv[None, :]
    cos, sin = jnp.cos(ang), jnp.sin(ang)
    n = pos.shape[0]
    pad = jnp.zeros((n, HEAD_DIM - ROT_DIM), F32)
    cos64 = jnp.concatenate([cos, cos, pad + 1.0], axis=1)
    sa64 = jnp.concatenate([-sin, jnp.zeros_like(sin), pad], axis=1)
    sb64 = jnp.concatenate([jnp.zeros_like(sin), sin, pad], axis=1)
    return tuple(jnp.concatenate([a, a], axis=1) for a in (cos64, sa64, sb64))


def _rope(a, cos, sa, sb):
    w = a.shape[1]
    return a * cos + pltpu.roll(a, w - ROT_DIM // 2, 1) * sa + pltpu.roll(a, ROT_DIM // 2, 1) * sb


def _tile_lanes(a, n):
    return jnp.concatenate([a] * n, axis=1)


def _att1_kernel(x_ref, g_ref, w_ref, cos_ref, sa_ref, sb_ref, *refs):
    k_ref, v_ref, qt_ref, kb_ref, vt_ref, sgt_ref, km_ref, z_scr = refs[-8:]
    z_scr[...] = _bdot(_rms(x_ref[...], g_ref[...]), w_ref[...])
    n = BRANCH_W // LANE
    cos, sa, sb = (_tile_lanes(r[...], n) for r in (cos_ref, sa_ref, sb_ref))
    q = _rope(z_scr[:, 0:BRANCH_W], cos, sa, sb) * (LOG2_E * HEAD_DIM ** -0.5)
    k = _rope(z_scr[:, BRANCH_W:2 * BRANCH_W], cos, sa, sb)
    v = z_scr[:, 2 * BRANCH_W:3 * BRANCH_W]
    v_t = v.T
    k_ref[...] = k.T
    v_ref[...] = v_t
    qt_ref[...] = q.T.astype(BF16)
    lane = lax.broadcasted_iota(jnp.int32, (k.shape[0], LANE), 1)
    one_hot = jnp.where(lane == HEAD_DIM, 1.0, 0.0)
    for p in range(N_PAIR):
        slab = k[:, p * LANE:(p + 1) * LANE]
        kb_ref[2 * p] = jnp.where(lane < HEAD_DIM, slab, one_hot).astype(BF16)
        kb_ref[2 * p + 1] = jnp.where(lane < HEAD_DIM, pltpu.roll(slab, HEAD_DIM, 1),
                                      one_hot).astype(BF16)
    tail = jnp.where(lax.broadcasted_iota(jnp.int32, (V_ROWS - HEAD_DIM, v_t.shape[1]), 0) == 0,
                     1.0, 0.0)
    for h in range(N_HEAD):
        vt_ref[h * V_ROWS:(h + 1) * V_ROWS, :] = jnp.concatenate(
            [v_t[h * HEAD_DIM:(h + 1) * HEAD_DIM, :], tail], axis=0).astype(BF16)
    sgt_ref[...] = _silu(z_scr[:, 3 * BRANCH_W:4 * BRANCH_W]).T
    km_ref[...] = jnp.mean(k, axis=0, keepdims=True)


def _att1_call(x, norm_g, w_a, tables, layer, depth, kv_t):
    b, t, _ = x.shape
    tt = TOK_TILE
    nb = t // tt
    tok = lambda i, j: (i, j, 0)
    tr = lambda i, j: (i, 0, j)
    kv_spec = pl.BlockSpec((None, None, BRANCH_W, tt), lambda i, j: (layer, i, 0, j))
    kv_shape = jax.ShapeDtypeStruct((depth, b, BRANCH_W, t), F32)
    carried = [] if kv_t is None else [pl.BlockSpec(memory_space=pl.ANY)] * 2
    return pl.pallas_call(
        _att1_kernel,
        grid=(b, nb),
        in_specs=[
            pl.BlockSpec((None, tt, D_MODEL), tok),
            _full((1, D_MODEL)), _full((D_MODEL, N_ATT_IN)),
            pl.BlockSpec((tt, LANE), lambda i, j: (j, 0)),
            pl.BlockSpec((tt, LANE), lambda i, j: (j, 0)),
            pl.BlockSpec((tt, LANE), lambda i, j: (j, 0)),
        ] + carried,
        input_output_aliases={} if kv_t is None else {6: 0, 7: 1},
        out_specs=[
            kv_spec,
            kv_spec,
            pl.BlockSpec((None, BRANCH_W, tt), tr),
            pl.BlockSpec((None, N_HEAD, tt, LANE), lambda i, j: (i, 0, j, 0)),
            pl.BlockSpec((None, None, N_HEAD * V_ROWS, tt), lambda i, j: (i, j, 0, 0)),
            pl.BlockSpec((None, BRANCH_W, tt), tr),
            pl.BlockSpec((None, None, 1, BRANCH_W), lambda i, j: (i, j, 0, 0)),
        ],
        out_shape=[
            kv_shape,
            kv_shape,
            jax.ShapeDtypeStruct((b, BRANCH_W, t), BF16),
            jax.ShapeDtypeStruct((b, N_HEAD, t, LANE), BF16),
            jax.ShapeDtypeStruct((b, nb, N_HEAD * V_ROWS, tt), BF16),
            jax.ShapeDtypeStruct((b, BRANCH_W, t), F32),
            jax.ShapeDtypeStruct((b, nb, 1, BRANCH_W), F32),
        ],
        scratch_shapes=[pltpu.VMEM((tt, N_ATT_IN), F32)],
        compiler_params=_params("parallel", "parallel"),
        name="att_project",
    )(x, norm_g, w_a, *tables, *(kv_t or ()))


def _top_blocks(gate, n_slots):
    nb = gate.shape[0]
    row = lax.broadcasted_iota(jnp.int32, gate.shape, 0).astype(F32)
    sel = jnp.zeros(gate.shape, F32)
    for j in range(MOBA_TOPK):
        m = jnp.max(gate, axis=0, keepdims=True)
        idx = jnp.min(jnp.where(gate == m, row, float(nb)), axis=0, keepdims=True)
        hit = row == jnp.where(j < n_slots, idx, -1.0)
        sel = jnp.where(hit, 1.0, sel)
        gate = jnp.where(row == idx, -jnp.inf, gate)
    return sel


def _att2_kernel(qt_ref, kb_ref, vt_ref, km_ref, sgt_ref, yt_ref,
                 qp_scr, bias_scr, s_scr, e_scr, alpha_scr, m_scr, acc_scr):
    i = pl.program_id(1)
    tq = qt_ref.shape[1]
    nb = km_ref.shape[1]
    blk = lax.broadcasted_iota(jnp.int32, (nb, tq), 0)
    qp_scr[...] = jnp.zeros(qp_scr.shape, BF16)
    for h in range(N_HEAD):
        q_h = qt_ref[h * HEAD_DIM:(h + 1) * HEAD_DIM, :]
        qp_scr[h, 0:HEAD_DIM, :] = q_h
        gate = jnp.dot(km_ref[h].astype(BF16), q_h, preferred_element_type=F32)
        sel = _top_blocks(jnp.where(blk < i, gate, -jnp.inf), i)
        bias_scr[h] = jnp.where(sel > 0.0, 0.0, NEG)
    m_scr[...] = jnp.full(m_scr.shape, NEG, F32)
    acc_scr[...] = jnp.zeros(acc_scr.shape, F32)
    bias_row0 = lax.broadcasted_iota(jnp.int32, (BIAS_ROWS, tq), 0) == 0

    def scores(n, slot, bias_of, extra=None):
        start = pl.multiple_of(n * MOBA_BLOCK, MOBA_BLOCK)
        for h in range(N_HEAD):
            qp_scr[h, HEAD_DIM:HEAD_DIM + BIAS_ROWS, :] = jnp.where(
                bias_row0, bias_of(h), 0.0).astype(BF16)
            s = jnp.dot(kb_ref[h, pl.ds(start, MOBA_BLOCK), :], qp_scr[h],
                        preferred_element_type=F32)
            s_scr[slot, h] = s if extra is None else s + extra

    def softmax(slot):
        for h in range(N_HEAD):
            m = m_scr[h:h + 1, :]
            m_new = jnp.maximum(m, jnp.max(s_scr[slot, h], axis=0, keepdims=True))
            m_scr[h:h + 1, :] = m_new
            alpha_scr[slot, h:h + 1, :] = jnp.exp2(m - m_new)
            e_scr[slot, h] = jnp.exp2(s_scr[slot, h] - m_new).astype(BF16)

    def values(n, slot):
        for h in range(N_HEAD):
            pv = jnp.dot(vt_ref[n, h * V_ROWS:(h + 1) * V_ROWS, :], e_scr[slot, h],
                         preferred_element_type=F32)
            acc_scr[h] = alpha_scr[slot, h:h + 1, :] * acc_scr[h] + pv

    def row_bias(n):
        return lambda h: bias_scr[h, pl.ds(n, 1), :]

    def block_of(v):
        return jnp.where(v == 0, i, jnp.minimum(v - 1, nb - 1))

    def step(t, slot):
        nxt = block_of(t + 1)
        scores(nxt, 1 - slot, row_bias(nxt))
        values(block_of(t - 1), 1 - slot)
        softmax(slot)

    causal = jnp.where(lax.broadcasted_iota(jnp.int32, (MOBA_BLOCK, tq), 0)
                       <= lax.broadcasted_iota(jnp.int32, (MOBA_BLOCK, tq), 1), 0.0, NEG)
    scores(i, 0, lambda h: jnp.zeros((1, tq), F32), extra=causal)
    softmax(0)
    scores(0, 1, row_bias(0))

    def body(j, carry):
        step(2 * j + 1, 1)
        step(2 * j + 2, 0)
        return carry

    trips = (i + 1) // 2
    lax.fori_loop(0, trips, body, 0)
    values(block_of(2 * trips), 0)
    for h in range(N_HEAD):
        rows = slice(h * HEAD_DIM, (h + 1) * HEAD_DIM)
        denom = acc_scr[h, HEAD_DIM:HEAD_DIM + 1, :]
        yt_ref[rows, :] = acc_scr[h, 0:HEAD_DIM, :] / denom * sgt_ref[rows, :]


def _att2_call(qt, kb, vt, km, sgt):
    b, _, t = qt.shape
    nb = vt.shape[1]
    tq = MOBA_BLOCK
    return pl.pallas_call(
        _att2_kernel,
        grid=(b, t // tq),
        in_specs=[
            pl.BlockSpec((None, BRANCH_W, tq), lambda bi, i: (bi, 0, i)),
            pl.BlockSpec((None, N_HEAD, t, LANE), lambda bi, i: (bi, 0, 0, 0),
                         pipeline_mode=pl.Buffered(1)),
            pl.BlockSpec((None, nb, N_HEAD * V_ROWS, MOBA_BLOCK), lambda bi, i: (bi, 0, 0, 0),
                         pipeline_mode=pl.Buffered(1)),
            pl.BlockSpec((None, N_HEAD, nb, HEAD_DIM), lambda bi, i: (bi, 0, 0, 0)),
            pl.BlockSpec((None, BRANCH_W, tq), lambda bi, i: (bi, 0, i)),
        ],
        out_specs=pl.BlockSpec((None, BRANCH_W, tq), lambda bi, i: (bi, 0, i)),
        out_shape=jax.ShapeDtypeStruct((b, BRANCH_W, t), F32),
        scratch_shapes=[
            pltpu.VMEM((N_HEAD, LANE, tq), BF16), pltpu.VMEM((N_HEAD, nb, tq), F32),
            pltpu.VMEM((2, N_HEAD, MOBA_BLOCK, tq), F32), pltpu.VMEM((2, N_HEAD, MOBA_BLOCK, tq), BF16),
            pltpu.VMEM((2, N_HEAD, tq), F32), pltpu.VMEM((N_HEAD, tq), F32),
            pltpu.VMEM((N_HEAD, V_ROWS, tq), F32),
        ],
        compiler_params=_params("parallel", "arbitrary"),
        name="moba_attention",
    )(qt, kb, vt, km, sgt)


def _rwkv_columns(mixed, w0, wlb_ref, a0, alb_ref, kk_scale, k_a, bd_ref):
    r = mixed[:, 0:BRANCH_W]
    k = mixed[:, BRANCH_W:2 * BRANCH_W]
    v = mixed[:, 2 * BRANCH_W:3 * BRANCH_W]
    lora = mixed[:, 3 * BRANCH_W:3 * BRANCH_W + 2 * LORA]
    lane = lax.broadcasted_iota(jnp.int32, lora.shape, 1)
    lora = jnp.where(lane < LORA, jnp.tanh(lora), lora)
    log_decay = -(DECAY_SCALE * _sigmoid(w0 + _bdot(lora, wlb_ref[...])))
    a_lr = _sigmoid(a0 + _bdot(lora, alb_ref[...]))
    kk = k * kk_scale
    kk = kk * lax.rsqrt(jnp.maximum(_bdot(kk * kk, bd_ref[...]), 1e-24))
    k = k * (1.0 + (a_lr - 1.0) * k_a)
    return r, k, v, log_decay, kk, a_lr


def _rwkv_finish(y, r, k, v, r_k, gn_g, gn_b, g_rwkv, bd_ref):
    m = _split_dot(y, bd_ref[...]) * (1.0 / HEAD_DIM)
    d = y - m
    var = _bdot(d * d, bd_ref[...]) * (1.0 / HEAD_DIM)
    yn = d * lax.rsqrt(var + GN_EPS) * gn_g + gn_b
    bonus = _bdot(r * k * r_k, bd_ref[...]) * v
    return ((yn + bonus) * _silu(g_rwkv)).astype(BF16)


def _stack_heads(a, mask_a):
    return jnp.concatenate([jnp.where(mask_a, a, 0.0), jnp.where(mask_a, 0.0, a)], axis=0)


def _rwkv_init(shs_scr, s_scr):
    shs_scr[0:8, :] = jnp.zeros((8, SHIFT_W), F32)
    s_scr[...] = jnp.zeros(s_scr.shape, F32)


def _rwkv_tile(h, fillers, w_ref, mu_ref, w0_ref, wlb_ref, a0_ref, alb_ref, kk_ref, ka_ref,
               rk_ref, gg_ref, gb_ref, bd_ref,
               y_ref, st_ref, sh_ref,
               z_scr, shs_scr, r_scr, k_scr, v_scr, ld_scr, a_scr, b_scr, y_scr, s_scr,
               cum_scr, ar_scr, bk_scr, at_scr, vs_scr, vst_scr, bend_scr, kend_scr, pw_scr, akrk_scr,
               arb_scr, wu_scr, wut_scr, g_scr, q_scr, x_scr, yv_scr, kv_scr, h_scr, y0_scr, pt_scr):
    tt = h.shape[0]
    fillers = list(fillers)

    def fill(n=1):
        for _ in range(min(n, len(fillers))):
            fillers.pop(0)()

    z_scr[...] = jnp.dot(h, w_ref[...], preferred_element_type=F32)
    fill(2)
    shs_scr[8:8 + tt, :] = z_scr[:, 0:SHIFT_W]
    cur = z_scr[:, 0:SHIFT_W]
    mixed = cur + (shs_scr[7:7 + tt, :] - cur) * mu_ref[...]
    sh_ref[...] = shs_scr[7 + tt:8 + tt, :]
    shs_scr[7:8, :] = shs_scr[7 + tt:8 + tt, :]

    r, k, v, log_decay, kk, a_lr = _rwkv_columns(
        mixed, w0_ref[...], wlb_ref, a0_ref[...], alb_ref, kk_ref[...], ka_ref[...], bd_ref)
    r_scr[...] = r
    k_scr[...] = k
    v_scr[...] = v
    ld_scr[...] = log_decay
    a_scr[...] = -kk
    b_scr[...] = kk * a_lr

    n2 = 2 * CHUNK
    row = lax.broadcasted_iota(jnp.int32, (n2, n2), 0)
    col = lax.broadcasted_iota(jnp.int32, (n2, n2), 1)
    same = (row >= CHUNK) == (col >= CHUNK)
    rt = jnp.where(row >= CHUNK, row - CHUNK, row)
    ct = jnp.where(col >= CHUNK, col - CHUNK, col)
    strict = same & (rt > ct)
    incl = same & (rt >= ct)
    crow = lax.broadcasted_iota(jnp.int32, (CHUNK, CHUNK), 0)
    ccol = lax.broadcasted_iota(jnp.int32, (CHUNK, CHUNK), 1)
    tri = jnp.where(crow >= ccol, 1.0, 0.0).astype(BF16)
    mask_a = lax.broadcasted_iota(jnp.int32, (CHUNK, LANE), 1) < HEAD_DIM

    n_chunk = tt // CHUNK
    units = [(c, p) for c in range(n_chunk) for p in range(N_PAIR)]
    n_unit = range(len(units))
    dot = functools.partial(jnp.dot, preferred_element_type=F32)

    def nt(a, b):
        return lax.dot_general(a, b, (((1,), (1,)), ((), ())), preferred_element_type=F32)

    def blk(ref, c, p):
        return ref[c * CHUNK:(c + 1) * CHUNK, p * LANE:(p + 1) * LANE]

    for c in range(n_chunk):
        rows = slice(c * CHUNK, (c + 1) * CHUNK)
        cum_scr[rows, :] = _split3_dot_left(tri, ld_scr[rows, :])
    for u, (c, p) in enumerate(units):
        cum = blk(cum_scr, c, p)
        last = cum[CHUNK - 1:CHUNK, :]
        p_inv = jnp.exp(-cum)
        p_end = jnp.exp(last - cum)
        bb = blk(b_scr, c, p)
        kc = blk(k_scr, c, p)
        v_s = _stack_heads(blk(v_scr, c, p), mask_a)
        a_s = _stack_heads(blk(a_scr, c, p) * jnp.exp(cum - blk(ld_scr, c, p)),
                           mask_a).astype(BF16)
        ar_scr[u, 0:n2, :] = a_s
        ar_scr[u, n2:2 * n2, :] = _stack_heads(blk(r_scr, c, p) * jnp.exp(cum), mask_a).astype(BF16)
        bk_scr[u, 0:n2, :] = _stack_heads(bb * p_inv, mask_a).astype(BF16)
        bk_scr[u, n2:2 * n2, :] = _stack_heads(kc * p_inv, mask_a).astype(BF16)
        at_scr[u, :, 0:n2] = a_s
        vs_scr[u] = v_s.astype(BF16)
        vst_scr[u] = v_s.T.astype(BF16)
        bend_scr[u] = _stack_heads(bb * p_end, mask_a).astype(BF16)
        kend_scr[u] = _stack_heads(kc * p_end, mask_a).astype(BF16)
        pt_scr[u] = jnp.broadcast_to(jnp.exp(last), (8, LANE))
    eye = jnp.where(row == col, 1.0, 0.0)
    for u in n_unit:
        quad = nt(ar_scr[u], bk_scr[u])
        a_ab = jnp.where(strict, quad[0:n2, 0:n2], 0.0)
        x_scr[u] = eye + a_ab
        pw_scr[u] = a_ab.astype(BF16)
        akrk_scr[u, 0:n2, :] = jnp.where(strict, quad[0:n2, n2:2 * n2], 0.0).astype(BF16)
        akrk_scr[u, n2:2 * n2, :] = jnp.where(incl, quad[n2:2 * n2, n2:2 * n2], 0.0).astype(BF16)
        arb_scr[u] = jnp.where(incl, quad[n2:2 * n2, 0:n2], 0.0).astype(BF16)
    for _ in range(CHUNK.bit_length() - 2):
        for u in n_unit:
            pw_scr[u] = dot(pw_scr[u], pw_scr[u]).astype(BF16)
        for u in n_unit:
            x_scr[u] = x_scr[u] + dot(x_scr[u].astype(BF16), pw_scr[u])
        fill()
    for u in n_unit:
        ty = dot(akrk_scr[u], vs_scr[u])
        at_scr[u, :, n2:2 * n2] = ty[0:n2, :].astype(BF16)
        yv_scr[u] = ty[n2:2 * n2, :]
    fill()
    for u in n_unit:
        kv_scr[u] = dot(vst_scr[u], kend_scr[u])
    fill(len(fillers))
    for u in n_unit:
        wu = dot(x_scr[u].astype(BF16), at_scr[u])
        wu_scr[u] = wu.astype(BF16)
        wut_scr[u] = wu.T.astype(BF16)
    for u in n_unit:
        gh = dot(wut_scr[u], bend_scr[u])
        g_scr[u] = gh[0:n2, :].astype(BF16)
        h_scr[u] = gh[n2:2 * n2, :] + kv_scr[u]
    for u in n_unit:
        qy = dot(arb_scr[u], wu_scr[u])
        q_scr[u] = (ar_scr[u, n2:2 * n2, :].astype(F32) + qy[:, 0:n2]).astype(BF16)
        y0_scr[u] = qy[:, n2:2 * n2] + yv_scr[u]
    for c in range(n_chunk):
        rows = slice(c * CHUNK, (c + 1) * CHUNK)
        us = [c * N_PAIR + p for p in range(N_PAIR)]
        s0 = [s_scr[p] for p in range(N_PAIR)]
        s0b = [a.astype(BF16) for a in s0]
        for p, u in enumerate(us):
            s_scr[p] = s0[p] * pt_scr[u, 0:1, :] + dot(s0b[p], g_scr[u]) + h_scr[u]
        for p, u in enumerate(us):
            y_s = nt(q_scr[u], s0b[p]) + y0_scr[u]
            y_scr[rows, p * LANE:(p + 1) * LANE] = y_s[0:CHUNK, :] + y_s[CHUNK:n2, :]

    y_ref[...] = _rwkv_finish(y_scr[...], r_scr[...], k_scr[...], v_scr[...], rk_ref[...],
                              gg_ref[...], gb_ref[...], z_scr[:, SHIFT_W:SHIFT_W + BRANCH_W], bd_ref)
    for p in range(N_PAIR):
        st_ref[2 * p] = s_scr[p, 0:HEAD_DIM, 0:HEAD_DIM]
        st_ref[2 * p + 1] = s_scr[p, HEAD_DIM:LANE, HEAD_DIM:LANE]


def _head_block_diag():
    i = jnp.arange(BRANCH_W) // HEAD_DIM
    return (i[:, None] == i[None, :]).astype(BF16)


N_CONV_REFS = (5, 2, 2)
N_RWKV_REFS = (12, 3, 31)


def _conv_rwkv_kernel(x_ref, g_ref, *refs):
    ci, co, cs = N_CONV_REFS
    ri, ro, rs = N_RWKV_REFS
    conv_in, refs = refs[:ci], refs[ci:]
    rwkv_in, refs = refs[:ri], refs[ri:]
    conv_out, refs = refs[:co], refs[co:]
    rwkv_out, refs = refs[:ro], refs[ro:]
    conv_scr, rwkv_scr = refs[:cs], refs[cs:]
    assert len(rwkv_scr) == rs
    tt = x_ref.shape[0]

    @pl.when(pl.program_id(1) == 0)
    def _():
        _conv_init(conv_scr[1], tt)
        _rwkv_init(rwkv_scr[1], rwkv_scr[9])

    h = _rms(x_ref[...], g_ref[...]).astype(BF16)
    conv_chunks, conv_finish = _conv_tile(h, *conv_in, *conv_out, *conv_scr)
    _rwkv_tile(h, conv_chunks, *rwkv_in, *rwkv_out, *rwkv_scr)
    conv_finish()


def _conv_rwkv_call(x, lp):
    b, t, _ = x.shape
    tt = TOK_TILE
    vec = _full((1, BRANCH_W))
    scr = lambda w: pltpu.VMEM((tt, w), F32)
    units = (tt // CHUNK) * N_PAIR
    unit = lambda dt, r=1, c=1: pltpu.VMEM((units, r * LANE, c * LANE), dt)
    tok = pl.BlockSpec((None, tt, BRANCH_W), lambda i, j: (i, j, 0))
    return pl.pallas_call(
        _conv_rwkv_kernel,
        grid=(b, t // tt),
        in_specs=[
            pl.BlockSpec((None, tt, D_MODEL), lambda i, j: (i, j, 0)), _full((1, D_MODEL)),
            _full((D_MODEL, N_CONV_IN)), _full((CONV_W, BRANCH_W)), vec, vec, vec,
            _full((D_MODEL, N_RWKV_IN)), _full((1, SHIFT_W)),
            vec, _full((2 * LORA, BRANCH_W)), vec, _full((2 * LORA, BRANCH_W)), vec, vec,
            vec, vec, vec, _full((BRANCH_W, BRANCH_W)),
        ],
        out_specs=[
            tok,
            pl.BlockSpec((None, CONV_W - 1, BRANCH_W), lambda i, j: (i, 0, 0)),
            tok,
            pl.BlockSpec((None, N_HEAD, HEAD_DIM, HEAD_DIM), lambda i, j: (i, 0, 0, 0)),
            pl.BlockSpec((None, 1, SHIFT_W), lambda i, j: (i, 0, 0)),
        ],
        out_shape=[
            jax.ShapeDtypeStruct((b, t, BRANCH_W), BF16),
            jax.ShapeDtypeStruct((b, CONV_W - 1, BRANCH_W), F32),
            jax.ShapeDtypeStruct((b, t, BRANCH_W), BF16),
            jax.ShapeDtypeStruct((b, N_HEAD, HEAD_DIM, HEAD_DIM), F32),
            jax.ShapeDtypeStruct((b, 1, SHIFT_W), F32),
        ],
        scratch_shapes=[
            scr(N_CONV_IN), pltpu.VMEM((HIST + tt + 8, BRANCH_W), F32),
            scr(N_RWKV_IN), pltpu.VMEM((8 + tt, SHIFT_W), F32),
            scr(BRANCH_W), scr(BRANCH_W), scr(BRANCH_W), scr(BRANCH_W), scr(BRANCH_W), scr(BRANCH_W),
            scr(BRANCH_W), pltpu.VMEM((N_PAIR, LANE, LANE), F32),
            scr(BRANCH_W), unit(BF16, 2), unit(BF16, 2), unit(BF16, 1, 2), *([unit(BF16)] * 5),
            unit(BF16, 2), unit(BF16), unit(BF16, 1, 2), unit(BF16, 2), unit(BF16), unit(BF16),
            *([unit(F32)] * 5), pltpu.VMEM((units, 8, LANE), F32),
        ],
        compiler_params=_params("parallel", "arbitrary"),
        name="conv_rwkv_branches",
    )(x, lp["norm_g"], lp["w_conv"], lp["conv_w"], lp["conv_b"], lp["conv_ln_g"], lp["conv_ln_b"],
      lp["w_rwkv"], lp["shift_mu"], lp["w0"], lp["w_lora_b"], lp["a0"], lp["a_lora_b"],
      lp["k_k"], lp["k_a"], lp["r_k"], lp["gn_g"], lp["gn_b"], lp["head_bd"])


def _merge_kernel(x_ref, yc_ref, ya_ref, yr_ref, pe_ref, g_ref, wm_ref, wb_ref, wo_ref, pg_ref,
                  pp_ref, fg_ref, o_ref, mg_scr, *, att_transposed, final_norm):
    x = x_ref[...]
    mg_scr[...] = _bdot(_rms(x, g_ref[...]), wm_ref[...])
    ya = ya_ref[...].T if att_transposed else ya_ref[...]
    s = (_sigmoid(mg_scr[:, 0:D_MODEL]) * _bdot(yc_ref[...], wb_ref[0])
         + _sigmoid(mg_scr[:, D_MODEL:2 * D_MODEL]) * _bdot(ya, wb_ref[1])
         + _sigmoid(mg_scr[:, 2 * D_MODEL:3 * D_MODEL]) * _bdot(yr_ref[...], wb_ref[2]))
    x = x + _bdot(s, wo_ref[...])
    x = x + _sigmoid(_bdot(x, pg_ref[...])) * _bdot(pe_ref[...], pp_ref[...])
    o_ref[...] = _rms(x, fg_ref[...]) if final_norm else x


def _merge_call(x, y_conv, y_att, y_rw, pe, lp, final_g, *, att_transposed, final_norm):
    b, t, _ = x.shape
    tm = min(TOK_TILE, t)
    tok = lambda i, j: (i, j, 0)
    att_spec = (pl.BlockSpec((None, BRANCH_W, tm), lambda i, j: (i, 0, j)) if att_transposed
                else pl.BlockSpec((None, tm, BRANCH_W), tok))
    return pl.pallas_call(
        functools.partial(_merge_kernel, att_transposed=att_transposed, final_norm=final_norm),
        grid=(b, t // tm),
        in_specs=[
            pl.BlockSpec((None, tm, D_MODEL), tok),
            pl.BlockSpec((None, tm, BRANCH_W), tok),
            att_spec,
            pl.BlockSpec((None, tm, BRANCH_W), tok),
            pl.BlockSpec((None, tm, P_DIM), tok),
            _full((1, D_MODEL)), _full((D_MODEL, N_MERGE_IN)), _full((3, BRANCH_W, D_MODEL)),
            _full((D_MODEL, D_MODEL)), _full((D_MODEL, D_MODEL)), _full((P_DIM, D_MODEL)),
            _full((1, D_MODEL)),
        ],
        out_specs=pl.BlockSpec((None, tm, D_MODEL), tok),
        out_shape=jax.ShapeDtypeStruct((b, t, D_MODEL), F32),
        scratch_shapes=[pltpu.VMEM((tm, N_MERGE_IN), F32)],
        compiler_params=_params("parallel", "parallel"),
        name="merge",
    )(x, y_conv, y_att, y_rw, pe, lp["norm_g"], lp["w_merge"], lp["w_branch"], lp["w_out"],
      lp["ple_gate"], lp["ple_proj"], final_g)


def _proj_kernel(x_ref, g_ref, w_ref, z_ref):
    z_ref[...] = _bdot(_rms(x_ref[...], g_ref[...]), w_ref[...])


def _proj_call(x, norm_g, w_in):
    m = x.shape[0]
    n_tiles = 3
    tn = N_IN // n_tiles
    return pl.pallas_call(
        _proj_kernel,
        grid=(n_tiles,),
        in_specs=[_full((m, D_MODEL)), _full((1, D_MODEL)),
                  pl.BlockSpec((D_MODEL, tn), lambda j: (0, j))],
        out_specs=pl.BlockSpec((m, tn), lambda j: (0, j)),
        out_shape=jax.ShapeDtypeStruct((m, N_IN), F32),
        compiler_params=_params("parallel"),
        name="decode_project",
    )(x, norm_g, w_in)


def _row_to_col(row_vec):
    n = row_vec.shape[1]
    eye = lax.broadcasted_iota(jnp.int32, (n, n), 0) == lax.broadcasted_iota(jnp.int32, (n, n), 1)
    return jnp.sum(jnp.where(eye, row_vec, 0.0), axis=1, keepdims=True)


def _col_to_row(col_vec):
    n = col_vec.shape[0]
    eye = lax.broadcasted_iota(jnp.int32, (n, n), 0) == lax.broadcasted_iota(jnp.int32, (n, n), 1)
    return jnp.sum(jnp.where(eye, col_vec, 0.0), axis=0, keepdims=True)


def _mix_kernel(z_ref, buf_ref, st_ref, sh0_ref, cw_ref, cb_ref, lg_ref, lb_ref, cos_ref, sa_ref,
                sb_ref, mu_ref, w0_ref, wlb_ref, a0_ref, alb_ref, kk_ref, ka_ref, rk_ref, gg_ref,
                gb_ref, bd_ref,
                yc_ref, cs_ref, q_ref, k_ref, v_ref, sg_ref, yr_ref, so_ref, sho_ref, ext_scr):
    o_att = N_CONV_IN
    o_rw = N_CONV_IN + N_ATT_IN
    u = z_ref[:, 0:BRANCH_W] * _sigmoid(z_ref[:, BRANCH_W:2 * BRANCH_W])
    ext_scr[0:8, :] = jnp.zeros((8, BRANCH_W), F32)
    ext_scr[2:HIST, :] = buf_ref[...]
    ext_scr[HIST:HIST + 16, :] = jnp.broadcast_to(u, (16, BRANCH_W))
    yc_ref[...] = _conv_branch(ext_scr, HIST, 8, cw_ref, cb_ref[...], lg_ref[...], lb_ref[...],
                               z_ref[:, 2 * BRANCH_W:3 * BRANCH_W])[0:1, :]
    cs_ref[...] = ext_scr[3:HIST + 1, :]
    n = BRANCH_W // LANE
    cos, sa, sb = (_tile_lanes(r[...], n) for r in (cos_ref, sa_ref, sb_ref))
    q_ref[...] = _rope(z_ref[:, o_att:o_att + BRANCH_W], cos, sa, sb) * (HEAD_DIM ** -0.5)
    k_ref[...] = _rope(z_ref[:, o_att + BRANCH_W:o_att + 2 * BRANCH_W], cos, sa, sb)
    v_ref[...] = z_ref[:, o_att + 2 * BRANCH_W:o_att + 3 * BRANCH_W]
    sg_ref[...] = _silu(z_ref[:, o_att + 3 * BRANCH_W:o_att + 4 * BRANCH_W])
    cur = z_ref[:, o_rw:o_rw + SHIFT_W]
    sho_ref[...] = cur
    mixed = cur + (sh0_ref[...] - cur) * mu_ref[...]
    r, k, v, log_decay, kk, a_lr = _rwkv_columns(
        mixed, w0_ref[...], wlb_ref, a0_ref[...], alb_ref, kk_ref[...], ka_ref[...], bd_ref)
    decay = jnp.exp(log_decay)
    b_vec = kk * a_lr
    ys = []
    for h in range(N_HEAD):
        lanes = slice(h * HEAD_DIM, (h + 1) * HEAD_DIM)
        s = st_ref[h]
        sa_col = jnp.sum(s * (-kk[:, lanes]), axis=1, keepdims=True)
        s = s * decay[:, lanes] + sa_col * b_vec[:, lanes] + _row_to_col(v[:, lanes]) * k[:, lanes]
        so_ref[h] = s
        ys.append(_col_to_row(jnp.sum(s * r[:, lanes], axis=1, keepdims=True)))
    y = jnp.concatenate(ys, axis=1)
    yr_ref[...] = _rwkv_finish(y, r, k, v, rk_ref[...], gg_ref[...], gb_ref[...],
                               z_ref[:, o_rw + SHIFT_W:o_rw + SHIFT_W + BRANCH_W], bd_ref)


def _mix_call(z, buf, state, shift0, lp, tables):
    b = z.shape[0]
    row = lambda w: pl.BlockSpec((None, 1, w), lambda i: (i, 0, 0))
    vec = _full((1, BRANCH_W))
    tab = _full((1, LANE))
    o = lambda w, dt=F32: jax.ShapeDtypeStruct((b, 1, w), dt)
    return pl.pallas_call(
        _mix_kernel,
        grid=(b,),
        in_specs=[
            row(N_IN),
            pl.BlockSpec((None, CONV_W - 1, BRANCH_W), lambda i: (i, 0, 0)),
            pl.BlockSpec((None, N_HEAD, HEAD_DIM, HEAD_DIM), lambda i: (i, 0, 0, 0)),
            row(SHIFT_W),
            _full((CONV_W, BRANCH_W)), vec, vec, vec, tab, tab, tab,
            _full((1, SHIFT_W)), vec, _full((2 * LORA, BRANCH_W)), vec, _full((2 * LORA, BRANCH_W)),
            vec, vec, vec, vec, vec, _full((BRANCH_W, BRANCH_W)),
        ],
        out_specs=[
            row(BRANCH_W),
            pl.BlockSpec((None, CONV_W - 1, BRANCH_W), lambda i: (i, 0, 0)),
            row(BRANCH_W), row(BRANCH_W), row(BRANCH_W), row(BRANCH_W), row(BRANCH_W),
            pl.BlockSpec((None, N_HEAD, HEAD_DIM, HEAD_DIM), lambda i: (i, 0, 0, 0)),
            row(SHIFT_W),
        ],
        out_shape=[
            o(BRANCH_W, BF16), jax.ShapeDtypeStruct((b, CONV_W - 1, BRANCH_W), F32),
            o(BRANCH_W), o(BRANCH_W), o(BRANCH_W), o(BRANCH_W), o(BRANCH_W, BF16),
            jax.ShapeDtypeStruct((b, N_HEAD, HEAD_DIM, HEAD_DIM), F32), o(SHIFT_W),
        ],
        scratch_shapes=[pltpu.VMEM((HIST + 16, BRANCH_W), F32)],
        compiler_params=_params("parallel"),
        name="decode_mix",
    )(z.reshape(b, 1, N_IN), buf, state, shift0.reshape(b, 1, SHIFT_W),
      lp["conv_w"], lp["conv_b"], lp["conv_ln_g"], lp["conv_ln_b"], *tables,
      lp["shift_mu"], lp["w0"], lp["w_lora_b"], lp["a0"], lp["a_lora_b"], lp["k_k"], lp["k_a"],
      lp["r_k"], lp["gn_g"], lp["gn_b"], lp["head_bd"])


def _head_rows(q_row):
    lane = lax.broadcasted_iota(jnp.int32, (N_HEAD, BRANCH_W), 1)
    row = lax.broadcasted_iota(jnp.int32, (N_HEAD, BRANCH_W), 0)
    return jnp.where((lane >= row * HEAD_DIM) & (lane < (row + 1) * HEAD_DIM), q_row, 0.0)


def _score_kernel(pt_ref, q_ref, *refs):
    del pt_ref
    pages = refs[:PAGES_PER_STEP]
    s_ref, gate_ref = refs[PAGES_PER_STEP:]
    j = pl.program_id(1)
    page_rows = pages[0].shape[1]

    @pl.when(j == 0)
    def _():
        gate_ref[...] = jnp.zeros(gate_ref.shape, F32)

    qh = _head_rows(q_ref[...])
    lane = lax.broadcasted_iota(jnp.int32, gate_ref.shape, 1)
    pages_per_block = MOBA_BLOCK // page_rows
    gate = gate_ref[...]
    for r in range(PAGES_PER_STEP):
        s = _bdot(qh, pages[r][...])
        s_ref[:, r * page_rows:(r + 1) * page_rows] = s
        blk = (j * PAGES_PER_STEP + r) // pages_per_block
        gate = gate + jnp.where(lane == blk, jnp.sum(s, axis=1, keepdims=True), 0.0)
    gate_ref[...] = gate


def _score_call(page_table, q, cache, layer, n_pool):
    b, n_pages = page_table.shape
    page_rows = cache.shape[2]
    past = n_pages * page_rows
    assert n_pages % PAGES_PER_STEP == 0 and MOBA_BLOCK % page_rows == 0
    assert past // MOBA_BLOCK <= LANE

    def page_spec(r):
        return pl.BlockSpec(
            (None, BRANCH_W, page_rows),
            lambda i, j, pt: (layer * n_pool + pt[i, j * PAGES_PER_STEP + r], 0, 0))

    grid_spec = pltpu.PrefetchScalarGridSpec(
        num_scalar_prefetch=1,
        grid=(b, n_pages // PAGES_PER_STEP),
        in_specs=[pl.BlockSpec((None, 1, BRANCH_W), lambda i, j, pt: (i, 0, 0))]
        + [page_spec(r) for r in range(PAGES_PER_STEP)],
        out_specs=[
            pl.BlockSpec((None, N_HEAD, PAGES_PER_STEP * page_rows), lambda i, j, pt: (i, 0, j)),
            pl.BlockSpec((None, N_HEAD, LANE), lambda i, j, pt: (i, 0, 0)),
        ],
    )
    return pl.pallas_call(
        _score_kernel,
        grid_spec=grid_spec,
        out_shape=[jax.ShapeDtypeStruct((b, N_HEAD, past), F32),
                   jax.ShapeDtypeStruct((b, N_HEAD, LANE), F32)],
        compiler_params=_params("parallel", "arbitrary"),
        name="decode_scores",
    )(page_table, q, *([cache] * PAGES_PER_STEP))


def _select_kernel(s_ref, gate_ref, q_ref, k_ref, p_ref, idx_ref, pown_ref, *, n_blocks):
    lane = lax.broadcasted_iota(jnp.int32, gate_ref.shape, 1).astype(F32)
    gate = jnp.where(lane < n_blocks, gate_ref[...], -jnp.inf)
    key_blk = jnp.right_shift(lax.broadcasted_iota(jnp.int32, s_ref.shape, 1),
                              MOBA_BLOCK.bit_length() - 1).astype(F32)
    sel = jnp.zeros(s_ref.shape, F32)
    idx_out = jnp.zeros(gate_ref.shape, F32)
    for j in range(MOBA_TOPK):
        m = jnp.max(gate, axis=1, keepdims=True)
        idx = jnp.min(jnp.where(gate == m, lane, float(LANE)), axis=1, keepdims=True)
        sel = jnp.where(key_blk == idx, 1.0, sel)
        idx_out = jnp.where(lane == j, idx, idx_out)
        gate = jnp.where(lane == idx, -jnp.inf, gate)
    s_own = jnp.sum(_head_rows(q_ref[...]) * k_ref[...], axis=1, keepdims=True)
    s = jnp.where(sel > 0.0, s_ref[...], NEG)
    m = jnp.maximum(jnp.max(s, axis=1, keepdims=True), s_own)
    e = jnp.exp(s - m)
    e_own = jnp.exp(s_own - m)
    l = jnp.sum(e, axis=1, keepdims=True) + e_own
    p_ref[...] = e / l
    pown_ref[...] = jnp.broadcast_to(e_own / l, pown_ref.shape)
    idx_ref[...] = idx_out.astype(jnp.int32)


def _select_call(scores, gate, q, k_new):
    b, _, past = scores.shape
    n_blocks = past // MOBA_BLOCK
    assert n_blocks >= MOBA_TOPK
    head = lambda w: pl.BlockSpec((None, N_HEAD, w), lambda i: (i, 0, 0))
    row = pl.BlockSpec((None, 1, BRANCH_W), lambda i: (i, 0, 0))
    return pl.pallas_call(
        functools.partial(_select_kernel, n_blocks=n_blocks),
        grid=(b,),
        in_specs=[head(past), head(LANE), row, row],
        out_specs=[head(past), head(LANE), head(LANE)],
        out_shape=[jax.ShapeDtypeStruct((b, N_HEAD, past), F32),
                   jax.ShapeDtypeStruct((b, N_HEAD, LANE), jnp.int32),
                   jax.ShapeDtypeStruct((b, N_HEAD, LANE), F32)],
        compiler_params=_params("parallel"),
        name="decode_select",
    )(scores, gate, q, k_new)


def _gather_kernel(pt_ref, ix_ref, pown_ref, vnew_ref, sg_ref, *refs, n_sel):
    del pt_ref, ix_ref
    n = GATHER_HEADS * n_sel
    p_rows = refs[:n]
    v_pages = refs[n:2 * n]
    y_ref = refs[2 * n]
    for hh in range(GATHER_HEADS):
        h = pl.program_id(1) * GATHER_HEADS + hh
        acc = pown_ref[pl.ds(h, 1), 0:1] * vnew_ref[hh]
        for r in range(hh * n_sel, (hh + 1) * n_sel):
            p8 = jnp.broadcast_to(p_rows[r][pl.ds(h, 1), :], (8, p_rows[r].shape[1]))
            acc = acc + _bdot_nt(p8, v_pages[r][...])[0:1, :]
        y_ref[hh] = acc * sg_ref[hh]


def _gather_call(page_table, idx, probs, p_own, v_new, sg, cache, layer, n_pool):
    b, n_pages = page_table.shape
    page_rows = cache.shape[2]
    ppb = MOBA_BLOCK // page_rows
    n_sel = MOBA_TOPK * ppb

    def seq_page(i, h, ix, r):
        return ix[i, h * MOBA_TOPK + r // ppb] * ppb + r % ppb

    def p_spec(hh, r):
        return pl.BlockSpec(
            (None, N_HEAD, page_rows),
            lambda i, g, pt, ix: (i, 0, seq_page(i, g * GATHER_HEADS + hh, ix, r)))

    def v_spec(hh, r):
        def index(i, g, pt, ix):
            h = g * GATHER_HEADS + hh
            return layer * n_pool + pt[i, seq_page(i, h, ix, r)], h, 0
        return pl.BlockSpec((None, HEAD_DIM, page_rows), index)

    by_head = lambda a: a.reshape(b, N_HEAD, 1, HEAD_DIM)
    rows = pl.BlockSpec((None, GATHER_HEADS, 1, HEAD_DIM), lambda i, g, pt, ix: (i, g, 0, 0))
    slots = [(hh, r) for hh in range(GATHER_HEADS) for r in range(n_sel)]
    grid_spec = pltpu.PrefetchScalarGridSpec(
        num_scalar_prefetch=2,
        grid=(b, N_HEAD // GATHER_HEADS),
        in_specs=[pl.BlockSpec((None, N_HEAD, LANE), lambda i, g, pt, ix: (i, 0, 0)), rows, rows]
        + [p_spec(*s) for s in slots] + [v_spec(*s) for s in slots],
        out_specs=rows,
    )
    return pl.pallas_call(
        functools.partial(_gather_kernel, n_sel=n_sel),
        grid_spec=grid_spec,
        out_shape=jax.ShapeDtypeStruct((b, N_HEAD, 1, HEAD_DIM), F32),
        compiler_params=_params("parallel", "parallel"),
        name="decode_gather",
    )(page_table, idx, p_own, by_head(v_new), by_head(sg),
      *([probs] * len(slots)), *([cache] * len(slots)))


def _layer_params(i, norm_g, w_in, conv_w, conv_b, conv_ln_g, conv_ln_b, shift_mu, w0, w_lora_b, a0,
                  a_lora_b, k_k, k_a, r_k, gn_g, gn_b, w_branch, w_out, ple_proj, ple_gate):
    w = w_in[i].astype(BF16)
    o1, o2, o3 = N_CONV_IN, N_CONV_IN + N_ATT_IN, N_CONV_IN + N_ATT_IN + N_RWKV_IN
    row = lambda a: a[i].reshape(1, -1)
    zeros = jnp.zeros((LORA, BRANCH_W), F32)
    return dict(
        norm_g=row(norm_g), w_in=w, w_conv=w[:, :o1], w_att=w[:, o1:o2], w_rwkv=w[:, o2:o3],
        w_merge=w[:, o3:],
        conv_w=conv_w[i], conv_b=row(conv_b), conv_ln_g=row(conv_ln_g), conv_ln_b=row(conv_ln_b),
        shift_mu=row(shift_mu), w0=row(w0), a0=row(a0),
        w_lora_b=jnp.concatenate([w_lora_b[i], zeros], axis=0).astype(BF16),
        a_lora_b=jnp.concatenate([zeros, a_lora_b[i]], axis=0).astype(BF16),
        k_k=row(k_k), k_a=row(k_a), r_k=row(r_k), gn_g=row(gn_g), gn_b=row(gn_b),
        w_branch=w_branch[i].astype(BF16), w_out=w_out[i].astype(BF16),
        ple_proj=ple_proj[i].astype(BF16), ple_gate=ple_gate[i].astype(BF16),
        head_bd=_head_block_diag(),
    )


def _prompt_layer(x, pe, lp, tables, final_g, final_norm, layer, depth, kv_t):
    b, t, _ = x.shape
    nb = t // MOBA_BLOCK
    y_conv, conv_new, y_rw, wkv, shift = _conv_rwkv_call(x, lp)
    k_t, v_t, qt, kb, vt, sgt, km = _att1_call(x, lp["norm_g"], lp["w_att"], tables, layer, depth, kv_t)
    km = km.reshape(b, nb, N_HEAD, HEAD_DIM).transpose(0, 2, 1, 3)
    y_att_t = _att2_call(qt, kb, vt, km, sgt)
    x = _merge_call(x, y_conv, y_att_t, y_rw, pe, lp, final_g,
                    att_transposed=True, final_norm=final_norm)
    return x, (k_t, v_t), conv_new, wkv, shift.reshape(b, SHIFT_W)


def _sample_layer(x, pe, lp, tables, final_g, final_norm, layer, n_pool, cache_k, cache_v,
                  page_table, buf, state, shift0):
    b = x.shape[0]
    z = _proj_call(x.reshape(b, D_MODEL), lp["norm_g"], lp["w_in"])
    y_conv, conv_new, q, k, v, sg, y_rw, wkv, shift = _mix_call(z, buf, state, shift0, lp, tables)
    scores, gate = _score_call(page_table, q, cache_k, layer, n_pool)
    probs, idx, p_own = _select_call(scores, gate, q, k)
    idx = idx[:, :, :MOBA_TOPK].reshape(b, N_HEAD * MOBA_TOPK)
    y_att = _gather_call(page_table, idx, probs, p_own, v, sg, cache_v, layer, n_pool)
    tok = lambda a: a.reshape(1, b, -1)
    x = _merge_call(tok(x), tok(y_conv), tok(y_att), tok(y_rw), tok(pe), lp, final_g,
                    att_transposed=False, final_norm=final_norm)
    return (x.reshape(b, 1, D_MODEL), k, v, conv_new, wkv, shift.reshape(b, SHIFT_W))


def kernel(x_prompt, x_sample, cache_k, cache_v, page_table, state_conv, state_wkv, state_shift,
           p_prompt, p_sample, norm_g, w_in, conv_w, conv_b, conv_ln_g, conv_ln_b, shift_mu, w0,
           w_lora_b, a0, a_lora_b, k_k, k_a, r_k, gn_g, gn_b, w_branch, w_out, ple_proj, ple_gate,
           final_norm_g):
    depth = w_in.shape[0]
    b_p, t_p, _ = x_prompt.shape
    b_s, t_s, _ = x_sample.shape
    assert t_s == 1 and t_p % TOK_TILE == 0 and TOK_TILE == MOBA_BLOCK
    n_pool, page_rows = cache_k.shape[1], cache_k.shape[2]
    past_len = page_table.shape[1] * page_rows
    assert past_len % MOBA_BLOCK == 0
    tables_p = _rope_tables(jnp.arange(t_p, dtype=jnp.int32))
    tables_s = _rope_tables(past_len + jnp.arange(1, dtype=jnp.int32))
    by_page = lambda c: c.transpose(0, 1, 3, 4, 2).reshape(depth * n_pool, BRANCH_W, page_rows)
    cache_k, cache_v = by_page(cache_k), by_page(cache_v)
    final_g = final_norm_g.reshape(1, D_MODEL)
    xp, xs = x_prompt, x_sample
    outs = [[] for _ in range(8)]
    kv_t = None
    for i in range(depth):
        lp = _layer_params(i, norm_g, w_in, conv_w, conv_b, conv_ln_g, conv_ln_b, shift_mu, w0,
                           w_lora_b, a0, a_lora_b, k_k, k_a, r_k, gn_g, gn_b, w_branch, w_out,
                           ple_proj, ple_gate)
        last = i == depth - 1
        xp, kv_t, cp, wp, sp = _prompt_layer(xp, p_prompt[i], lp, tables_p, final_g, last, i,
                                             depth, kv_t)
        xs, ks, vs, cs, ws, ss = _sample_layer(
            xs, p_sample[i], lp, tables_s, final_g, last, i, n_pool, cache_k, cache_v, page_table,
            state_conv[i], state_wkv[i], state_shift[i])
        heads_s = lambda a: a.reshape(b_s, 1, N_HEAD, HEAD_DIM)
        for lst, a in zip(outs, (heads_s(ks), heads_s(vs), cp, cs, wp, ws, sp, ss)):
            lst.append(a)
    heads_p = lambda a: a.reshape(depth, b_p, N_HEAD, HEAD_DIM, t_p).transpose(0, 1, 4, 2, 3)
    return (xp, xs, heads_p(kv_t[0]), heads_p(kv_t[1])) + tuple(jnp.stack(lst) for lst in outs)
```

```python
import functools

import jax
import jax.numpy as jnp
from jax import lax
from jax.experimental import pallas as pl
from jax.experimental.pallas import tpu as pltpu

D_MODEL = 1024
P_DIM = 256
HEAD_DIM = 64
BRANCH_W = 512
N_HEAD = 8
N_PAIR = 4
CONV_W = 31
ROT_DIM = 16
ROPE_THETA = 500000.0
MOBA_BLOCK = 256
MOBA_TOPK = 3
LORA = 64
SHIFT_W = 3 * BRANCH_W + 2 * LORA
N_CONV_IN = 3 * BRANCH_W
N_ATT_IN = 4 * BRANCH_W
N_RWKV_IN = SHIFT_W + BRANCH_W
N_MERGE_IN = 3 * D_MODEL
N_IN = N_CONV_IN + N_ATT_IN + N_RWKV_IN + N_MERGE_IN
NORM_EPS = 1e-6
LN_EPS = 1e-5
GN_EPS = 64e-5
NEG = -1e30
LOG2_E = 1.4426950408889634
DECAY_SCALE = 0.6065306597126334

LANE = 128
TOK_TILE = 256
V_ROWS = 80
BIAS_ROWS = 16
CHUNK = 64
HIST = 32
CONV_ROWS = 64
GATHER_HEADS = 8
PAGES_PER_STEP = 64
VMEM_LIMIT = 56 * 1024 * 1024

F32 = jnp.float32
BF16 = jnp.bfloat16


def _bdot(a, b):
    return jnp.dot(a.astype(BF16), b.astype(BF16), preferred_element_type=F32)


def _bdot_nt(a, b):
    return lax.dot_general(a.astype(BF16), b.astype(BF16), (((1,), (1,)), ((), ())),
                           preferred_element_type=F32)


def _split_dot(x, m_bf16):
    hi = x.astype(BF16)
    lo = (x - hi.astype(F32)).astype(BF16)
    return (jnp.dot(hi, m_bf16, preferred_element_type=F32)
            + jnp.dot(lo, m_bf16, preferred_element_type=F32))


def _split3_dot_left(m_bf16, x):
    hi = x.astype(BF16)
    r1 = x - hi.astype(F32)
    mid = r1.astype(BF16)
    lo = (r1 - mid.astype(F32)).astype(BF16)
    return (jnp.dot(m_bf16, hi, preferred_element_type=F32)
            + jnp.dot(m_bf16, mid, preferred_element_type=F32)
            + jnp.dot(m_bf16, lo, preferred_element_type=F32))


def _sigmoid(x):
    return 1.0 / (1.0 + jnp.exp(-x))


def _silu(x):
    return x * _sigmoid(x)


def _rms(x, g):
    return x * lax.rsqrt(jnp.mean(x * x, axis=-1, keepdims=True) + NORM_EPS) * g


def _params(*sem):
    return pltpu.CompilerParams(dimension_semantics=sem, vmem_limit_bytes=VMEM_LIMIT)


def _full(shape):
    nd = len(shape)
    return pl.BlockSpec(shape, lambda *_: (0,) * nd)


def _conv_branch(ext_ref, base, rows, cw_ref, cb, lg, lb, g_conv):
    acc = jnp.zeros((rows, BRANCH_W), F32) + cb
    first = base - (CONV_W - 1)
    for s in range(8):
        z = None
        for j in range(CONV_W):
            if (first + j) % 8 == s:
                start = first + j - s
                term = ext_ref[start:start + rows + 8, :] * cw_ref[j:j + 1, :]
                z = term if z is None else z + term
        acc = acc + z[s:s + rows, :]
    mu = jnp.mean(acc, axis=-1, keepdims=True)
    d = acc - mu
    var = jnp.mean(d * d, axis=-1, keepdims=True)
    y = d * lax.rsqrt(var + LN_EPS) * lg + lb
    return (_silu(y) * _silu(g_conv)).astype(BF16)


def _conv_init(ext_scr, tt):
    ext_scr[0:HIST, :] = jnp.zeros((HIST, BRANCH_W), F32)
    ext_scr[HIST + tt:HIST + tt + 8, :] = jnp.zeros((8, BRANCH_W), F32)


def _conv_tile(h, w_ref, cw_ref, cb_ref, lg_ref, lb_ref, y_ref, cs_ref, z_scr, ext_scr):
    tt = h.shape[0]
    z_scr[...] = jnp.dot(h, w_ref[...], preferred_element_type=F32)
    ext_scr[HIST:HIST + tt, :] = z_scr[:, 0:BRANCH_W] * _sigmoid(z_scr[:, BRANCH_W:2 * BRANCH_W])

    def rows(r):
        y_ref[r:r + CONV_ROWS, :] = _conv_branch(
            ext_scr, HIST + r, CONV_ROWS, cw_ref, cb_ref[...], lg_ref[...], lb_ref[...],
            z_scr[r:r + CONV_ROWS, 2 * BRANCH_W:3 * BRANCH_W])

    def finish():
        cs_ref[...] = ext_scr[HIST + tt - (CONV_W - 1):HIST + tt, :]
        ext_scr[0:HIST, :] = ext_scr[tt:tt + HIST, :]

    return [functools.partial(rows, r) for r in range(0, tt, CONV_ROWS)], finish


def _rope_tables(pos):
    half = ROT_DIM // 2
    inv = jnp.power(ROPE_THETA, -jnp.arange(half, dtype=F32) * (2.0 / ROT_DIM))
    ang = pos.astype(F32)[:, None] * inv[None, :]
    cos, sin = jnp.cos(ang), jnp.sin(ang)
    n = pos.shape[0]
    pad = jnp.zeros((n, HEAD_DIM - ROT_DIM), F32)
    cos64 = jnp.concatenate([cos, cos, pad + 1.0], axis=1)
    sa64 = jnp.concatenate([-sin, jnp.zeros_like(sin), pad], axis=1)
    sb64 = jnp.concatenate([jnp.zeros_like(sin), sin, pad], axis=1)
    return tuple(jnp.concatenate([a, a], axis=1) for a in (cos64, sa64, sb64))


def _rope(a, cos, sa, sb):
    w = a.shape[1]
    return a * cos + pltpu.roll(a, w - ROT_DIM // 2, 1) * sa + pltpu.roll(a, ROT_DIM // 2, 1) * sb


def _tile_lanes(a, n):
    return jnp.concatenate([a] * n, axis=1)


def _att1_kernel(x_ref, g_ref, w_ref, cos_ref, sa_ref, sb_ref, *refs):
    k_ref, v_ref, qt_ref, kb_ref, vt_ref, sgt_ref, km_ref, z_scr = refs[-8:]
    z_scr[...] = _bdot(_rms(x_ref[...], g_ref[...]), w_ref[...])
    n = BRANCH_W // LANE
    cos, sa, sb = (_tile_lanes(r[...], n) for r in (cos_ref, sa_ref, sb_ref))
    q = _rope(z_scr[:, 0:BRANCH_W], cos, sa, sb) * (LOG2_E * HEAD_DIM ** -0.5)
    k = _rope(z_scr[:, BRANCH_W:2 * BRANCH_W], cos, sa, sb)
    v = z_scr[:, 2 * BRANCH_W:3 * BRANCH_W]
    v_t = v.T
    k_ref[...] = k.T
    v_ref[...] = v_t
    qt_ref[...] = q.T.astype(BF16)
    lane = lax.broadcasted_iota(jnp.int32, (k.shape[0], LANE), 1)
    one_hot = jnp.where(lane == HEAD_DIM, 1.0, 0.0)
    for p in range(N_PAIR):
        slab = k[:, p * LANE:(p + 1) * LANE]
        kb_ref[2 * p] = jnp.where(lane < HEAD_DIM, slab, one_hot).astype(BF16)
        kb_ref[2 * p + 1] = jnp.where(lane < HEAD_DIM, pltpu.roll(slab, HEAD_DIM, 1),
                                      one_hot).astype(BF16)
    tail = jnp.where(lax.broadcasted_iota(jnp.int32, (V_ROWS - HEAD_DIM, v_t.shape[1]), 0) == 0,
                     1.0, 0.0)
    for h in range(N_HEAD):
        vt_ref[h * V_ROWS:(h + 1) * V_ROWS, :] = jnp.concatenate(
            [v_t[h * HEAD_DIM:(h + 1) * HEAD_DIM, :], tail], axis=0).astype(BF16)
    sgt_ref[...] = _silu(z_scr[:, 3 * BRANCH_W:4 * BRANCH_W]).T
    km_ref[...] = jnp.mean(k, axis=0, keepdims=True)


def _att1_call(x, norm_g, w_a, tables, layer, depth, kv_t):
    b, t, _ = x.shape
    tt = TOK_TILE
    nb = t // tt
    tok = lambda i, j: (i, j, 0)
    tr = lambda i, j: (i, 0, j)
    kv_spec = pl.BlockSpec((None, None, BRANCH_W, tt), lambda i, j: (layer, i, 0, j))
    kv_shape = jax.ShapeDtypeStruct((depth, b, BRANCH_W, t), F32)
    carried = [] if kv_t is None else [pl.BlockSpec(memory_space=pl.ANY)] * 2
    return pl.pallas_call(
        _att1_kernel,
        grid=(b, nb),
        in_specs=[
            pl.BlockSpec((None, tt, D_MODEL), tok),
            _full((1, D_MODEL)), _full((D_MODEL, N_ATT_IN)),
            pl.BlockSpec((tt, LANE), lambda i, j: (j, 0)),
            pl.BlockSpec((tt, LANE), lambda i, j: (j, 0)),
            pl.BlockSpec((tt, LANE), lambda i, j: (j, 0)),
        ] + carried,
        input_output_aliases={} if kv_t is None else {6: 0, 7: 1},
        out_specs=[
            kv_spec,
            kv_spec,
            pl.BlockSpec((None, BRANCH_W, tt), tr),
            pl.BlockSpec((None, N_HEAD, tt, LANE), lambda i, j: (i, 0, j, 0)),
            pl.BlockSpec((None, None, N_HEAD * V_ROWS, tt), lambda i, j: (i, j, 0, 0)),
            pl.BlockSpec((None, BRANCH_W, tt), tr),
            pl.BlockSpec((None, None, 1, BRANCH_W), lambda i, j: (i, j, 0, 0)),
        ],
        out_shape=[
            kv_shape,
            kv_shape,
            jax.ShapeDtypeStruct((b, BRANCH_W, t), BF16),
            jax.ShapeDtypeStruct((b, N_HEAD, t, LANE), BF16),
            jax.ShapeDtypeStruct((b, nb, N_HEAD * V_ROWS, tt), BF16),
            jax.ShapeDtypeStruct((b, BRANCH_W, t), F32),
            jax.ShapeDtypeStruct((b, nb, 1, BRANCH_W), F32),
        ],
        scratch_shapes=[pltpu.VMEM((tt, N_ATT_IN), F32)],
        compiler_params=_params("parallel", "parallel"),
        name="att_project",
    )(x, norm_g, w_a, *tables, *(kv_t or ()))


def _top_blocks(gate, n_slots):
    nb = gate.shape[0]
    row = lax.broadcasted_iota(jnp.int32, gate.shape, 0).astype(F32)
    sel = jnp.zeros(gate.shape, F32)
    for j in range(MOBA_TOPK):
        m = jnp.max(gate, axis=0, keepdims=True)
        idx = jnp.min(jnp.where(gate == m, row, float(nb)), axis=0, keepdims=True)
        hit = row == jnp.where(j < n_slots, idx, -1.0)
        sel = jnp.where(hit, 1.0, sel)
        gate = jnp.where(row == idx, -jnp.inf, gate)
    return sel


def _att2_kernel(qt_ref, kb_ref, vt_ref, km_ref, sgt_ref, yt_ref,
                 qp_scr, bias_scr, s_scr, e_scr, alpha_scr, m_scr, acc_scr):
    i = pl.program_id(1)
    tq = qt_ref.shape[1]
    nb = km_ref.shape[1]
    blk = lax.broadcasted_iota(jnp.int32, (nb, tq), 0)
    qp_scr[...] = jnp.zeros(qp_scr.shape, BF16)
    for h in range(N_HEAD):
        q_h = qt_ref[h * HEAD_DIM:(h + 1) * HEAD_DIM, :]
        qp_scr[h, 0:HEAD_DIM, :] = q_h
        gate = jnp.dot(km_ref[h].astype(BF16), q_h, preferred_element_type=F32)
        sel = _top_blocks(jnp.where(blk < i, gate, -jnp.inf), i)
        bias_scr[h] = jnp.where(sel > 0.0, 0.0, NEG)
    m_scr[...] = jnp.full(m_scr.shape, NEG, F32)
    acc_scr[...] = jnp.zeros(acc_scr.shape, F32)
    bias_row0 = lax.broadcasted_iota(jnp.int32, (BIAS_ROWS, tq), 0) == 0

    def scores(n, slot, bias_of, extra=None):
        start = pl.multiple_of(n * MOBA_BLOCK, MOBA_BLOCK)
        for h in range(N_HEAD):
            qp_scr[h, HEAD_DIM:HEAD_DIM + BIAS_ROWS, :] = jnp.where(
                bias_row0, bias_of(h), 0.0).astype(BF16)
            s = jnp.dot(kb_ref[h, pl.ds(start, MOBA_BLOCK), :], qp_scr[h],
                        preferred_element_type=F32)
            s_scr[slot, h] = s if extra is None else s + extra

    def softmax(slot):
        for h in range(N_HEAD):
            m = m_scr[h:h + 1, :]
            m_new = jnp.maximum(m, jnp.max(s_scr[slot, h], axis=0, keepdims=True))
            m_scr[h:h + 1, :] = m_new
            alpha_scr[slot, h:h + 1, :] = jnp.exp2(m - m_new)
            e_scr[slot, h] = jnp.exp2(s_scr[slot, h] - m_new).astype(BF16)

    def values(n, slot):
        for h in range(N_HEAD):
            pv = jnp.dot(vt_ref[n, h * V_ROWS:(h + 1) * V_ROWS, :], e_scr[slot, h],
                         preferred_element_type=F32)
            acc_scr[h] = alpha_scr[slot, h:h + 1, :] * acc_scr[h] + pv

    def row_bias(n):
        return lambda h: bias_scr[h, pl.ds(n, 1), :]

    def block_of(v):
        return jnp.where(v == 0, i, jnp.minimum(v - 1, nb - 1))

    def step(t, slot):
        nxt = block_of(t + 1)
        scores(nxt, 1 - slot, row_bias(nxt))
        values(block_of(t - 1), 1 - slot)
        softmax(slot)

    causal = jnp.where(lax.broadcasted_iota(jnp.int32, (MOBA_BLOCK, tq), 0)
                       <= lax.broadcasted_iota(jnp.int32, (MOBA_BLOCK, tq), 1), 0.0, NEG)
    scores(i, 0, lambda h: jnp.zeros((1, tq), F32), extra=causal)
    softmax(0)
    scores(0, 1, row_bias(0))

    def body(j, carry):
        step(2 * j + 1, 1)
        step(2 * j + 2, 0)
        return carry

    trips = (i + 1) // 2
    lax.fori_loop(0, trips, body, 0)
    values(block_of(2 * trips), 0)
    for h in range(N_HEAD):
        rows = slice(h * HEAD_DIM, (h + 1) * HEAD_DIM)
        denom = acc_scr[h, HEAD_DIM:HEAD_DIM + 1, :]
        yt_ref[rows, :] = acc_scr[h, 0:HEAD_DIM, :] / denom * sgt_ref[rows, :]


def _att2_call(qt, kb, vt, km, sgt):
    b, _, t = qt.shape
    nb = vt.shape[1]
    tq = MOBA_BLOCK
    return pl.pallas_call(
        _att2_kernel,
        grid=(b, t // tq),
        in_specs=[
            pl.BlockSpec((None, BRANCH_W, tq), lambda bi, i: (bi, 0, i)),
            pl.BlockSpec((None, N_HEAD, t, LANE), lambda bi, i: (bi, 0, 0, 0),
                         pipeline_mode=pl.Buffered(1)),
            pl.BlockSpec((None, nb, N_HEAD * V_ROWS, MOBA_BLOCK), lambda bi, i: (bi, 0, 0, 0),
                         pipeline_mode=pl.Buffered(1)),
            pl.BlockSpec((None, N_HEAD, nb, HEAD_DIM), lambda bi, i: (bi, 0, 0, 0)),
            pl.BlockSpec((None, BRANCH_W, tq), lambda bi, i: (bi, 0, i)),
        ],
        out_specs=pl.BlockSpec((None, BRANCH_W, tq), lambda bi, i: (bi, 0, i)),
        out_shape=jax.ShapeDtypeStruct((b, BRANCH_W, t), F32),
        scratch_shapes=[
            pltpu.VMEM((N_HEAD, LANE, tq), BF16), pltpu.VMEM((N_HEAD, nb, tq), F32),
            pltpu.VMEM((2, N_HEAD, MOBA_BLOCK, tq), F32), pltpu.VMEM((2, N_HEAD, MOBA_BLOCK, tq), BF16),
            pltpu.VMEM((2, N_HEAD, tq), F32), pltpu.VMEM((N_HEAD, tq), F32),
            pltpu.VMEM((N_HEAD, V_ROWS, tq), F32),
        ],
        compiler_params=_params("parallel", "arbitrary"),
        name="moba_attention",
    )(qt, kb, vt, km, sgt)


def _rwkv_columns(mixed, w0, wlb_ref, a0, alb_ref, kk_scale, k_a, bd_ref):
    r = mixed[:, 0:BRANCH_W]
    k = mixed[:, BRANCH_W:2 * BRANCH_W]
    v = mixed[:, 2 * BRANCH_W:3 * BRANCH_W]
    lora = mixed[:, 3 * BRANCH_W:3 * BRANCH_W + 2 * LORA]
    lane = lax.broadcasted_iota(jnp.int32, lora.shape, 1)
    lora = jnp.where(lane < LORA, jnp.tanh(lora), lora)
    log_decay = -(DECAY_SCALE * _sigmoid(w0 + _bdot(lora, wlb_ref[...])))
    a_lr = _sigmoid(a0 + _bdot(lora, alb_ref[...]))
    kk = k * kk_scale
    kk = kk * lax.rsqrt(jnp.maximum(_bdot(kk * kk, bd_ref[...]), 1e-24))
    k = k * (1.0 + (a_lr - 1.0) * k_a)
    return r, k, v, log_decay, kk, a_lr


def _rwkv_finish(y, r, k, v, r_k, gn_g, gn_b, g_rwkv, bd_ref):
    m = _split_dot(y, bd_ref[...]) * (1.0 / HEAD_DIM)
    d = y - m
    var = _bdot(d * d, bd_ref[...]) * (1.0 / HEAD_DIM)
    yn = d * lax.rsqrt(var + GN_EPS) * gn_g + gn_b
    bonus = _bdot(r * k * r_k, bd_ref[...]) * v
    return ((yn + bonus) * _silu(g_rwkv)).astype(BF16)


def _stack_heads(a, mask_a):
    return jnp.concatenate([jnp.where(mask_a, a, 0.0), jnp.where(mask_a, 0.0, a)], axis=0)


def _rwkv_init(shs_scr, s_scr):
    shs_scr[0:8, :] = jnp.zeros((8, SHIFT_W), F32)
    s_scr[...] = jnp.zeros(s_scr.shape, F32)


def _rwkv_tile(h, fillers, w_ref, mu_ref, w0_ref, wlb_ref, a0_ref, alb_ref, kk_ref, ka_ref,
               rk_ref, gg_ref, gb_ref, bd_ref,
               y_ref, st_ref, sh_ref,
               z_scr, shs_scr, r_scr, k_scr, v_scr, ld_scr, a_scr, b_scr, y_scr, s_scr,
               cum_scr, ar_scr, bk_scr, at_scr, vs_scr, vst_scr, bend_scr, kend_scr, pw_scr, akrk_scr,
               arb_scr, wu_scr, wut_scr, g_scr, q_scr, x_scr, yv_scr, kv_scr, h_scr, y0_scr, pt_scr):
    tt = h.shape[0]
    fillers = list(fillers)

    def fill(n=1):
        for _ in range(min(n, len(fillers))):
            fillers.pop(0)()

    z_scr[...] = jnp.dot(h, w_ref[...], preferred_element_type=F32)
    fill(2)
    shs_scr[8:8 + tt, :] = z_scr[:, 0:SHIFT_W]
    cur = z_scr[:, 0:SHIFT_W]
    mixed = cur + (shs_scr[7:7 + tt, :] - cur) * mu_ref[...]
    sh_ref[...] = shs_scr[7 + tt:8 + tt, :]
    shs_scr[7:8, :] = shs_scr[7 + tt:8 + tt, :]

    r, k, v, log_decay, kk, a_lr = _rwkv_columns(
        mixed, w0_ref[...], wlb_ref, a0_ref[...], alb_ref, kk_ref[...], ka_ref[...], bd_ref)
    r_scr[...] = r
    k_scr[...] = k
    v_scr[...] = v
    ld_scr[...] = log_decay
    a_scr[...] = -kk
    b_scr[...] = kk * a_lr

    n2 = 2 * CHUNK
    row = lax.broadcasted_iota(jnp.int32, (n2, n2), 0)
    col = lax.broadcasted_iota(jnp.int32, (n2, n2), 1)
    same = (row >= CHUNK) == (col >= CHUNK)
    rt = jnp.where(row >= CHUNK, row - CHUNK, row)
    ct = jnp.where(col >= CHUNK, col - CHUNK, col)
    strict = same & (rt > ct)
    incl = same & (rt >= ct)
    crow = lax.broadcasted_iota(jnp.int32, (CHUNK, CHUNK), 0)
    ccol = lax.broadcasted_iota(jnp.int32, (CHUNK, CHUNK), 1)
    tri = jnp.where(crow >= ccol, 1.0, 0.0).astype(BF16)
    mask_a = lax.broadcasted_iota(jnp.int32, (CHUNK, LANE), 1) < HEAD_DIM

    n_chunk = tt // CHUNK
    units = [(c, p) for c in range(n_chunk) for p in range(N_PAIR)]
    n_unit = range(len(units))
    dot = functools.partial(jnp.dot, preferred_element_type=F32)

    def nt(a, b):
        return lax.dot_general(a, b, (((1,), (1,)), ((), ())), preferred_element_type=F32)

    def blk(ref, c, p):
        return ref[c * CHUNK:(c + 1) * CHUNK, p * LANE:(p + 1) * LANE]

    for c in range(n_chunk):
        rows = slice(c * CHUNK, (c + 1) * CHUNK)
        cum_scr[rows, :] = _split3_dot_left(tri, ld_scr[rows, :])
    for u, (c, p) in enumerate(units):
        cum = blk(cum_scr, c, p)
        last = cum[CHUNK - 1:CHUNK, :]
        p_inv = jnp.exp(-cum)
        p_end = jnp.exp(last - cum)
        bb = blk(b_scr, c, p)
        kc = blk(k_scr, c, p)
        v_s = _stack_heads(blk(v_scr, c, p), mask_a)
        a_s = _stack_heads(blk(a_scr, c, p) * jnp.exp(cum - blk(ld_scr, c, p)),
                           mask_a).astype(BF16)
        ar_scr[u, 0:n2, :] = a_s
        ar_scr[u, n2:2 * n2, :] = _stack_heads(blk(r_scr, c, p) * jnp.exp(cum), mask_a).astype(BF16)
        bk_scr[u, 0:n2, :] = _stack_heads(bb * p_inv, mask_a).astype(BF16)
        bk_scr[u, n2:2 * n2, :] = _stack_heads(kc * p_inv, mask_a).astype(BF16)
        at_scr[u, :, 0:n2] = a_s
        vs_scr[u] = v_s.astype(BF16)
        vst_scr[u] = v_s.T.astype(BF16)
        bend_scr[u] = _stack_heads(bb * p_end, mask_a).astype(BF16)
        kend_scr[u] = _stack_heads(kc * p_end, mask_a).astype(BF16)
        pt_scr[u] = jnp.broadcast_to(jnp.exp(last), (8, LANE))
    eye = jnp.where(row == col, 1.0, 0.0)
    for u in n_unit:
        quad = nt(ar_scr[u], bk_scr[u])
        a_ab = jnp.where(strict, quad[0:n2, 0:n2], 0.0)
        x_scr[u] = eye + a_ab
        pw_scr[u] = a_ab.astype(BF16)
        akrk_scr[u, 0:n2, :] = jnp.where(strict, quad[0:n2, n2:2 * n2], 0.0).astype(BF16)
        akrk_scr[u, n2:2 * n2, :] = jnp.where(incl, quad[n2:2 * n2, n2:2 * n2], 0.0).astype(BF16)
        arb_scr[u] = jnp.where(incl, quad[n2:2 * n2, 0:n2], 0.0).astype(BF16)
    for _ in range(CHUNK.bit_length() - 2):
        for u in n_unit:
            pw_scr[u] = dot(pw_scr[u], pw_scr[u]).astype(BF16)
        for u in n_unit:
            x_scr[u] = x_scr[u] + dot(x_scr[u].astype(BF16), pw_scr[u])
        fill()
    for u in n_unit:
        ty = dot(akrk_scr[u], vs_scr[u])
        at_scr[u, :, n2:2 * n2] = ty[0:n2, :].astype(BF16)
        yv_scr[u] = ty[n2:2 * n2, :]
    fill()
    for u in n_unit:
        kv_scr[u] = dot(vst_scr[u], kend_scr[u])
    fill(len(fillers))
    for u in n_unit:
        wu = dot(x_scr[u].astype(BF16), at_scr[u])
        wu_scr[u] = wu.astype(BF16)
        wut_scr[u] = wu.T.astype(BF16)
    for u in n_unit:
        gh = dot(wut_scr[u], bend_scr[u])
        g_scr[u] = gh[0:n2, :].astype(BF16)
        h_scr[u] = gh[n2:2 * n2, :] + kv_scr[u]
    for u in n_unit:
        qy = dot(arb_scr[u], wu_scr[u])
        q_scr[u] = (ar_scr[u, n2:2 * n2, :].astype(F32) + qy[:, 0:n2]).astype(BF16)
        y0_scr[u] = qy[:, n2:2 * n2] + yv_scr[u]
    for c in range(n_chunk):
        rows = slice(c * CHUNK, (c + 1) * CHUNK)
        us = [c * N_PAIR + p for p in range(N_PAIR)]
        s0 = [s_scr[p] for p in range(N_PAIR)]
        s0b = [a.astype(BF16) for a in s0]
        for p, u in enumerate(us):
            s_scr[p] = s0[p] * pt_scr[u, 0:1, :] + dot(s0b[p], g_scr[u]) + h_scr[u]
        for p, u in enumerate(us):
            y_s = nt(q_scr[u], s0b[p]) + y0_scr[u]
            y_scr[rows, p * LANE:(p + 1) * LANE] = y_s[0:CHUNK, :] + y_s[CHUNK:n2, :]

    y_ref[...] = _rwkv_finish(y_scr[...], r_scr[...], k_scr[...], v_scr[...], rk_ref[...],
                              gg_ref[...], gb_ref[...], z_scr[:, SHIFT_W:SHIFT_W + BRANCH_W], bd_ref)
    for p in range(N_PAIR):
        st_ref[2 * p] = s_scr[p, 0:HEAD_DIM, 0:HEAD_DIM]
        st_ref[2 * p + 1] = s_scr[p, HEAD_DIM:LANE, HEAD_DIM:LANE]


def _head_block_diag():
    i = jnp.arange(BRANCH_W) // HEAD_DIM
    return (i[:, None] == i[None, :]).astype(BF16)


N_CONV_REFS = (5, 2, 2)
N_RWKV_REFS = (12, 3, 31)


def _conv_rwkv_kernel(x_ref, g_ref, *refs):
    ci, co, cs = N_CONV_REFS
    ri, ro, rs = N_RWKV_REFS
    conv_in, refs = refs[:ci], refs[ci:]
    rwkv_in, refs = refs[:ri], refs[ri:]
    conv_out, refs = refs[:co], refs[co:]
    rwkv_out, refs = refs[:ro], refs[ro:]
    conv_scr, rwkv_scr = refs[:cs], refs[cs:]
    assert len(rwkv_scr) == rs
    tt = x_ref.shape[0]

    @pl.when(pl.program_id(1) == 0)
    def _():
        _conv_init(conv_scr[1], tt)
        _rwkv_init(rwkv_scr[1], rwkv_scr[9])

    h = _rms(x_ref[...], g_ref[...]).astype(BF16)
    conv_chunks, conv_finish = _conv_tile(h, *conv_in, *conv_out, *conv_scr)
    _rwkv_tile(h, conv_chunks, *rwkv_in, *rwkv_out, *rwkv_scr)
    conv_finish()


def _conv_rwkv_call(x, lp):
    b, t, _ = x.shape
    tt = TOK_TILE
    vec = _full((1, BRANCH_W))
    scr = lambda w: pltpu.VMEM((tt, w), F32)
    units = (tt // CHUNK) * N_PAIR
    unit = lambda dt, r=1, c=1: pltpu.VMEM((units, r * LANE, c * LANE), dt)
    tok = pl.BlockSpec((None, tt, BRANCH_W), lambda i, j: (i, j, 0))
    return pl.pallas_call(
        _conv_rwkv_kernel,
        grid=(b, t // tt),
        in_specs=[
            pl.BlockSpec((None, tt, D_MODEL), lambda i, j: (i, j, 0)), _full((1, D_MODEL)),
            _full((D_MODEL, N_CONV_IN)), _full((CONV_W, BRANCH_W)), vec, vec, vec,
            _full((D_MODEL, N_RWKV_IN)), _full((1, SHIFT_W)),
            vec, _full((2 * LORA, BRANCH_W)), vec, _full((2 * LORA, BRANCH_W)), vec, vec,
            vec, vec, vec, _full((BRANCH_W, BRANCH_W)),
        ],
        out_specs=[
            tok,
            pl.BlockSpec((None, CONV_W - 1, BRANCH_W), lambda i, j: (i, 0, 0)),
            tok,
            pl.BlockSpec((None, N_HEAD, HEAD_DIM, HEAD_DIM), lambda i, j: (i, 0, 0, 0)),
            pl.BlockSpec((None, 1, SHIFT_W), lambda i, j: (i, 0, 0)),
        ],
        out_shape=[
            jax.ShapeDtypeStruct((b, t, BRANCH_W), BF16),
            jax.ShapeDtypeStruct((b, CONV_W - 1, BRANCH_W), F32),
            jax.ShapeDtypeStruct((b, t, BRANCH_W), BF16),
            jax.ShapeDtypeStruct((b, N_HEAD, HEAD_DIM, HEAD_DIM), F32),
            jax.ShapeDtypeStruct((b, 1, SHIFT_W), F32),
        ],
        scratch_shapes=[
            scr(N_CONV_IN), pltpu.VMEM((HIST + tt + 8, BRANCH_W), F32),
            scr(N_RWKV_IN), pltpu.VMEM((8 + tt, SHIFT_W), F32),
            scr(BRANCH_W), scr(BRANCH_W), scr(BRANCH_W), scr(BRANCH_W), scr(BRANCH_W), scr(BRANCH_W),
            scr(BRANCH_W), pltpu.VMEM((N_PAIR, LANE, LANE), F32),
            scr(BRANCH_W), unit(BF16, 2), unit(BF16, 2), unit(BF16, 1, 2), *([unit(BF16)] * 5),
            unit(BF16, 2), unit(BF16), unit(BF16, 1, 2), unit(BF16, 2), unit(BF16), unit(BF16),
            *([unit(F32)] * 5), pltpu.VMEM((units, 8, LANE), F32),
        ],
        compiler_params=_params("parallel", "arbitrary"),
        name="conv_rwkv_branches",
    )(x, lp["norm_g"], lp["w_conv"], lp["conv_w"], lp["conv_b"], lp["conv_ln_g"], lp["conv_ln_b"],
      lp["w_rwkv"], lp["shift_mu"], lp["w0"], lp["w_lora_b"], lp["a0"], lp["a_lora_b"],
      lp["k_k"], lp["k_a"], lp["r_k"], lp["gn_g"], lp["gn_b"], lp["head_bd"])


def _merge_kernel(x_ref, yc_ref, ya_ref, yr_ref, pe_ref, g_ref, wm_ref, wb_ref, wo_ref, pg_ref,
                  pp_ref, fg_ref, o_ref, mg_scr, *, att_transposed, final_norm):
    x = x_ref[...]
    mg_scr[...] = _bdot(_rms(x, g_ref[...]), wm_ref[...])
    ya = ya_ref[...].T if att_transposed else ya_ref[...]
    s = (_sigmoid(mg_scr[:, 0:D_MODEL]) * _bdot(yc_ref[...], wb_ref[0])
         + _sigmoid(mg_scr[:, D_MODEL:2 * D_MODEL]) * _bdot(ya, wb_ref[1])
         + _sigmoid(mg_scr[:, 2 * D_MODEL:3 * D_MODEL]) * _bdot(yr_ref[...], wb_ref[2]))
    x = x + _bdot(s, wo_ref[...])
    x = x + _sigmoid(_bdot(x, pg_ref[...])) * _bdot(pe_ref[...], pp_ref[...])
    o_ref[...] = _rms(x, fg_ref[...]) if final_norm else x


def _merge_call(x, y_conv, y_att, y_rw, pe, lp, final_g, *, att_transposed, final_norm):
    b, t, _ = x.shape
    tm = min(TOK_TILE, t)
    tok = lambda i, j: (i, j, 0)
    att_spec = (pl.BlockSpec((None, BRANCH_W, tm), lambda i, j: (i, 0, j)) if att_transposed
                else pl.BlockSpec((None, tm, BRANCH_W), tok))
    return pl.pallas_call(
        functools.partial(_merge_kernel, att_transposed=att_transposed, final_norm=final_norm),
        grid=(b, t // tm),
        in_specs=[
            pl.BlockSpec((None, tm, D_MODEL), tok),
            pl.BlockSpec((None, tm, BRANCH_W), tok),
            att_spec,
            pl.BlockSpec((None, tm, BRANCH_W), tok),
            pl.BlockSpec((None, tm, P_DIM), tok),
            _full((1, D_MODEL)), _full((D_MODEL, N_MERGE_IN)), _full((3, BRANCH_W, D_MODEL)),
            _full((D_MODEL, D_MODEL)), _full((D_MODEL, D_MODEL)), _full((P_DIM, D_MODEL)),
            _full((1, D_MODEL)),
        ],
        out_specs=pl.BlockSpec((None, tm, D_MODEL), tok),
        out_shape=jax.ShapeDtypeStruct((b, t, D_MODEL), F32),
        scratch_shapes=[pltpu.VMEM((tm, N_MERGE_IN), F32)],
        compiler_params=_params("parallel", "parallel"),
        name="merge",
    )(x, y_conv, y_att, y_rw, pe, lp["norm_g"], lp["w_merge"], lp["w_branch"], lp["w_out"],
      lp["ple_gate"], lp["ple_proj"], final_g)


def _proj_kernel(x_ref, g_ref, w_ref, z_ref):
    z_ref[...] = _bdot(_rms(x_ref[...], g_ref[...]), w_ref[...])


def _proj_call(x, norm_g, w_in):
    m = x.shape[0]
    n_tiles = 3
    tn = N_IN // n_tiles
    return pl.pallas_call(
        _proj_kernel,
        grid=(n_tiles,),
        in_specs=[_full((m, D_MODEL)), _full((1, D_MODEL)),
                  pl.BlockSpec((D_MODEL, tn), lambda j: (0, j))],
        out_specs=pl.BlockSpec((m, tn), lambda j: (0, j)),
        out_shape=jax.ShapeDtypeStruct((m, N_IN), F32),
        compiler_params=_params("parallel"),
        name="decode_project",
    )(x, norm_g, w_in)


def _row_to_col(row_vec):
    n = row_vec.shape[1]
    eye = lax.broadcasted_iota(jnp.int32, (n, n), 0) == lax.broadcasted_iota(jnp.int32, (n, n), 1)
    return jnp.sum(jnp.where(eye, row_vec, 0.0), axis=1, keepdims=True)


def _col_to_row(col_vec):
    n = col_vec.shape[0]
    eye = lax.broadcasted_iota(jnp.int32, (n, n), 0) == lax.broadcasted_iota(jnp.int32, (n, n), 1)
    return jnp.sum(jnp.where(eye, col_vec, 0.0), axis=0, keepdims=True)


def _mix_kernel(z_ref, buf_ref, st_ref, sh0_ref, cw_ref, cb_ref, lg_ref, lb_ref, cos_ref, sa_ref,
                sb_ref, mu_ref, w0_ref, wlb_ref, a0_ref, alb_ref, kk_ref, ka_ref, rk_ref, gg_ref,
                gb_ref, bd_ref,
                yc_ref, cs_ref, q_ref, k_ref, v_ref, sg_ref, yr_ref, so_ref, sho_ref, ext_scr):
    o_att = N_CONV_IN
    o_rw = N_CONV_IN + N_ATT_IN
    u = z_ref[:, 0:BRANCH_W] * _sigmoid(z_ref[:, BRANCH_W:2 * BRANCH_W])
    ext_scr[0:8, :] = jnp.zeros((8, BRANCH_W), F32)
    ext_scr[2:HIST, :] = buf_ref[...]
    ext_scr[HIST:HIST + 16, :] = jnp.broadcast_to(u, (16, BRANCH_W))
    yc_ref[...] = _conv_branch(ext_scr, HIST, 8, cw_ref, cb_ref[...], lg_ref[...], lb_ref[...],
                               z_ref[:, 2 * BRANCH_W:3 * BRANCH_W])[0:1, :]
    cs_ref[...] = ext_scr[3:HIST + 1, :]
    n = BRANCH_W // LANE
    cos, sa, sb = (_tile_lanes(r[...], n) for r in (cos_ref, sa_ref, sb_ref))
    q_ref[...] = _rope(z_ref[:, o_att:o_att + BRANCH_W], cos, sa, sb) * (HEAD_DIM ** -0.5)
    k_ref[...] = _rope(z_ref[:, o_att + BRANCH_W:o_att + 2 * BRANCH_W], cos, sa, sb)
    v_ref[...] = z_ref[:, o_att + 2 * BRANCH_W:o_att + 3 * BRANCH_W]
    sg_ref[...] = _silu(z_ref[:, o_att + 3 * BRANCH_W:o_att + 4 * BRANCH_W])
    cur = z_ref[:, o_rw:o_rw + SHIFT_W]
    sho_ref[...] = cur
    mixed = cur + (sh0_ref[...] - cur) * mu_ref[...]
    r, k, v, log_decay, kk, a_lr = _rwkv_columns(
        mixed, w0_ref[...], wlb_ref, a0_ref[...], alb_ref, kk_ref[...], ka_ref[...], bd_ref)
    decay = jnp.exp(log_decay)
    b_vec = kk * a_lr
    ys = []
    for h in range(N_HEAD):
        lanes = slice(h * HEAD_DIM, (h + 1) * HEAD_DIM)
        s = st_ref[h]
        sa_col = jnp.sum(s * (-kk[:, lanes]), axis=1, keepdims=True)
        s = s * decay[:, lanes] + sa_col * b_vec[:, lanes] + _row_to_col(v[:, lanes]) * k[:, lanes]
        so_ref[h] = s
        ys.append(_col_to_row(jnp.sum(s * r[:, lanes], axis=1, keepdims=True)))
    y = jnp.concatenate(ys, axis=1)
    yr_ref[...] = _rwkv_finish(y, r, k, v, rk_ref[...], gg_ref[...], gb_ref[...],
                               z_ref[:, o_rw + SHIFT_W:o_rw + SHIFT_W + BRANCH_W], bd_ref)


def _mix_call(z, buf, state, shift0, lp, tables):
    b = z.shape[0]
    row = lambda w: pl.BlockSpec((None, 1, w), lambda i: (i, 0, 0))
    vec = _full((1, BRANCH_W))
    tab = _full((1, LANE))
    o = lambda w, dt=F32: jax.ShapeDtypeStruct((b, 1, w), dt)
    return pl.pallas_call(
        _mix_kernel,
        grid=(b,),
        in_specs=[
            row(N_IN),
            pl.BlockSpec((None, CONV_W - 1, BRANCH_W), lambda i: (i, 0, 0)),
            pl.BlockSpec((None, N_HEAD, HEAD_DIM, HEAD_DIM), lambda i: (i, 0, 0, 0)),
            row(SHIFT_W),
            _full((CONV_W, BRANCH_W)), vec, vec, vec, tab, tab, tab,
            _full((1, SHIFT_W)), vec, _full((2 * LORA, BRANCH_W)), vec, _full((2 * LORA, BRANCH_W)),
            vec, vec, vec, vec, vec, _full((BRANCH_W, BRANCH_W)),
        ],
        out_specs=[
            row(BRANCH_W),
            pl.BlockSpec((None, CONV_W - 1, BRANCH_W), lambda i: (i, 0, 0)),
            row(BRANCH_W), row(BRANCH_W), row(BRANCH_W), row(BRANCH_W), row(BRANCH_W),
            pl.BlockSpec((None, N_HEAD, HEAD_DIM, HEAD_DIM), lambda i: (i, 0, 0, 0)),
            row(SHIFT_W),
        ],
        out_shape=[
            o(BRANCH_W, BF16), jax.ShapeDtypeStruct((b, CONV_W - 1, BRANCH_W), F32),
            o(BRANCH_W), o(BRANCH_W), o(BRANCH_W), o(BRANCH_W), o(BRANCH_W, BF16),
            jax.ShapeDtypeStruct((b, N_HEAD, HEAD_DIM, HEAD_DIM), F32), o(SHIFT_W),
        ],
        scratch_shapes=[pltpu.VMEM((HIST + 16, BRANCH_W), F32)],
        compiler_params=_params("parallel"),
        name="decode_mix",
    )(z.reshape(b, 1, N_IN), buf, state, shift0.reshape(b, 1, SHIFT_W),
      lp["conv_w"], lp["conv_b"], lp["conv_ln_g"], lp["conv_ln_b"], *tables,
      lp["shift_mu"], lp["w0"], lp["w_lora_b"], lp["a0"], lp["a_lora_b"], lp["k_k"], lp["k_a"],
      lp["r_k"], lp["gn_g"], lp["gn_b"], lp["head_bd"])


def _head_rows(q_row):
    lane = lax.broadcasted_iota(jnp.int32, (N_HEAD, BRANCH_W), 1)
    row = lax.broadcasted_iota(jnp.int32, (N_HEAD, BRANCH_W), 0)
    return jnp.where((lane >= row * HEAD_DIM) & (lane < (row + 1) * HEAD_DIM), q_row, 0.0)


def _score_kernel(pt_ref, q_ref, *refs):
    del pt_ref
    pages = refs[:PAGES_PER_STEP]
    s_ref, gate_ref = refs[PAGES_PER_STEP:]
    j = pl.program_id(1)
    page_rows = pages[0].shape[1]

    @pl.when(j == 0)
    def _():
        gate_ref[...] = jnp.zeros(gate_ref.shape, F32)

    qh = _head_rows(q_ref[...])
    lane = lax.broadcasted_iota(jnp.int32, gate_ref.shape, 1)
    pages_per_block = MOBA_BLOCK // page_rows
    gate = gate_ref[...]
    for r in range(PAGES_PER_STEP):
        s = _bdot(qh, pages[r][...])
        s_ref[:, r * page_rows:(r + 1) * page_rows] = s
        blk = (j * PAGES_PER_STEP + r) // pages_per_block
        gate = gate + jnp.where(lane == blk, jnp.sum(s, axis=1, keepdims=True), 0.0)
    gate_ref[...] = gate


def _score_call(page_table, q, cache, layer, n_pool):
    b, n_pages = page_table.shape
    page_rows = cache.shape[2]
    past = n_pages * page_rows
    assert n_pages % PAGES_PER_STEP == 0 and MOBA_BLOCK % page_rows == 0
    assert past // MOBA_BLOCK <= LANE

    def page_spec(r):
        return pl.BlockSpec(
            (None, BRANCH_W, page_rows),
            lambda i, j, pt: (layer * n_pool + pt[i, j * PAGES_PER_STEP + r], 0, 0))

    grid_spec = pltpu.PrefetchScalarGridSpec(
        num_scalar_prefetch=1,
        grid=(b, n_pages // PAGES_PER_STEP),
        in_specs=[pl.BlockSpec((None, 1, BRANCH_W), lambda i, j, pt: (i, 0, 0))]
        + [page_spec(r) for r in range(PAGES_PER_STEP)],
        out_specs=[
            pl.BlockSpec((None, N_HEAD, PAGES_PER_STEP * page_rows), lambda i, j, pt: (i, 0, j)),
            pl.BlockSpec((None, N_HEAD, LANE), lambda i, j, pt: (i, 0, 0)),
        ],
    )
    return pl.pallas_call(
        _score_kernel,
        grid_spec=grid_spec,
        out_shape=[jax.ShapeDtypeStruct((b, N_HEAD, past), F32),
                   jax.ShapeDtypeStruct((b, N_HEAD, LANE), F32)],
        compiler_params=_params("parallel", "arbitrary"),
        name="decode_scores",
    )(page_table, q, *([cache] * PAGES_PER_STEP))


def _select_kernel(s_ref, gate_ref, q_ref, k_ref, p_ref, idx_ref, pown_ref, *, n_blocks):
    lane = lax.broadcasted_iota(jnp.int32, gate_ref.shape, 1).astype(F32)
    gate = jnp.where(lane < n_blocks, gate_ref[...], -jnp.inf)
    key_blk = jnp.right_shift(lax.broadcasted_iota(jnp.int32, s_ref.shape, 1),
                              MOBA_BLOCK.bit_length() - 1).astype(F32)
    sel = jnp.zeros(s_ref.shape, F32)
    idx_out = jnp.zeros(gate_ref.shape, F32)
    for j in range(MOBA_TOPK):
        m = jnp.max(gate, axis=1, keepdims=True)
        idx = jnp.min(jnp.where(gate == m, lane, float(LANE)), axis=1, keepdims=True)
        sel = jnp.where(key_blk == idx, 1.0, sel)
        idx_out = jnp.where(lane == j, idx, idx_out)
        gate = jnp.where(lane == idx, -jnp.inf, gate)
    s_own = jnp.sum(_head_rows(q_ref[...]) * k_ref[...], axis=1, keepdims=True)
    s = jnp.where(sel > 0.0, s_ref[...], NEG)
    m = jnp.maximum(jnp.max(s, axis=1, keepdims=True), s_own)
    e = jnp.exp(s - m)
    e_own = jnp.exp(s_own - m)
    l = jnp.sum(e, axis=1, keepdims=True) + e_own
    p_ref[...] = e / l
    pown_ref[...] = jnp.broadcast_to(e_own / l, pown_ref.shape)
    idx_ref[...] = idx_out.astype(jnp.int32)


def _select_call(scores, gate, q, k_new):
    b, _, past = scores.shape
    n_blocks = past // MOBA_BLOCK
    assert n_blocks >= MOBA_TOPK
    head = lambda w: pl.BlockSpec((None, N_HEAD, w), lambda i: (i, 0, 0))
    row = pl.BlockSpec((None, 1, BRANCH_W), lambda i: (i, 0, 0))
    return pl.pallas_call(
        functools.partial(_select_kernel, n_blocks=n_blocks),
        grid=(b,),
        in_specs=[head(past), head(LANE), row, row],
        out_specs=[head(past), head(LANE), head(LANE)],
        out_shape=[jax.ShapeDtypeStruct((b, N_HEAD, past), F32),
                   jax.ShapeDtypeStruct((b, N_HEAD, LANE), jnp.int32),
                   jax.ShapeDtypeStruct((b, N_HEAD, LANE), F32)],
        compiler_params=_params("parallel"),
        name="decode_select",
    )(scores, gate, q, k_new)


def _gather_kernel(pt_ref, ix_ref, pown_ref, vnew_ref, sg_ref, *refs, n_sel):
    del pt_ref, ix_ref
    n = GATHER_HEADS * n_sel
    p_rows = refs[:n]
    v_pages = refs[n:2 * n]
    y_ref = refs[2 * n]
    for hh in range(GATHER_HEADS):
        h = pl.program_id(1) * GATHER_HEADS + hh
        acc = pown_ref[pl.ds(h, 1), 0:1] * vnew_ref[hh]
        for r in range(hh * n_sel, (hh + 1) * n_sel):
            p8 = jnp.broadcast_to(p_rows[r][pl.ds(h, 1), :], (8, p_rows[r].shape[1]))
            acc = acc + _bdot_nt(p8, v_pages[r][...])[0:1, :]
        y_ref[hh] = acc * sg_ref[hh]


def _gather_call(page_table, idx, probs, p_own, v_new, sg, cache, layer, n_pool):
    b, n_pages = page_table.shape
    page_rows = cache.shape[2]
    ppb = MOBA_BLOCK // page_rows
    n_sel = MOBA_TOPK * ppb

    def seq_page(i, h, ix, r):
        return ix[i, h * MOBA_TOPK + r // ppb] * ppb + r % ppb

    def p_spec(hh, r):
        return pl.BlockSpec(
            (None, N_HEAD, page_rows),
            lambda i, g, pt, ix: (i, 0, seq_page(i, g * GATHER_HEADS + hh, ix, r)))

    def v_spec(hh, r):
        def index(i, g, pt, ix):
            h = g * GATHER_HEADS + hh
            return layer * n_pool + pt[i, seq_page(i, h, ix, r)], h, 0
        return pl.BlockSpec((None, HEAD_DIM, page_rows), index)

    by_head = lambda a: a.reshape(b, N_HEAD, 1, HEAD_DIM)
    rows = pl.BlockSpec((None, GATHER_HEADS, 1, HEAD_DIM), lambda i, g, pt, ix: (i, g, 0, 0))
    slots = [(hh, r) for hh in range(GATHER_HEADS) for r in range(n_sel)]
    grid_spec = pltpu.PrefetchScalarGridSpec(
        num_scalar_prefetch=2,
        grid=(b, N_HEAD // GATHER_HEADS),
        in_specs=[pl.BlockSpec((None, N_HEAD, LANE), lambda i, g, pt, ix: (i, 0, 0)), rows, rows]
        + [p_spec(*s) for s in slots] + [v_spec(*s) for s in slots],
        out_specs=rows,
    )
    return pl.pallas_call(
        functools.partial(_gather_kernel, n_sel=n_sel),
        grid_spec=grid_spec,
        out_shape=jax.ShapeDtypeStruct((b, N_HEAD, 1, HEAD_DIM), F32),
        compiler_params=_params("parallel", "parallel"),
        name="decode_gather",
    )(page_table, idx, p_own, by_head(v_new), by_head(sg),
      *([probs] * len(slots)), *([cache] * len(slots)))


def _layer_params(i, norm_g, w_in, conv_w, conv_b, conv_ln_g, conv_ln_b, shift_mu, w0, w_lora_b, a0,
                  a_lora_b, k_k, k_a, r_k, gn_g, gn_b, w_branch, w_out, ple_proj, ple_gate):
    w = w_in[i].astype(BF16)
    o1, o2, o3 = N_CONV_IN, N_CONV_IN + N_ATT_IN, N_CONV_IN + N_ATT_IN + N_RWKV_IN
    row = lambda a: a[i].reshape(1, -1)
    zeros = jnp.zeros((LORA, BRANCH_W), F32)
    return dict(
        norm_g=row(norm_g), w_in=w, w_conv=w[:, :o1], w_att=w[:, o1:o2], w_rwkv=w[:, o2:o3],
        w_merge=w[:, o3:],
        conv_w=conv_w[i], conv_b=row(conv_b), conv_ln_g=row(conv_ln_g), conv_ln_b=row(conv_ln_b),
        shift_mu=row(shift_mu), w0=row(w0), a0=row(a0),
        w_lora_b=jnp.concatenate([w_lora_b[i], zeros], axis=0).astype(BF16),
        a_lora_b=jnp.concatenate([zeros, a_lora_b[i]], axis=0).astype(BF16),
        k_k=row(k_k), k_a=row(k_a), r_k=row(r_k), gn_g=row(gn_g), gn_b=row(gn_b),
        w_branch=w_branch[i].astype(BF16), w_out=w_out[i].astype(BF16),
        ple_proj=ple_proj[i].astype(BF16), ple_gate=ple_gate[i].astype(BF16),
        head_bd=_head_block_diag(),
    )


def _prompt_layer(x, pe, lp, tables, final_g, final_norm, layer, depth, kv_t):
    b, t, _ = x.shape
    nb = t // MOBA_BLOCK
    y_conv, conv_new, y_rw, wkv, shift = _conv_rwkv_call(x, lp)
    k_t, v_t, qt, kb, vt, sgt, km = _att1_call(x, lp["norm_g"], lp["w_att"], tables, layer, depth, kv_t)
    km = km.reshape(b, nb, N_HEAD, HEAD_DIM).transpose(0, 2, 1, 3)
    y_att_t = _att2_call(qt, kb, vt, km, sgt)
    x = _merge_call(x, y_conv, y_att_t, y_rw, pe, lp, final_g,
                    att_transposed=True, final_norm=final_norm)
    return x, (k_t, v_t), conv_new, wkv, shift.reshape(b, SHIFT_W)


def _sample_layer(x, pe, lp, tables, final_g, final_norm, layer, n_pool, cache_k, cache_v,
                  page_table, buf, state, shift0):
    b = x.shape[0]
    z = _proj_call(x.reshape(b, D_MODEL), lp["norm_g"], lp["w_in"])
    y_conv, conv_new, q, k, v, sg, y_rw, wkv, shift = _mix_call(z, buf, state, shift0, lp, tables)
    scores, gate = _score_call(page_table, q, cache_k, layer, n_pool)
    probs, idx, p_own = _select_call(scores, gate, q, k)
    idx = idx[:, :, :MOBA_TOPK].reshape(b, N_HEAD * MOBA_TOPK)
    y_att = _gather_call(page_table, idx, probs, p_own, v, sg, cache_v, layer, n_pool)
    tok = lambda a: a.reshape(1, b, -1)
    x = _merge_call(tok(x), tok(y_conv), tok(y_att), tok(y_rw), tok(pe), lp, final_g,
                    att_transposed=False, final_norm=final_norm)
    return (x.reshape(b, 1, D_MODEL), k, v, conv_new, wkv, shift.reshape(b, SHIFT_W))


def kernel(x_prompt, x_sample, cache_k, cache_v, page_table, state_conv, state_wkv, state_shift,
           p_prompt, p_sample, norm_g, w_in, conv_w, conv_b, conv_ln_g, conv_ln_b, shift_mu, w0,
           w_lora_b, a0, a_lora_b, k_k, k_a, r_k, gn_g, gn_b, w_branch, w_out, ple_proj, ple_gate,
           final_norm_g):
    depth = w_in.shape[0]
    b_p, t_p, _ = x_prompt.shape
    b_s, t_s, _ = x_sample.shape
    assert t_s == 1 and t_p % TOK_TILE == 0 and TOK_TILE == MOBA_BLOCK
    n_pool, page_rows = cache_k.shape[1], cache_k.shape[2]
    past_len = page_table.shape[1] * page_rows
    assert past_len % MOBA_BLOCK == 0
    tables_p = _rope_tables(jnp.arange(t_p, dtype=jnp.int32))
    tables_s = _rope_tables(past_len + jnp.arange(1, dtype=jnp.int32))
    by_page = lambda c: c.transpose(0, 1, 3, 4, 2).reshape(depth * n_pool, BRANCH_W, page_rows)
    cache_k, cache_v = by_page(cache_k), by_page(cache_v)
    final_g = final_norm_g.reshape(1, D_MODEL)
    xp, xs = x_prompt, x_sample
    outs = [[] for _ in range(8)]
    kv_t = None
    for i in range(depth):
        lp = _layer_params(i, norm_g, w_in, conv_w, conv_b, conv_ln_g, conv_ln_b, shift_mu, w0,
                           w_lora_b, a0, a_lora_b, k_k, k_a, r_k, gn_g, gn_b, w_branch, w_out,
                           ple_proj, ple_gate)
        last = i == depth - 1
        xp, kv_t, cp, wp, sp = _prompt_layer(xp, p_prompt[i], lp, tables_p, final_g, last, i,
                                             depth, kv_t)
        xs, ks, vs, cs, ws, ss = _sample_layer(
            xs, p_sample[i], lp, tables_s, final_g, last, i, n_pool, cache_k, cache_v, page_table,
            state_conv[i], state_wkv[i], state_shift[i])
        heads_s = lambda a: a.reshape(b_s, 1, N_HEAD, HEAD_DIM)
        for lst, a in zip(outs, (heads_s(ks), heads_s(vs), cp, cs, wp, ws, sp, ss)):
            lst.append(a)
    heads_p = lambda a: a.reshape(depth, b_p, N_HEAD, HEAD_DIM, t_p).transpose(0, 1, 4, 2, 3)
    return (xp, xs, heads_p(kv_t[0]), heads_p(kv_t[1])) + tuple(jnp.stack(lst) for lst in outs)
```

```python
import functools

import jax
import jax.numpy as jnp
from jax import lax
from jax.experimental import pallas as pl
from jax.experimental.pallas import tpu as pltpu

D_MODEL = 1024
P_DIM = 256
HEAD_DIM = 64
BRANCH_W = 512
N_HEAD = 8
N_PAIR = 4
CONV_W = 31
ROT_DIM = 16
ROPE_THETA = 500000.0
MOBA_BLOCK = 256
MOBA_TOPK = 3
LORA = 64
SHIFT_W = 3 * BRANCH_W + 2 * LORA
N_CONV_IN = 3 * BRANCH_W
N_ATT_IN = 4 * BRANCH_W
N_RWKV_IN = SHIFT_W + BRANCH_W
N_MERGE_IN = 3 * D_MODEL
N_IN = N_CONV_IN + N_ATT_IN + N_RWKV_IN + N_MERGE_IN
NORM_EPS = 1e-6
LN_EPS = 1e-5
GN_EPS = 64e-5
NEG = -1e30
LOG2_E = 1.4426950408889634
DECAY_SCALE = 0.6065306597126334

LANE = 128
TOK_TILE = 256
V_ROWS = 80
BIAS_ROWS = 16
CHUNK = 64
HIST = 32
CONV_ROWS = 128
GATHER_HEADS = 4
PAGES_PER_STEP = 32
VMEM_LIMIT = 56 * 1024 * 1024

F32 = jnp.float32
BF16 = jnp.bfloat16


def _bdot(a, b):
    return jnp.dot(a.astype(BF16), b.astype(BF16), preferred_element_type=F32)


def _bdot_nt(a, b):
    return lax.dot_general(a.astype(BF16), b.astype(BF16), (((1,), (1,)), ((), ())),
                           preferred_element_type=F32)


def _split_dot(x, m_bf16):
    hi = x.astype(BF16)
    lo = (x - hi.astype(F32)).astype(BF16)
    return (jnp.dot(hi, m_bf16, preferred_element_type=F32)
            + jnp.dot(lo, m_bf16, preferred_element_type=F32))


def _split3_dot_left(m_bf16, x):
    hi = x.astype(BF16)
    r1 = x - hi.astype(F32)
    mid = r1.astype(BF16)
    lo = (r1 - mid.astype(F32)).astype(BF16)
    return (jnp.dot(m_bf16, hi, preferred_element_type=F32)
            + jnp.dot(m_bf16, mid, preferred_element_type=F32)
            + jnp.dot(m_bf16, lo, preferred_element_type=F32))


def _sigmoid(x):
    return 1.0 / (1.0 + jnp.exp(-x))


def _silu(x):
    return x * _sigmoid(x)


def _rms(x, g):
    return x * lax.rsqrt(jnp.mean(x * x, axis=-1, keepdims=True) + NORM_EPS) * g


def _params(*sem):
    return pltpu.CompilerParams(dimension_semantics=sem, vmem_limit_bytes=VMEM_LIMIT)


def _full(shape):
    nd = len(shape)
    return pl.BlockSpec(shape, lambda *_: (0,) * nd)


def _conv_branch(ext_ref, base, rows, cw_ref, cb, lg, lb, g_conv):
    acc = jnp.zeros((rows, BRANCH_W), F32) + cb
    first = base - (CONV_W - 1)
    for s in range(8):
        z = None
        for j in range(CONV_W):
            if (first + j) % 8 == s:
                start = first + j - s
                term = ext_ref[start:start + rows + 8, :] * cw_ref[j:j + 1, :]
                z = term if z is None else z + term
        acc = acc + z[s:s + rows, :]
    mu = jnp.mean(acc, axis=-1, keepdims=True)
    d = acc - mu
    var = jnp.mean(d * d, axis=-1, keepdims=True)
    y = d * lax.rsqrt(var + LN_EPS) * lg + lb
    return (_silu(y) * _silu(g_conv)).astype(BF16)


def _conv_init(ext_scr, tt):
    ext_scr[0:HIST, :] = jnp.zeros((HIST, BRANCH_W), F32)
    ext_scr[HIST + tt:HIST + tt + 8, :] = jnp.zeros((8, BRANCH_W), F32)


def _conv_tile(h, w_ref, cw_ref, cb_ref, lg_ref, lb_ref, y_ref, cs_ref, z_scr, ext_scr):
    tt = h.shape[0]
    z_scr[...] = jnp.dot(h, w_ref[...], preferred_element_type=F32)
    ext_scr[HIST:HIST + tt, :] = z_scr[:, 0:BRANCH_W] * _sigmoid(z_scr[:, BRANCH_W:2 * BRANCH_W])

    def rows(r):
        y_ref[r:r + CONV_ROWS, :] = _conv_branch(
            ext_scr, HIST + r, CONV_ROWS, cw_ref, cb_ref[...], lg_ref[...], lb_ref[...],
            z_scr[r:r + CONV_ROWS, 2 * BRANCH_W:3 * BRANCH_W])

    def finish():
        cs_ref[...] = ext_scr[HIST + tt - (CONV_W - 1):HIST + tt, :]
        ext_scr[0:HIST, :] = ext_scr[tt:tt + HIST, :]

    return [functools.partial(rows, r) for r in range(0, tt, CONV_ROWS)], finish


def _rope_tables(pos):
    half = ROT_DIM // 2
    inv = jnp.power(ROPE_THETA, -jnp.arange(half, dtype=F32) * (2.0 / ROT_DIM))
    ang = pos.astype(F32)[:, None] * inv[None, :]
    cos, sin = jnp.cos(ang), jnp.sin(ang)
    n = pos.shape[0]
    pad = jnp.zeros((n, HEAD_DIM - ROT_DIM), F32)
    cos64 = jnp.concatenate([cos, cos, pad + 1.0], axis=1)
    sa64 = jnp.concatenate([-sin, jnp.zeros_like(sin), pad], axis=1)
    sb64 = jnp.concatenate([jnp.zeros_like(sin), sin, pad], axis=1)
    return tuple(jnp.concatenate([a, a], axis=1) for a in (cos64, sa64, sb64))


def _rope(a, cos, sa, sb):
    w = a.shape[1]
    return a * cos + pltpu.roll(a, w - ROT_DIM // 2, 1) * sa + pltpu.roll(a, ROT_DIM // 2, 1) * sb


def _tile_lanes(a, n):
    return jnp.concatenate([a] * n, axis=1)


def _att1_kernel(x_ref, g_ref, w_ref, cos_ref, sa_ref, sb_ref, *refs):
    k_ref, v_ref, qt_ref, kb_ref, vt_ref, sgt_ref, km_ref, z_scr = refs[-8:]
    z_scr[...] = _bdot(_rms(x_ref[...], g_ref[...]), w_ref[...])
    n = BRANCH_W // LANE
    cos, sa, sb = (_tile_lanes(r[...], n) for r in (cos_ref, sa_ref, sb_ref))
    q = _rope(z_scr[:, 0:BRANCH_W], cos, sa, sb) * (LOG2_E * HEAD_DIM ** -0.5)
    k = _rope(z_scr[:, BRANCH_W:2 * BRANCH_W], cos, sa, sb)
    v = z_scr[:, 2 * BRANCH_W:3 * BRANCH_W]
    v_t = v.T
    k_ref[...] = k.T
    v_ref[...] = v_t
    qt_ref[...] = q.T.astype(BF16)
    lane = lax.broadcasted_iota(jnp.int32, (k.shape[0], LANE), 1)
    one_hot = jnp.where(lane == HEAD_DIM, 1.0, 0.0)
    for p in range(N_PAIR):
        slab = k[:, p * LANE:(p + 1) * LANE]
        kb_ref[2 * p] = jnp.where(lane < HEAD_DIM, slab, one_hot).astype(BF16)
        kb_ref[2 * p + 1] = jnp.where(lane < HEAD_DIM, pltpu.roll(slab, HEAD_DIM, 1),
                                      one_hot).astype(BF16)
    tail = jnp.where(lax.broadcasted_iota(jnp.int32, (V_ROWS - HEAD_DIM, v_t.shape[1]), 0) == 0,
                     1.0, 0.0)
    for h in range(N_HEAD):
        vt_ref[h * V_ROWS:(h + 1) * V_ROWS, :] = jnp.concatenate(
            [v_t[h * HEAD_DIM:(h + 1) * HEAD_DIM, :], tail], axis=0).astype(BF16)
    sgt_ref[...] = _silu(z_scr[:, 3 * BRANCH_W:4 * BRANCH_W]).T
    km_ref[...] = jnp.mean(k, axis=0, keepdims=True)


def _att1_call(x, norm_g, w_a, tables, layer, depth, kv_t):
    b, t, _ = x.shape
    tt = TOK_TILE
    nb = t // tt
    tok = lambda i, j: (i, j, 0)
    tr = lambda i, j: (i, 0, j)
    kv_spec = pl.BlockSpec((None, None, BRANCH_W, tt), lambda i, j: (layer, i, 0, j))
    kv_shape = jax.ShapeDtypeStruct((depth, b, BRANCH_W, t), F32)
    carried = [] if kv_t is None else [pl.BlockSpec(memory_space=pl.ANY)] * 2
    return pl.pallas_call(
        _att1_kernel,
        grid=(b, nb),
        in_specs=[
            pl.BlockSpec((None, tt, D_MODEL), tok),
            _full((1, D_MODEL)), _full((D_MODEL, N_ATT_IN)),
            pl.BlockSpec((tt, LANE), lambda i, j: (j, 0)),
            pl.BlockSpec((tt, LANE), lambda i, j: (j, 0)),
            pl.BlockSpec((tt, LANE), lambda i, j: (j, 0)),
        ] + carried,
        input_output_aliases={} if kv_t is None else {6: 0, 7: 1},
        out_specs=[
            kv_spec,
            kv_spec,
            pl.BlockSpec((None, BRANCH_W, tt), tr),
            pl.BlockSpec((None, N_HEAD, tt, LANE), lambda i, j: (i, 0, j, 0)),
            pl.BlockSpec((None, None, N_HEAD * V_ROWS, tt), lambda i, j: (i, j, 0, 0)),
            pl.BlockSpec((None, BRANCH_W, tt), tr),
            pl.BlockSpec((None, None, 1, BRANCH_W), lambda i, j: (i, j, 0, 0)),
        ],
        out_shape=[
            kv_shape,
            kv_shape,
            jax.ShapeDtypeStruct((b, BRANCH_W, t), BF16),
            jax.ShapeDtypeStruct((b, N_HEAD, t, LANE), BF16),
            jax.ShapeDtypeStruct((b, nb, N_HEAD * V_ROWS, tt), BF16),
            jax.ShapeDtypeStruct((b, BRANCH_W, t), F32),
            jax.ShapeDtypeStruct((b, nb, 1, BRANCH_W), F32),
        ],
        scratch_shapes=[pltpu.VMEM((tt, N_ATT_IN), F32)],
        compiler_params=_params("parallel", "parallel"),
        name="att_project",
    )(x, norm_g, w_a, *tables, *(kv_t or ()))


def _top_blocks(gate, n_slots):
    nb = gate.shape[0]
    row = lax.broadcasted_iota(jnp.int32, gate.shape, 0).astype(F32)
    sel = jnp.zeros(gate.shape, F32)
    for j in range(MOBA_TOPK):
        m = jnp.max(gate, axis=0, keepdims=True)
        idx = jnp.min(jnp.where(gate == m, row, float(nb)), axis=0, keepdims=True)
        hit = row == jnp.where(j < n_slots, idx, -1.0)
        sel = jnp.where(hit, 1.0, sel)
        gate = jnp.where(row == idx, -jnp.inf, gate)
    return sel


def _att2_kernel(qt_ref, kb_ref, vt_ref, km_ref, sgt_ref, yt_ref,
                 qp_scr, bias_scr, s_scr, e_scr, alpha_scr, m_scr, acc_scr):
    i = pl.program_id(1)
    tq = qt_ref.shape[1]
    nb = km_ref.shape[1]
    blk = lax.broadcasted_iota(jnp.int32, (nb, tq), 0)
    qp_scr[...] = jnp.zeros(qp_scr.shape, BF16)
    for h in range(N_HEAD):
        q_h = qt_ref[h * HEAD_DIM:(h + 1) * HEAD_DIM, :]
        qp_scr[h, 0:HEAD_DIM, :] = q_h
        gate = jnp.dot(km_ref[h].astype(BF16), q_h, preferred_element_type=F32)
        sel = _top_blocks(jnp.where(blk < i, gate, -jnp.inf), i)
        bias_scr[h] = jnp.where(sel > 0.0, 0.0, NEG)
    m_scr[...] = jnp.full(m_scr.shape, NEG, F32)
    acc_scr[...] = jnp.zeros(acc_scr.shape, F32)
    bias_row0 = lax.broadcasted_iota(jnp.int32, (BIAS_ROWS, tq), 0) == 0

    def scores(n, slot, bias_of, extra=None):
        start = pl.multiple_of(n * MOBA_BLOCK, MOBA_BLOCK)
        for h in range(N_HEAD):
            qp_scr[h, HEAD_DIM:HEAD_DIM + BIAS_ROWS, :] = jnp.where(
                bias_row0, bias_of(h), 0.0).astype(BF16)
            s = jnp.dot(kb_ref[h, pl.ds(start, MOBA_BLOCK), :], qp_scr[h],
                        preferred_element_type=F32)
            s_scr[slot, h] = s if extra is None else s + extra

    def softmax(slot):
        for h in range(N_HEAD):
            m = m_scr[h:h + 1, :]
            m_new = jnp.maximum(m, jnp.max(s_scr[slot, h], axis=0, keepdims=True))
            m_scr[h:h + 1, :] = m_new
            alpha_scr[slot, h:h + 1, :] = jnp.exp2(m - m_new)
            e_scr[slot, h] = jnp.exp2(s_scr[slot, h] - m_new).astype(BF16)

    def values(n, slot):
        for h in range(N_HEAD):
            pv = jnp.dot(vt_ref[n, h * V_ROWS:(h + 1) * V_ROWS, :], e_scr[slot, h],
                         preferred_element_type=F32)
            acc_scr[h] = alpha_scr[slot, h:h + 1, :] * acc_scr[h] + pv

    def row_bias(n):
        return lambda h: bias_scr[h, pl.ds(n, 1), :]

    def block_of(v):
        return jnp.where(v == 0, i, jnp.minimum(v - 1, nb - 1))

    def step(t, slot):
        nxt = block_of(t + 1)
        scores(nxt, 1 - slot, row_bias(nxt))
        values(block_of(t - 1), 1 - slot)
        softmax(slot)

    causal = jnp.where(lax.broadcasted_iota(jnp.int32, (MOBA_BLOCK, tq), 0)
                       <= lax.broadcasted_iota(jnp.int32, (MOBA_BLOCK, tq), 1), 0.0, NEG)
    scores(i, 0, lambda h: jnp.zeros((1, tq), F32), extra=causal)
    softmax(0)
    scores(0, 1, row_bias(0))

    def body(j, carry):
        step(2 * j + 1, 1)
        step(2 * j + 2, 0)
        return carry

    trips = (i + 1) // 2
    lax.fori_loop(0, trips, body, 0)
    values(block_of(2 * trips), 0)
    for h in range(N_HEAD):
        rows = slice(h * HEAD_DIM, (h + 1) * HEAD_DIM)
        denom = acc_scr[h, HEAD_DIM:HEAD_DIM + 1, :]
        yt_ref[rows, :] = acc_scr[h, 0:HEAD_DIM, :] / denom * sgt_ref[rows, :]


def _att2_call(qt, kb, vt, km, sgt):
    b, _, t = qt.shape
    nb = vt.shape[1]
    tq = MOBA_BLOCK
    return pl.pallas_call(
        _att2_kernel,
        grid=(b, t // tq),
        in_specs=[
            pl.BlockSpec((None, BRANCH_W, tq), lambda bi, i: (bi, 0, i)),
            pl.BlockSpec((None, N_HEAD, t, LANE), lambda bi, i: (bi, 0, 0, 0),
                         pipeline_mode=pl.Buffered(1)),
            pl.BlockSpec((None, nb, N_HEAD * V_ROWS, MOBA_BLOCK), lambda bi, i: (bi, 0, 0, 0),
                         pipeline_mode=pl.Buffered(1)),
            pl.BlockSpec((None, N_HEAD, nb, HEAD_DIM), lambda bi, i: (bi, 0, 0, 0)),
            pl.BlockSpec((None, BRANCH_W, tq), lambda bi, i: (bi, 0, i)),
        ],
        out_specs=pl.BlockSpec((None, BRANCH_W, tq), lambda bi, i: (bi, 0, i)),
        out_shape=jax.ShapeDtypeStruct((b, BRANCH_W, t), F32),
        scratch_shapes=[
            pltpu.VMEM((N_HEAD, LANE, tq), BF16), pltpu.VMEM((N_HEAD, nb, tq), F32),
            pltpu.VMEM((2, N_HEAD, MOBA_BLOCK, tq), F32), pltpu.VMEM((2, N_HEAD, MOBA_BLOCK, tq), BF16),
            pltpu.VMEM((2, N_HEAD, tq), F32), pltpu.VMEM((N_HEAD, tq), F32),
            pltpu.VMEM((N_HEAD, V_ROWS, tq), F32),
        ],
        compiler_params=_params("parallel", "arbitrary"),
        name="moba_attention",
    )(qt, kb, vt, km, sgt)


def _rwkv_columns(mixed, w0, wlb_ref, a0, alb_ref, kk_scale, k_a, bd_ref):
    r = mixed[:, 0:BRANCH_W]
    k = mixed[:, BRANCH_W:2 * BRANCH_W]
    v = mixed[:, 2 * BRANCH_W:3 * BRANCH_W]
    lora = mixed[:, 3 * BRANCH_W:3 * BRANCH_W + 2 * LORA]
    lane = lax.broadcasted_iota(jnp.int32, lora.shape, 1)
    lora = jnp.where(lane < LORA, jnp.tanh(lora), lora)
    log_decay = -(DECAY_SCALE * _sigmoid(w0 + _bdot(lora, wlb_ref[...])))
    a_lr = _sigmoid(a0 + _bdot(lora, alb_ref[...]))
    kk = k * kk_scale
    kk = kk * lax.rsqrt(jnp.maximum(_bdot(kk * kk, bd_ref[...]), 1e-24))
    k = k * (1.0 + (a_lr - 1.0) * k_a)
    return r, k, v, log_decay, kk, a_lr


def _rwkv_finish(y, r, k, v, r_k, gn_g, gn_b, g_rwkv, bd_ref):
    m = _split_dot(y, bd_ref[...]) * (1.0 / HEAD_DIM)
    d = y - m
    var = _bdot(d * d, bd_ref[...]) * (1.0 / HEAD_DIM)
    yn = d * lax.rsqrt(var + GN_EPS) * gn_g + gn_b
    bonus = _bdot(r * k * r_k, bd_ref[...]) * v
    return ((yn + bonus) * _silu(g_rwkv)).astype(BF16)


def _stack_heads(a, mask_a):
    return jnp.concatenate([jnp.where(mask_a, a, 0.0), jnp.where(mask_a, 0.0, a)], axis=0)


def _rwkv_init(shs_scr, s_scr):
    shs_scr[0:8, :] = jnp.zeros((8, SHIFT_W), F32)
    s_scr[...] = jnp.zeros(s_scr.shape, F32)


def _rwkv_tile(h, fillers, w_ref, mu_ref, w0_ref, wlb_ref, a0_ref, alb_ref, kk_ref, ka_ref,
               rk_ref, gg_ref, gb_ref, bd_ref,
               y_ref, st_ref, sh_ref,
               z_scr, shs_scr, r_scr, k_scr, v_scr, ld_scr, a_scr, b_scr, y_scr, s_scr,
               cum_scr, ar_scr, bk_scr, at_scr, vs_scr, vst_scr, bend_scr, kend_scr, pw_scr, akrk_scr,
               arb_scr, wu_scr, wut_scr, g_scr, q_scr, x_scr, yv_scr, kv_scr, h_scr, y0_scr, pt_scr):
    tt = h.shape[0]
    fillers = list(fillers)

    def fill(n=1):
        for _ in range(min(n, len(fillers))):
            fillers.pop(0)()

    z_scr[...] = jnp.dot(h, w_ref[...], preferred_element_type=F32)
    fill(2)
    shs_scr[8:8 + tt, :] = z_scr[:, 0:SHIFT_W]
    cur = z_scr[:, 0:SHIFT_W]
    mixed = cur + (shs_scr[7:7 + tt, :] - cur) * mu_ref[...]
    sh_ref[...] = shs_scr[7 + tt:8 + tt, :]
    shs_scr[7:8, :] = shs_scr[7 + tt:8 + tt, :]

    r, k, v, log_decay, kk, a_lr = _rwkv_columns(
        mixed, w0_ref[...], wlb_ref, a0_ref[...], alb_ref, kk_ref[...], ka_ref[...], bd_ref)
    r_scr[...] = r
    k_scr[...] = k
    v_scr[...] = v
    ld_scr[...] = log_decay
    a_scr[...] = -kk
    b_scr[...] = kk * a_lr

    n2 = 2 * CHUNK
    row = lax.broadcasted_iota(jnp.int32, (n2, n2), 0)
    col = lax.broadcasted_iota(jnp.int32, (n2, n2), 1)
    same = (row >= CHUNK) == (col >= CHUNK)
    rt = jnp.where(row >= CHUNK, row - CHUNK, row)
    ct = jnp.where(col >= CHUNK, col - CHUNK, col)
    strict = same & (rt > ct)
    incl = same & (rt >= ct)
    crow = lax.broadcasted_iota(jnp.int32, (CHUNK, CHUNK), 0)
    ccol = lax.broadcasted_iota(jnp.int32, (CHUNK, CHUNK), 1)
    tri = jnp.where(crow >= ccol, 1.0, 0.0).astype(BF16)
    mask_a = lax.broadcasted_iota(jnp.int32, (CHUNK, LANE), 1) < HEAD_DIM

    n_chunk = tt // CHUNK
    units = [(c, p) for c in range(n_chunk) for p in range(N_PAIR)]
    n_unit = range(len(units))
    dot = functools.partial(jnp.dot, preferred_element_type=F32)

    def nt(a, b):
        return lax.dot_general(a, b, (((1,), (1,)), ((), ())), preferred_element_type=F32)

    def blk(ref, c, p):
        return ref[c * CHUNK:(c + 1) * CHUNK, p * LANE:(p + 1) * LANE]

    for c in range(n_chunk):
        rows = slice(c * CHUNK, (c + 1) * CHUNK)
        cum_scr[rows, :] = _split3_dot_left(tri, ld_scr[rows, :])
    for u, (c, p) in enumerate(units):
        cum = blk(cum_scr, c, p)
        last = cum[CHUNK - 1:CHUNK, :]
        p_inv = jnp.exp(-cum)
        p_end = jnp.exp(last - cum)
        bb = blk(b_scr, c, p)
        kc = blk(k_scr, c, p)
        v_s = _stack_heads(blk(v_scr, c, p), mask_a)
        a_s = _stack_heads(blk(a_scr, c, p) * jnp.exp(cum - blk(ld_scr, c, p)),
                           mask_a).astype(BF16)
        ar_scr[u, 0:n2, :] = a_s
        ar_scr[u, n2:2 * n2, :] = _stack_heads(blk(r_scr, c, p) * jnp.exp(cum), mask_a).astype(BF16)
        bk_scr[u, 0:n2, :] = _stack_heads(bb * p_inv, mask_a).astype(BF16)
        bk_scr[u, n2:2 * n2, :] = _stack_heads(kc * p_inv, mask_a).astype(BF16)
        at_scr[u, :, 0:n2] = a_s
        vs_scr[u] = v_s.astype(BF16)
        vst_scr[u] = v_s.T.astype(BF16)
        bend_scr[u] = _stack_heads(bb * p_end, mask_a).astype(BF16)
        kend_scr[u] = _stack_heads(kc * p_end, mask_a).astype(BF16)
        pt_scr[u] = jnp.broadcast_to(jnp.exp(last), (8, LANE))
    eye = jnp.where(row == col, 1.0, 0.0)
    for u in n_unit:
        quad = nt(ar_scr[u], bk_scr[u])
        a_ab = jnp.where(strict, quad[0:n2, 0:n2], 0.0)
        x_scr[u] = eye + a_ab
        pw_scr[u] = a_ab.astype(BF16)
        akrk_scr[u, 0:n2, :] = jnp.where(strict, quad[0:n2, n2:2 * n2], 0.0).astype(BF16)
        akrk_scr[u, n2:2 * n2, :] = jnp.where(incl, quad[n2:2 * n2, n2:2 * n2], 0.0).astype(BF16)
        arb_scr[u] = jnp.where(incl, quad[n2:2 * n2, 0:n2], 0.0).astype(BF16)
    for _ in range(CHUNK.bit_length() - 2):
        for u in n_unit:
            pw_scr[u] = dot(pw_scr[u], pw_scr[u]).astype(BF16)
        for u in n_unit:
            x_scr[u] = x_scr[u] + dot(x_scr[u].astype(BF16), pw_scr[u])
        fill()
    for u in n_unit:
        ty = dot(akrk_scr[u], vs_scr[u])
        at_scr[u, :, n2:2 * n2] = ty[0:n2, :].astype(BF16)
        yv_scr[u] = ty[n2:2 * n2, :]
    fill()
    for u in n_unit:
        kv_scr[u] = dot(vst_scr[u], kend_scr[u])
    fill(len(fillers))
    for u in n_unit:
        wu = dot(x_scr[u].astype(BF16), at_scr[u])
        wu_scr[u] = wu.astype(BF16)
        wut_scr[u] = wu.T.astype(BF16)
    for u in n_unit:
        gh = dot(wut_scr[u], bend_scr[u])
        g_scr[u] = gh[0:n2, :].astype(BF16)
        h_scr[u] = gh[n2:2 * n2, :] + kv_scr[u]
    for u in n_unit:
        qy = dot(arb_scr[u], wu_scr[u])
        q_scr[u] = (ar_scr[u, n2:2 * n2, :].astype(F32) + qy[:, 0:n2]).astype(BF16)
        y0_scr[u] = qy[:, n2:2 * n2] + yv_scr[u]
    for c in range(n_chunk):
        rows = slice(c * CHUNK, (c + 1) * CHUNK)
        us = [c * N_PAIR + p for p in range(N_PAIR)]
        s0 = [s_scr[p] for p in range(N_PAIR)]
        s0b = [a.astype(BF16) for a in s0]
        for p, u in enumerate(us):
            s_scr[p] = s0[p] * pt_scr[u, 0:1, :] + dot(s0b[p], g_scr[u]) + h_scr[u]
        for p, u in enumerate(us):
            y_s = nt(q_scr[u], s0b[p]) + y0_scr[u]
            y_scr[rows, p * LANE:(p + 1) * LANE] = y_s[0:CHUNK, :] + y_s[CHUNK:n2, :]

    y_ref[...] = _rwkv_finish(y_scr[...], r_scr[...], k_scr[...], v_scr[...], rk_ref[...],
                              gg_ref[...], gb_ref[...], z_scr[:, SHIFT_W:SHIFT_W + BRANCH_W], bd_ref)
    for p in range(N_PAIR):
        st_ref[2 * p] = s_scr[p, 0:HEAD_DIM, 0:HEAD_DIM]
        st_ref[2 * p + 1] = s_scr[p, HEAD_DIM:LANE, HEAD_DIM:LANE]


def _head_block_diag():
    i = jnp.arange(BRANCH_W) // HEAD_DIM
    return (i[:, None] == i[None, :]).astype(BF16)


N_CONV_REFS = (5, 2, 2)
N_RWKV_REFS = (12, 3, 31)


def _conv_rwkv_kernel(x_ref, g_ref, *refs):
    ci, co, cs = N_CONV_REFS
    ri, ro, rs = N_RWKV_REFS
    conv_in, refs = refs[:ci], refs[ci:]
    rwkv_in, refs = refs[:ri], refs[ri:]
    conv_out, refs = refs[:co], refs[co:]
    rwkv_out, refs = refs[:ro], refs[ro:]
    conv_scr, rwkv_scr = refs[:cs], refs[cs:]
    assert len(rwkv_scr) == rs
    tt = x_ref.shape[0]

    @pl.when(pl.program_id(1) == 0)
    def _():
        _conv_init(conv_scr[1], tt)
        _rwkv_init(rwkv_scr[1], rwkv_scr[9])

    h = _rms(x_ref[...], g_ref[...]).astype(BF16)
    conv_chunks, conv_finish = _conv_tile(h, *conv_in, *conv_out, *conv_scr)
    _rwkv_tile(h, conv_chunks, *rwkv_in, *rwkv_out, *rwkv_scr)
    conv_finish()


def _conv_rwkv_call(x, lp):
    b, t, _ = x.shape
    tt = TOK_TILE
    vec = _full((1, BRANCH_W))
    scr = lambda w: pltpu.VMEM((tt, w), F32)
    units = (tt // CHUNK) * N_PAIR
    unit = lambda dt, r=1, c=1: pltpu.VMEM((units, r * LANE, c * LANE), dt)
    tok = pl.BlockSpec((None, tt, BRANCH_W), lambda i, j: (i, j, 0))
    return pl.pallas_call(
        _conv_rwkv_kernel,
        grid=(b, t // tt),
        in_specs=[
            pl.BlockSpec((None, tt, D_MODEL), lambda i, j: (i, j, 0)), _full((1, D_MODEL)),
            _full((D_MODEL, N_CONV_IN)), _full((CONV_W, BRANCH_W)), vec, vec, vec,
            _full((D_MODEL, N_RWKV_IN)), _full((1, SHIFT_W)),
            vec, _full((2 * LORA, BRANCH_W)), vec, _full((2 * LORA, BRANCH_W)), vec, vec,
            vec, vec, vec, _full((BRANCH_W, BRANCH_W)),
        ],
        out_specs=[
            tok,
            pl.BlockSpec((None, CONV_W - 1, BRANCH_W), lambda i, j: (i, 0, 0)),
            tok,
            pl.BlockSpec((None, N_HEAD, HEAD_DIM, HEAD_DIM), lambda i, j: (i, 0, 0, 0)),
            pl.BlockSpec((None, 1, SHIFT_W), lambda i, j: (i, 0, 0)),
        ],
        out_shape=[
            jax.ShapeDtypeStruct((b, t, BRANCH_W), BF16),
            jax.ShapeDtypeStruct((b, CONV_W - 1, BRANCH_W), F32),
            jax.ShapeDtypeStruct((b, t, BRANCH_W), BF16),
            jax.ShapeDtypeStruct((b, N_HEAD, HEAD_DIM, HEAD_DIM), F32),
            jax.ShapeDtypeStruct((b, 1, SHIFT_W), F32),
        ],
        scratch_shapes=[
            scr(N_CONV_IN), pltpu.VMEM((HIST + tt + 8, BRANCH_W), F32),
            scr(N_RWKV_IN), pltpu.VMEM((8 + tt, SHIFT_W), F32),
            scr(BRANCH_W), scr(BRANCH_W), scr(BRANCH_W), scr(BRANCH_W), scr(BRANCH_W), scr(BRANCH_W),
            scr(BRANCH_W), pltpu.VMEM((N_PAIR, LANE, LANE), F32),
            scr(BRANCH_W), unit(BF16, 2), unit(BF16, 2), unit(BF16, 1, 2), *([unit(BF16)] * 5),
            unit(BF16, 2), unit(BF16), unit(BF16, 1, 2), unit(BF16, 2), unit(BF16), unit(BF16),
            *([unit(F32)] * 5), pltpu.VMEM((units, 8, LANE), F32),
        ],
        compiler_params=_params("parallel", "arbitrary"),
        name="conv_rwkv_branches",
    )(x, lp["norm_g"], lp["w_conv"], lp["conv_w"], lp["conv_b"], lp["conv_ln_g"], lp["conv_ln_b"],
      lp["w_rwkv"], lp["shift_mu"], lp["w0"], lp["w_lora_b"], lp["a0"], lp["a_lora_b"],
      lp["k_k"], lp["k_a"], lp["r_k"], lp["gn_g"], lp["gn_b"], lp["head_bd"])


def _merge_kernel(x_ref, yc_ref, ya_ref, yr_ref, pe_ref, g_ref, wm_ref, wb_ref, wo_ref, pg_ref,
                  pp_ref, fg_ref, o_ref, mg_scr, *, att_transposed, final_norm):
    x = x_ref[...]
    mg_scr[...] = _bdot(_rms(x, g_ref[...]), wm_ref[...])
    ya = ya_ref[...].T if att_transposed else ya_ref[...]
    s = (_sigmoid(mg_scr[:, 0:D_MODEL]) * _bdot(yc_ref[...], wb_ref[0])
         + _sigmoid(mg_scr[:, D_MODEL:2 * D_MODEL]) * _bdot(ya, wb_ref[1])
         + _sigmoid(mg_scr[:, 2 * D_MODEL:3 * D_MODEL]) * _bdot(yr_ref[...], wb_ref[2]))
    x = x + _bdot(s, wo_ref[...])
    x = x + _sigmoid(_bdot(x, pg_ref[...])) * _bdot(pe_ref[...], pp_ref[...])
    o_ref[...] = _rms(x, fg_ref[...]) if final_norm else x


def _merge_call(x, y_conv, y_att, y_rw, pe, lp, final_g, *, att_transposed, final_norm):
    b, t, _ = x.shape
    tm = min(2 * TOK_TILE, t)
    tok = lambda i, j: (i, j, 0)
    att_spec = (pl.BlockSpec((None, BRANCH_W, tm), lambda i, j: (i, 0, j)) if att_transposed
                else pl.BlockSpec((None, tm, BRANCH_W), tok))
    return pl.pallas_call(
        functools.partial(_merge_kernel, att_transposed=att_transposed, final_norm=final_norm),
        grid=(b, t // tm),
        in_specs=[
            pl.BlockSpec((None, tm, D_MODEL), tok),
            pl.BlockSpec((None, tm, BRANCH_W), tok),
            att_spec,
            pl.BlockSpec((None, tm, BRANCH_W), tok),
            pl.BlockSpec((None, tm, P_DIM), tok),
            _full((1, D_MODEL)), _full((D_MODEL, N_MERGE_IN)), _full((3, BRANCH_W, D_MODEL)),
            _full((D_MODEL, D_MODEL)), _full((D_MODEL, D_MODEL)), _full((P_DIM, D_MODEL)),
            _full((1, D_MODEL)),
        ],
        out_specs=pl.BlockSpec((None, tm, D_MODEL), tok),
        out_shape=jax.ShapeDtypeStruct((b, t, D_MODEL), F32),
        scratch_shapes=[pltpu.VMEM((tm, N_MERGE_IN), F32)],
        compiler_params=_params("parallel", "parallel"),
        name="merge",
    )(x, y_conv, y_att, y_rw, pe, lp["norm_g"], lp["w_merge"], lp["w_branch"], lp["w_out"],
      lp["ple_gate"], lp["ple_proj"], final_g)


def _proj_kernel(x_ref, g_ref, w_ref, z_ref):
    z_ref[...] = _bdot(_rms(x_ref[...], g_ref[...]), w_ref[...])


def _proj_call(x, norm_g, w_in):
    m = x.shape[0]
    n_tiles = 3
    tn = N_IN // n_tiles
    return pl.pallas_call(
        _proj_kernel,
        grid=(n_tiles,),
        in_specs=[_full((m, D_MODEL)), _full((1, D_MODEL)),
                  pl.BlockSpec((D_MODEL, tn), lambda j: (0, j))],
        out_specs=pl.BlockSpec((m, tn), lambda j: (0, j)),
        out_shape=jax.ShapeDtypeStruct((m, N_IN), F32),
        compiler_params=_params("parallel"),
        name="decode_project",
    )(x, norm_g, w_in)


def _row_to_col(row_vec):
    n = row_vec.shape[1]
    eye = lax.broadcasted_iota(jnp.int32, (n, n), 0) == lax.broadcasted_iota(jnp.int32, (n, n), 1)
    return jnp.sum(jnp.where(eye, row_vec, 0.0), axis=1, keepdims=True)


def _col_to_row(col_vec):
    n = col_vec.shape[0]
    eye = lax.broadcasted_iota(jnp.int32, (n, n), 0) == lax.broadcasted_iota(jnp.int32, (n, n), 1)
    return jnp.sum(jnp.where(eye, col_vec, 0.0), axis=0, keepdims=True)


def _mix_kernel(z_ref, buf_ref, st_ref, sh0_ref, cw_ref, cb_ref, lg_ref, lb_ref, cos_ref, sa_ref,
                sb_ref, mu_ref, w0_ref, wlb_ref, a0_ref, alb_ref, kk_ref, ka_ref, rk_ref, gg_ref,
                gb_ref, bd_ref,
                yc_ref, cs_ref, q_ref, k_ref, v_ref, sg_ref, yr_ref, so_ref, sho_ref, ext_scr):
    o_att = N_CONV_IN
    o_rw = N_CONV_IN + N_ATT_IN
    u = z_ref[:, 0:BRANCH_W] * _sigmoid(z_ref[:, BRANCH_W:2 * BRANCH_W])
    ext_scr[0:8, :] = jnp.zeros((8, BRANCH_W), F32)
    ext_scr[2:HIST, :] = buf_ref[...]
    ext_scr[HIST:HIST + 16, :] = jnp.broadcast_to(u, (16, BRANCH_W))
    yc_ref[...] = _conv_branch(ext_scr, HIST, 8, cw_ref, cb_ref[...], lg_ref[...], lb_ref[...],
                               z_ref[:, 2 * BRANCH_W:3 * BRANCH_W])[0:1, :]
    cs_ref[...] = ext_scr[3:HIST + 1, :]
    n = BRANCH_W // LANE
    cos, sa, sb = (_tile_lanes(r[...], n) for r in (cos_ref, sa_ref, sb_ref))
    q_ref[...] = _rope(z_ref[:, o_att:o_att + BRANCH_W], cos, sa, sb) * (HEAD_DIM ** -0.5)
    k_ref[...] = _rope(z_ref[:, o_att + BRANCH_W:o_att + 2 * BRANCH_W], cos, sa, sb)
    v_ref[...] = z_ref[:, o_att + 2 * BRANCH_W:o_att + 3 * BRANCH_W]
    sg_ref[...] = _silu(z_ref[:, o_att + 3 * BRANCH_W:o_att + 4 * BRANCH_W])
    cur = z_ref[:, o_rw:o_rw + SHIFT_W]
    sho_ref[...] = cur
    mixed = cur + (sh0_ref[...] - cur) * mu_ref[...]
    r, k, v, log_decay, kk, a_lr = _rwkv_columns(
        mixed, w0_ref[...], wlb_ref, a0_ref[...], alb_ref, kk_ref[...], ka_ref[...], bd_ref)
    decay = jnp.exp(log_decay)
    b_vec = kk * a_lr
    ys = []
    for h in range(N_HEAD):
        lanes = slice(h * HEAD_DIM, (h + 1) * HEAD_DIM)
        s = st_ref[h]
        sa_col = jnp.sum(s * (-kk[:, lanes]), axis=1, keepdims=True)
        s = s * decay[:, lanes] + sa_col * b_vec[:, lanes] + _row_to_col(v[:, lanes]) * k[:, lanes]
        so_ref[h] = s
        ys.append(_col_to_row(jnp.sum(s * r[:, lanes], axis=1, keepdims=True)))
    y = jnp.concatenate(ys, axis=1)
    yr_ref[...] = _rwkv_finish(y, r, k, v, rk_ref[...], gg_ref[...], gb_ref[...],
                               z_ref[:, o_rw + SHIFT_W:o_rw + SHIFT_W + BRANCH_W], bd_ref)


def _mix_call(z, buf, state, shift0, lp, tables):
    b = z.shape[0]
    row = lambda w: pl.BlockSpec((None, 1, w), lambda i: (i, 0, 0))
    vec = _full((1, BRANCH_W))
    tab = _full((1, LANE))
    o = lambda w, dt=F32: jax.ShapeDtypeStruct((b, 1, w), dt)
    return pl.pallas_call(
        _mix_kernel,
        grid=(b,),
        in_specs=[
            row(N_IN),
            pl.BlockSpec((None, CONV_W - 1, BRANCH_W), lambda i: (i, 0, 0)),
            pl.BlockSpec((None, N_HEAD, HEAD_DIM, HEAD_DIM), lambda i: (i, 0, 0, 0)),
            row(SHIFT_W),
            _full((CONV_W, BRANCH_W)), vec, vec, vec, tab, tab, tab,
            _full((1, SHIFT_W)), vec, _full((2 * LORA, BRANCH_W)), vec, _full((2 * LORA, BRANCH_W)),
            vec, vec, vec, vec, vec, _full((BRANCH_W, BRANCH_W)),
        ],
        out_specs=[
            row(BRANCH_W),
            pl.BlockSpec((None, CONV_W - 1, BRANCH_W), lambda i: (i, 0, 0)),
            row(BRANCH_W), row(BRANCH_W), row(BRANCH_W), row(BRANCH_W), row(BRANCH_W),
            pl.BlockSpec((None, N_HEAD, HEAD_DIM, HEAD_DIM), lambda i: (i, 0, 0, 0)),
            row(SHIFT_W),
        ],
        out_shape=[
            o(BRANCH_W, BF16), jax.ShapeDtypeStruct((b, CONV_W - 1, BRANCH_W), F32),
            o(BRANCH_W), o(BRANCH_W), o(BRANCH_W), o(BRANCH_W), o(BRANCH_W, BF16),
            jax.ShapeDtypeStruct((b, N_HEAD, HEAD_DIM, HEAD_DIM), F32), o(SHIFT_W),
        ],
        scratch_shapes=[pltpu.VMEM((HIST + 16, BRANCH_W), F32)],
        compiler_params=_params("parallel"),
        name="decode_mix",
    )(z.reshape(b, 1, N_IN), buf, state, shift0.reshape(b, 1, SHIFT_W),
      lp["conv_w"], lp["conv_b"], lp["conv_ln_g"], lp["conv_ln_b"], *tables,
      lp["shift_mu"], lp["w0"], lp["w_lora_b"], lp["a0"], lp["a_lora_b"], lp["k_k"], lp["k_a"],
      lp["r_k"], lp["gn_g"], lp["gn_b"], lp["head_bd"])


def _head_rows(q_row):
    lane = lax.broadcasted_iota(jnp.int32, (N_HEAD, BRANCH_W), 1)
    row = lax.broadcasted_iota(jnp.int32, (N_HEAD, BRANCH_W), 0)
    return jnp.where((lane >= row * HEAD_DIM) & (lane < (row + 1) * HEAD_DIM), q_row, 0.0)


def _score_kernel(pt_ref, q_ref, *refs):
    del pt_ref
    pages = refs[:PAGES_PER_STEP]
    s_ref, gate_ref = refs[PAGES_PER_STEP:]
    j = pl.program_id(1)
    page_rows = pages[0].shape[1]

    @pl.when(j == 0)
    def _():
        gate_ref[...] = jnp.zeros(gate_ref.shape, F32)

    qh = _head_rows(q_ref[...])
    lane = lax.broadcasted_iota(jnp.int32, gate_ref.shape, 1)
    pages_per_block = MOBA_BLOCK // page_rows
    gate = gate_ref[...]
    for r in range(PAGES_PER_STEP):
        s = _bdot(qh, pages[r][...])
        s_ref[:, r * page_rows:(r + 1) * page_rows] = s
        blk = (j * PAGES_PER_STEP + r) // pages_per_block
        gate = gate + jnp.where(lane == blk, jnp.sum(s, axis=1, keepdims=True), 0.0)
    gate_ref[...] = gate


def _score_call(page_table, q, cache, layer, n_pool):
    b, n_pages = page_table.shape
    page_rows = cache.shape[2]
    past = n_pages * page_rows
    assert n_pages % PAGES_PER_STEP == 0 and MOBA_BLOCK % page_rows == 0
    assert past // MOBA_BLOCK <= LANE

    def page_spec(r):
        return pl.BlockSpec(
            (None, BRANCH_W, page_rows),
            lambda i, j, pt: (layer * n_pool + pt[i, j * PAGES_PER_STEP + r], 0, 0))

    grid_spec = pltpu.PrefetchScalarGridSpec(
        num_scalar_prefetch=1,
        grid=(b, n_pages // PAGES_PER_STEP),
        in_specs=[pl.BlockSpec((None, 1, BRANCH_W), lambda i, j, pt: (i, 0, 0))]
        + [page_spec(r) for r in range(PAGES_PER_STEP)],
        out_specs=[
            pl.BlockSpec((None, N_HEAD, PAGES_PER_STEP * page_rows), lambda i, j, pt: (i, 0, j)),
            pl.BlockSpec((None, N_HEAD, LANE), lambda i, j, pt: (i, 0, 0)),
        ],
    )
    return pl.pallas_call(
        _score_kernel,
        grid_spec=grid_spec,
        out_shape=[jax.ShapeDtypeStruct((b, N_HEAD, past), F32),
                   jax.ShapeDtypeStruct((b, N_HEAD, LANE), F32)],
        compiler_params=_params("parallel", "arbitrary"),
        name="decode_scores",
    )(page_table, q, *([cache] * PAGES_PER_STEP))


def _select_kernel(s_ref, gate_ref, q_ref, k_ref, p_ref, idx_ref, pown_ref, *, n_blocks):
    lane = lax.broadcasted_iota(jnp.int32, gate_ref.shape, 1).astype(F32)
    gate = jnp.where(lane < n_blocks, gate_ref[...], -jnp.inf)
    key_blk = jnp.right_shift(lax.broadcasted_iota(jnp.int32, s_ref.shape, 1),
                              MOBA_BLOCK.bit_length() - 1).astype(F32)
    sel = jnp.zeros(s_ref.shape, F32)
    idx_out = jnp.zeros(gate_ref.shape, F32)
    for j in range(MOBA_TOPK):
        m = jnp.max(gate, axis=1, keepdims=True)
        idx = jnp.min(jnp.where(gate == m, lane, float(LANE)), axis=1, keepdims=True)
        sel = jnp.where(key_blk == idx, 1.0, sel)
        idx_out = jnp.where(lane == j, idx, idx_out)
        gate = jnp.where(lane == idx, -jnp.inf, gate)
    s_own = jnp.sum(_head_rows(q_ref[...]) * k_ref[...], axis=1, keepdims=True)
    s = jnp.where(sel > 0.0, s_ref[...], NEG)
    m = jnp.maximum(jnp.max(s, axis=1, keepdims=True), s_own)
    e = jnp.exp(s - m)
    e_own = jnp.exp(s_own - m)
    l = jnp.sum(e, axis=1, keepdims=True) + e_own
    p_ref[...] = e / l
    pown_ref[...] = jnp.broadcast_to(e_own / l, pown_ref.shape)
    idx_ref[...] = idx_out.astype(jnp.int32)


def _select_call(scores, gate, q, k_new):
    b, _, past = scores.shape
    n_blocks = past // MOBA_BLOCK
    assert n_blocks >= MOBA_TOPK
    head = lambda w: pl.BlockSpec((None, N_HEAD, w), lambda i: (i, 0, 0))
    row = pl.BlockSpec((None, 1, BRANCH_W), lambda i: (i, 0, 0))
    return pl.pallas_call(
        functools.partial(_select_kernel, n_blocks=n_blocks),
        grid=(b,),
        in_specs=[head(past), head(LANE), row, row],
        out_specs=[head(past), head(LANE), head(LANE)],
        out_shape=[jax.ShapeDtypeStruct((b, N_HEAD, past), F32),
                   jax.ShapeDtypeStruct((b, N_HEAD, LANE), jnp.int32),
                   jax.ShapeDtypeStruct((b, N_HEAD, LANE), F32)],
        compiler_params=_params("parallel"),
        name="decode_select",
    )(scores, gate, q, k_new)


def _gather_kernel(pt_ref, ix_ref, pown_ref, vnew_ref, sg_ref, *refs, n_sel):
    del pt_ref, ix_ref
    n = GATHER_HEADS * n_sel
    p_rows = refs[:n]
    v_pages = refs[n:2 * n]
    y_ref = refs[2 * n]
    for hh in range(GATHER_HEADS):
        h = pl.program_id(1) * GATHER_HEADS + hh
        acc = pown_ref[pl.ds(h, 1), 0:1] * vnew_ref[hh]
        for r in range(hh * n_sel, (hh + 1) * n_sel):
            p8 = jnp.broadcast_to(p_rows[r][pl.ds(h, 1), :], (8, p_rows[r].shape[1]))
            acc = acc + _bdot_nt(p8, v_pages[r][...])[0:1, :]
        y_ref[hh] = acc * sg_ref[hh]


def _gather_call(page_table, idx, probs, p_own, v_new, sg, cache, layer, n_pool):
    b, n_pages = page_table.shape
    page_rows = cache.shape[2]
    ppb = MOBA_BLOCK // page_rows
    n_sel = MOBA_TOPK * ppb

    def seq_page(i, h, ix, r):
        return ix[i, h * MOBA_TOPK + r // ppb] * ppb + r % ppb

    def p_spec(hh, r):
        return pl.BlockSpec(
            (None, N_HEAD, page_rows),
            lambda i, g, pt, ix: (i, 0, seq_page(i, g * GATHER_HEADS + hh, ix, r)))

    def v_spec(hh, r):
        def index(i, g, pt, ix):
            h = g * GATHER_HEADS + hh
            return layer * n_pool + pt[i, seq_page(i, h, ix, r)], h, 0
        return pl.BlockSpec((None, HEAD_DIM, page_rows), index)

    by_head = lambda a: a.reshape(b, N_HEAD, 1, HEAD_DIM)
    rows = pl.BlockSpec((None, GATHER_HEADS, 1, HEAD_DIM), lambda i, g, pt, ix: (i, g, 0, 0))
    slots = [(hh, r) for hh in range(GATHER_HEADS) for r in range(n_sel)]
    grid_spec = pltpu.PrefetchScalarGridSpec(
        num_scalar_prefetch=2,
        grid=(b, N_HEAD // GATHER_HEADS),
        in_specs=[pl.BlockSpec((None, N_HEAD, LANE), lambda i, g, pt, ix: (i, 0, 0)), rows, rows]
        + [p_spec(*s) for s in slots] + [v_spec(*s) for s in slots],
        out_specs=rows,
    )
    return pl.pallas_call(
        functools.partial(_gather_kernel, n_sel=n_sel),
        grid_spec=grid_spec,
        out_shape=jax.ShapeDtypeStruct((b, N_HEAD, 1, HEAD_DIM), F32),
        compiler_params=_params("parallel", "parallel"),
        name="decode_gather",
    )(page_table, idx, p_own, by_head(v_new), by_head(sg),
      *([probs] * len(slots)), *([cache] * len(slots)))


def _layer_params(i, norm_g, w_in, conv_w, conv_b, conv_ln_g, conv_ln_b, shift_mu, w0, w_lora_b, a0,
                  a_lora_b, k_k, k_a, r_k, gn_g, gn_b, w_branch, w_out, ple_proj, ple_gate):
    w = w_in[i].astype(BF16)
    o1, o2, o3 = N_CONV_IN, N_CONV_IN + N_ATT_IN, N_CONV_IN + N_ATT_IN + N_RWKV_IN
    row = lambda a: a[i].reshape(1, -1)
    zeros = jnp.zeros((LORA, BRANCH_W), F32)
    return dict(
        norm_g=row(norm_g), w_in=w, w_conv=w[:, :o1], w_att=w[:, o1:o2], w_rwkv=w[:, o2:o3],
        w_merge=w[:, o3:],
        conv_w=conv_w[i], conv_b=row(conv_b), conv_ln_g=row(conv_ln_g), conv_ln_b=row(conv_ln_b),
        shift_mu=row(shift_mu), w0=row(w0), a0=row(a0),
        w_lora_b=jnp.concatenate([w_lora_b[i], zeros], axis=0).astype(BF16),
        a_lora_b=jnp.concatenate([zeros, a_lora_b[i]], axis=0).astype(BF16),
        k_k=row(k_k), k_a=row(k_a), r_k=row(r_k), gn_g=row(gn_g), gn_b=row(gn_b),
        w_branch=w_branch[i].astype(BF16), w_out=w_out[i].astype(BF16),
        ple_proj=ple_proj[i].astype(BF16), ple_gate=ple_gate[i].astype(BF16),
        head_bd=_head_block_diag(),
    )


def _prompt_layer(x, pe, lp, tables, final_g, final_norm, layer, depth, kv_t):
    b, t, _ = x.shape
    nb = t // MOBA_BLOCK
    y_conv, conv_new, y_rw, wkv, shift = _conv_rwkv_call(x, lp)
    k_t, v_t, qt, kb, vt, sgt, km = _att1_call(x, lp["norm_g"], lp["w_att"], tables, layer, depth, kv_t)
    km = km.reshape(b, nb, N_HEAD, HEAD_DIM).transpose(0, 2, 1, 3)
    y_att_t = _att2_call(qt, kb, vt, km, sgt)
    x = _merge_call(x, y_conv, y_att_t, y_rw, pe, lp, final_g,
                    att_transposed=True, final_norm=final_norm)
    return x, (k_t, v_t), conv_new, wkv, shift.reshape(b, SHIFT_W)


def _sample_layer(x, pe, lp, tables, final_g, final_norm, layer, n_pool, cache_k, cache_v,
                  page_table, buf, state, shift0):
    b = x.shape[0]
    z = _proj_call(x.reshape(b, D_MODEL), lp["norm_g"], lp["w_in"])
    y_conv, conv_new, q, k, v, sg, y_rw, wkv, shift = _mix_call(z, buf, state, shift0, lp, tables)
    scores, gate = _score_call(page_table, q, cache_k, layer, n_pool)
    probs, idx, p_own = _select_call(scores, gate, q, k)
    idx = idx[:, :, :MOBA_TOPK].reshape(b, N_HEAD * MOBA_TOPK)
    y_att = _gather_call(page_table, idx, probs, p_own, v, sg, cache_v, layer, n_pool)
    tok = lambda a: a.reshape(1, b, -1)
    x = _merge_call(tok(x), tok(y_conv), tok(y_att), tok(y_rw), tok(pe), lp, final_g,
                    att_transposed=False, final_norm=final_norm)
    return (x.reshape(b, 1, D_MODEL), k, v, conv_new, wkv, shift.reshape(b, SHIFT_W))


def kernel(x_prompt, x_sample, cache_k, cache_v, page_table, state_conv, state_wkv, state_shift,
           p_prompt, p_sample, norm_g, w_in, conv_w, conv_b, conv_ln_g, conv_ln_b, shift_mu, w0,
           w_lora_b, a0, a_lora_b, k_k, k_a, r_k, gn_g, gn_b, w_branch, w_out, ple_proj, ple_gate,
           final_norm_g):
    depth = w_in.shape[0]
    b_p, t_p, _ = x_prompt.shape
    b_s, t_s, _ = x_sample.shape
    assert t_s == 1 and t_p % TOK_TILE == 0 and TOK_TILE == MOBA_BLOCK
    n_pool, page_rows = cache_k.shape[1], cache_k.shape[2]
    past_len = page_table.shape[1] * page_rows
    assert past_len % MOBA_BLOCK == 0
    tables_p = _rope_tables(jnp.arange(t_p, dtype=jnp.int32))
    tables_s = _rope_tables(past_len + jnp.arange(1, dtype=jnp.int32))
    by_page = lambda c: c.transpose(0, 1, 3, 4, 2).reshape(depth * n_pool, BRANCH_W, page_rows)
    cache_k, cache_v = by_page(cache_k), by_page(cache_v)
    final_g = final_norm_g.reshape(1, D_MODEL)
    xp, xs = x_prompt, x_sample
    outs = [[] for _ in range(8)]
    kv_t = None
    for i in range(depth):
        lp = _layer_params(i, norm_g, w_in, conv_w, conv_b, conv_ln_g, conv_ln_b, shift_mu, w0,
                           w_lora_b, a0, a_lora_b, k_k, k_a, r_k, gn_g, gn_b, w_branch, w_out,
                           ple_proj, ple_gate)
        last = i == depth - 1
        xp, kv_t, cp, wp, sp = _prompt_layer(xp, p_prompt[i], lp, tables_p, final_g, last, i,
                                             depth, kv_t)
        xs, ks, vs, cs, ws, ss = _sample_layer(
            xs, p_sample[i], lp, tables_s, final_g, last, i, n_pool, cache_k, cache_v, page_table,
            state_conv[i], state_wkv[i], state_shift[i])
        heads_s = lambda a: a.reshape(b_s, 1, N_HEAD, HEAD_DIM)
        for lst, a in zip(outs, (heads_s(ks), heads_s(vs), cp, cs, wp, ws, sp, ss)):
            lst.append(a)
    heads_p = lambda a: a.reshape(depth, b_p, N_HEAD, HEAD_DIM, t_p).transpose(0, 1, 4, 2, 3)
    return (xp, xs, heads_p(kv_t[0]), heads_p(kv_t[1])) + tuple(jnp.stack(lst) for lst in outs)
```
